```python
import math
import jax, jax.numpy as jnp
from jax import lax
import numpy as np

D_MODEL = 1024
BATCH = 8
SEQ = 8192
DEPTH = 4
DEC_BATCH = 16
DEC_SEQ = 2048
PAST_LEN = 128

N_MEM = 256
Q_BLOCK = 128
EPS = 1e-6
N_MIXERS = 2
N_LAYERS_A = (DEPTH + 1) // 2
N_LAYERS_B = DEPTH // 2
MLA_HEADS = 8
MLA_Q_LORA = 384
MLA_KV_LORA = 256
MLA_NOPE = 64
MLA_ROPE = 32
MLA_QK = MLA_NOPE + MLA_ROPE
MLA_V = 64
ROPE_THETA = 10000.0
DIFF_HEADS = 8
DIFF_HD = 64
XA_HEADS = 4
XA_HD = D_MODEL // XA_HEADS
D_FF = 2816
CONV_W = 3

kernel_name = 'hybrid_mla_diffattn_convglu_encoder'


def _rmsnorm(x, g):
    x32 = x.astype(jnp.float32)
    y = x32 * lax.rsqrt(jnp.mean(x32 * x32, axis=-1, keepdims=True) + EPS)
    return (y * g.astype(jnp.float32)).astype(x.dtype)


def _sweep_query_blocks(fn, q):
    b, s = q.shape[:2]
    nb = s // Q_BLOCK
    qb = jnp.moveaxis(q.reshape((b, nb, Q_BLOCK) + q.shape[2:]), 1, 0)
    out = lax.map(lambda a: fn(a[0], a[1]), (qb, jnp.arange(nb)))
    out = jnp.moveaxis(out, 0, 1)
    return out.reshape((b, s) + out.shape[3:])


def _rope_tables(seq):
    inv = ROPE_THETA ** (-jnp.arange(0, MLA_ROPE, 2, dtype=jnp.float32) / MLA_ROPE)
    ang = jnp.arange(seq, dtype=jnp.float32)[:, None] * inv[None, :]
    ang = jnp.concatenate([ang, ang], axis=-1)
    return jnp.cos(ang), jnp.sin(ang)


def _apply_rope(x, cos, sin):
    x32 = x.astype(jnp.float32)
    half = MLA_ROPE // 2
    rot = jnp.concatenate([-x32[..., half:], x32[..., :half]], axis=-1)
    return (x32 * cos[None, :, None, :] + rot * sin[None, :, None, :]).astype(x.dtype)


def _alibi_slopes(n):
    return 2.0 ** (-8.0 * jnp.arange(1, n + 1, dtype=jnp.float32) / n)


def _mla(x, norm_g, w_down, q_lat_g, kv_lat_g, w_uq, w_ukv, q_g, k_g, w_o):
    b, s, _ = x.shape
    h = _rmsnorm(x, norm_g)
    down = h @ w_down
    c_q, c_kv, k_rope = jnp.split(down, [MLA_Q_LORA, MLA_Q_LORA + MLA_KV_LORA], axis=-1)
    c_q = _rmsnorm(c_q, q_lat_g)
    c_kv = _rmsnorm(c_kv, kv_lat_g)
    q = (c_q @ w_uq).reshape(b, s, MLA_HEADS, MLA_QK)
    kv = (c_kv @ w_ukv).reshape(b, s, MLA_HEADS, MLA_NOPE + MLA_V)
    k_nope, v = jnp.split(kv, [MLA_NOPE], axis=-1)
    k_rope = jnp.broadcast_to(k_rope[:, :, None, :], (b, s, MLA_HEADS, MLA_ROPE))
    k = jnp.concatenate([k_nope, k_rope], axis=-1)
    q = _rmsnorm(q, q_g)
    k = _rmsnorm(k, k_g)
    cos, sin = _rope_tables(s)
    q = jnp.concatenate([q[..., :MLA_NOPE], _apply_rope(q[..., MLA_NOPE:], cos, sin)], axis=-1)
    k = jnp.concatenate([k[..., :MLA_NOPE], _apply_rope(k[..., MLA_NOPE:], cos, sin)], axis=-1)
    scale = MLA_QK ** -0.5

    def block(qb, bi):
        sc = jnp.einsum('bqhd,bkhd->bhqk', qb, k).astype(jnp.float32) * scale
        p = jax.nn.softmax(sc, axis=-1).astype(v.dtype)
        return jnp.einsum('bhqk,bkhe->bqhe', p, v)

    o = _sweep_query_blocks(block, q)
    return o.reshape(b, s, MLA_HEADS * MLA_V) @ w_o


def _diff_attn(x, layer_idx, norm_g, w_qkv, q_g, k_g, lam_p, sub_g, w_o):
    b, s, _ = x.shape
    lambda_init = 0.8 - 0.6 * math.exp(-0.3 * layer_idx)
    h = _rmsnorm(x, norm_g)
    q, k, v = jnp.split(h @ w_qkv, 3, axis=-1)
    q = _rmsnorm(q.reshape(b, s, DIFF_HEADS, 2, DIFF_HD), q_g)
    k = _rmsnorm(k.reshape(b, s, DIFF_HEADS, 2, DIFF_HD), k_g)
    v = v.reshape(b, s, DIFF_HEADS, 2 * DIFF_HD)
    lp = lam_p.astype(jnp.float32)
    lam = jnp.exp(jnp.sum(lp[0] * lp[1])) - jnp.exp(jnp.sum(lp[2] * lp[3])) + lambda_init
    slopes = _alibi_slopes(DIFF_HEADS)
    tk = jnp.arange(s)
    scale = DIFF_HD ** -0.5

    def block(qb, bi):
        tq = bi * Q_BLOCK + jnp.arange(Q_BLOCK)
        dist = jnp.abs(tq[:, None] - tk[None, :]).astype(jnp.float32)
        bias = -slopes[:, None, None] * dist[None]
        sc = jnp.einsum('bqhcd,bkhcd->bchqk', qb, k).astype(jnp.float32) * scale + bias
        p = jax.nn.softmax(sc, axis=-1)
        a = (p[:, 0] - lam * p[:, 1]).astype(v.dtype)
        return jnp.einsum('bhqk,bkhe->bqhe', a, v)

    o = _sweep_query_blocks(block, q)
    o = _rmsnorm(o, sub_g) * (1.0 - lambda_init)
    return o.reshape(b, s, DIFF_HEADS * 2 * DIFF_HD) @ w_o


def _mem_xattn(x, mem, norm_g, mem_g, w_q, w_kv, q_g, k_g, w_o):
    b, s, _ = x.shape
    m = mem.shape[1]
    q = (_rmsnorm(x, norm_g) @ w_q).reshape(b, s, XA_HEADS, XA_HD)
    k, v = jnp.split(_rmsnorm(mem, mem_g) @ w_kv, 2, axis=-1)
    k = k.reshape(b, m, XA_HEADS, XA_HD)
    v = v.reshape(b, m, XA_HEADS, XA_HD)
    q = _rmsnorm(q, q_g)
    k = _rmsnorm(k, k_g)
    sc = jnp.einsum('bqhd,bmhd->bhqm', q, k).astype(jnp.float32) * (XA_HD ** -0.5)
    p = jax.nn.softmax(sc, axis=-1).astype(v.dtype)
    o = jnp.einsum('bhqm,bmhd->bqhd', p, v).reshape(b, s, XA_HEADS * XA_HD)
    return o @ w_o


def _conv_glu(x, norm_g, w_gu, conv_w, conv_b, w_down):
    s = x.shape[1]
    h = _rmsnorm(x, norm_g)
    g, u = jnp.split(h @ w_gu, 2, axis=-1)
    pad = CONV_W // 2
    gp = jnp.pad(g, ((0, 0), (pad, pad), (0, 0)))
    g = sum(gp[:, j:j + s] * conv_w[j] for j in range(CONV_W)) + conv_b
    return (jax.nn.silu(g) * u) @ w_down


def _trunk(x, mem, mla_p, diff_p, xa_p, ffn_p):
    for i in range(DEPTH):
        j = i // N_MIXERS
        if i % N_MIXERS == 0:
            x = x + _mla(x, *[p[j] for p in mla_p])
        else:
            x = x + _diff_attn(x, i, *[p[j] for p in diff_p])
        x = x + _mem_xattn(x, mem, *[p[i] for p in xa_p])
        x = x + _conv_glu(x, *[p[i] for p in ffn_p])
    return x


def setup_inputs(seed: int = 0) -> dict:
    key = jax.random.key(seed)
    keys = iter(jax.random.split(key, 32))
    f32 = jnp.float32

    def w(shape, fan_in):
        return jax.random.normal(next(keys), shape, f32) * (fan_in ** -0.5)

    def g(shape):
        return 1.0 + 0.02 * jax.random.normal(next(keys), shape, f32)

    D = D_MODEL
    A, B = N_LAYERS_A, N_LAYERS_B
    return {
        'x_prompt': jax.random.normal(next(keys), (BATCH, SEQ, D), f32),
        'x_sample': jax.random.normal(next(keys), (DEC_BATCH, DEC_SEQ, D), f32),
        'mem_prompt': jax.random.normal(next(keys), (BATCH, N_MEM, D), f32),
        'mem_sample': jax.random.normal(next(keys), (DEC_BATCH, N_MEM, D), f32),
        'mla_norm': g((A, D)),
        'mla_w_down': w((A, D, MLA_Q_LORA + MLA_KV_LORA + MLA_ROPE), D),
        'mla_q_lat_norm': g((A, MLA_Q_LORA)),
        'mla_kv_lat_norm': g((A, MLA_KV_LORA)),
        'mla_w_uq': w((A, MLA_Q_LORA, MLA_HEADS * MLA_QK), MLA_Q_LORA),
        'mla_w_ukv': w((A, MLA_KV_LORA, MLA_HEADS * (MLA_NOPE + MLA_V)), MLA_KV_LORA),
        'mla_q_norm': g((A, MLA_QK)),
        'mla_k_norm': g((A, MLA_QK)),
        'mla_w_o': w((A, MLA_HEADS * MLA_V, D), MLA_HEADS * MLA_V),
        'diff_norm': g((B, D)),
        'diff_w_qkv': w((B, D, 3 * DIFF_HEADS * 2 * DIFF_HD), D),
        'diff_q_norm': g((B, 2, DIFF_HD)),
        'diff_k_norm': g((B, 2, DIFF_HD)),
        'diff_lambda': 0.1 * jax.random.normal(next(keys), (B, 4, DIFF_HD), f32),
        'diff_sub_norm': g((B, 2 * DIFF_HD)),
        'diff_w_o': w((B, DIFF_HEADS * 2 * DIFF_HD, D), DIFF_HEADS * 2 * DIFF_HD),
        'xa_norm': g((DEPTH, D)),
        'xa_mem_norm': g((DEPTH, D)),
        'xa_w_q': w((DEPTH, D, XA_HEADS * XA_HD), D),
        'xa_w_kv': w((DEPTH, D, 2 * XA_HEADS * XA_HD), D),
        'xa_q_norm': g((DEPTH, XA_HD)),
        'xa_k_norm': g((DEPTH, XA_HD)),
        'xa_w_o': w((DEPTH, XA_HEADS * XA_HD, D), XA_HEADS * XA_HD),
        'ffn_norm': g((DEPTH, D)),
        'ffn_w_gu': w((DEPTH, D, 2 * D_FF), D),
        'ffn_conv_w': w((DEPTH, CONV_W, D_FF), CONV_W),
        'ffn_conv_b': 0.02 * jax.random.normal(next(keys), (DEPTH, D_FF), f32),
        'ffn_w_down': w((DEPTH, D_FF, D), D_FF),
    }


def reference(x_prompt, x_sample, mem_prompt, mem_sample,
              mla_norm, mla_w_down, mla_q_lat_norm, mla_kv_lat_norm, mla_w_uq, mla_w_ukv,
              mla_q_norm, mla_k_norm, mla_w_o,
              diff_norm, diff_w_qkv, diff_q_norm, diff_k_norm, diff_lambda, diff_sub_norm, diff_w_o,
              xa_norm, xa_mem_norm, xa_w_q, xa_w_kv, xa_q_norm, xa_k_norm, xa_w_o,
              ffn_norm, ffn_w_gu, ffn_conv_w, ffn_conv_b, ffn_w_down):
    mla_p = (mla_norm, mla_w_down, mla_q_lat_norm, mla_kv_lat_norm, mla_w_uq, mla_w_ukv,
             mla_q_norm, mla_k_norm, mla_w_o)
    diff_p = (diff_norm, diff_w_qkv, diff_q_norm, diff_k_norm, diff_lambda, diff_sub_norm, diff_w_o)
    xa_p = (xa_norm, xa_mem_norm, xa_w_q, xa_w_kv, xa_q_norm, xa_k_norm, xa_w_o)
    ffn_p = (ffn_norm, ffn_w_gu, ffn_conv_w, ffn_conv_b, ffn_w_down)
    y_prompt = _trunk(x_prompt, mem_prompt, mla_p, diff_p, xa_p, ffn_p)
    y_sample = _trunk(x_sample, mem_sample, mla_p, diff_p, xa_p, ffn_p)
    return (y_prompt, y_sample)
```

```python
import functools
import math

import jax
import jax.numpy as jnp
from jax import lax
from jax.experimental import pallas as pl
from jax.experimental.pallas import tpu as pltpu

D_MODEL = 1024
DEPTH = 4
N_MEM = 256
EPS = 1e-6
MLA_HEADS = 8
MLA_Q_LORA = 384
MLA_KV_LORA = 256
MLA_NOPE = 64
MLA_ROPE = 32
MLA_QK = MLA_NOPE + MLA_ROPE
MLA_V = 64
ROPE_THETA = 10000.0
DIFF_HEADS = 8
DIFF_HD = 64
XA_HEADS = 4
XA_HD = D_MODEL // XA_HEADS
D_FF = 2816
CONV_W = 3

LANE = 128
HALO = 8
LOG2E = math.log2(math.e)
NEG_BIG = -1e30
VMEM_LIMIT = 56 * 1024 * 1024

F32 = jnp.float32
BF16 = jnp.bfloat16


def _const_spec(shape):
    nd = len(shape)
    return pl.BlockSpec(shape, lambda *_: (0,) * nd, pipeline_mode=pl.Buffered(1))


def _params(sem):
    return pltpu.CompilerParams(dimension_semantics=sem, vmem_limit_bytes=VMEM_LIMIT)


def _rms(x, g):
    return x * lax.rsqrt(jnp.mean(x * x, axis=-1, keepdims=True) + EPS) * g


def _dot(a, b):
    return jnp.dot(a, b, preferred_element_type=F32)


def _dot_nt(a, b):
    return lax.dot_general(a, b, (((1,), (1,)), ((), ())), preferred_element_type=F32)


def _mem_kv_kernel(mem_ref, g_ref, w_ref, kg_ref, k_ref, v_ref):
    h = _rms(mem_ref[...], g_ref[...]).astype(BF16)
    kv = _dot(h, w_ref[...])
    kg = kg_ref[...]
    for hd in range(XA_HEADS):
        sl = slice(hd * XA_HD, (hd + 1) * XA_HD)
        k_ref[:, sl] = _rms(kv[:, sl], kg).astype(BF16)
    v_ref[...] = kv[:, D_MODEL:].astype(BF16)


def _mem_kv(mem, mem_g, w_kv, k_g):
    bm = mem.shape[0]
    out = jax.ShapeDtypeStruct((DEPTH, bm, N_MEM, D_MODEL), BF16)
    return pl.pallas_call(
        _mem_kv_kernel,
        grid=(DEPTH, bm),
        in_specs=[
            pl.BlockSpec((None, N_MEM, D_MODEL), lambda l, b: (b, 0, 0)),
            pl.BlockSpec((None, 1, D_MODEL), lambda l, b: (l, 0, 0)),
            pl.BlockSpec((None, D_MODEL, 2 * D_MODEL), lambda l, b: (l, 0, 0)),
            pl.BlockSpec((None, 1, XA_HD), lambda l, b: (l, 0, 0)),
        ],
        out_specs=[
            pl.BlockSpec((None, None, N_MEM, D_MODEL), lambda l, b: (l, b, 0, 0)),
            pl.BlockSpec((None, None, N_MEM, D_MODEL), lambda l, b: (l, b, 0, 0)),
        ],
        out_shape=[out, out],
        compiler_params=_params(("arbitrary", "arbitrary")),
        name="mem_kv",
    )(mem, mem_g[:, None, :], w_kv.astype(BF16), k_g[:, None, :])


def _mla_pre_kernel(x_ref, ng_ref, wd_ref, qlg_ref, kvlg_ref, wuq_ref, wukv_ref,
                    gq_ref, gk_ref, cos_ref, sin_ref, q_ref, k_ref, v_ref):
    h = _rms(x_ref[...], ng_ref[...]).astype(BF16)
    down = _dot(h, wd_ref[...])
    c_q = _rms(down[:, :MLA_Q_LORA], qlg_ref[...]).astype(BF16)
    kv0 = MLA_Q_LORA
    c_kv = _rms(down[:, kv0:kv0 + MLA_KV_LORA], kvlg_ref[...]).astype(BF16)
    kr = down[:, kv0 + MLA_KV_LORA:kv0 + MLA_KV_LORA + LANE]
    kr_rot = down[:, kv0 + MLA_KV_LORA + LANE:]
    qq = _dot(c_q, wuq_ref[...])
    kvp = _dot(c_kv, wukv_ref[...])
    cos = cos_ref[...]
    sin = sin_ref[...]
    gq = gq_ref[...]
    gk = gk_ref[...]
    hw = MLA_HEADS * LANE
    one_col = (lax.broadcasted_iota(jnp.int32, (1, LANE), 1) == MLA_V).astype(F32)
    q_scale = (MLA_QK ** -0.5) * LOG2E
    for hd in range(MLA_HEADS):
        sl = slice(hd * LANE, (hd + 1) * LANE)
        qh = qq[:, sl]
        rq = lax.rsqrt(jnp.sum(qh * qh, axis=-1, keepdims=True) * (1.0 / MLA_QK) + EPS)
        q_rot = qq[:, hw + hd * LANE:hw + (hd + 1) * LANE]
        q_ref[:, sl] = ((rq * q_scale) * (qh * gq * cos + q_rot * sin)).astype(BF16)
        kh = kvp[:, sl] + kr
        rk = lax.rsqrt(jnp.sum(kh * kh, axis=-1, keepdims=True) * (1.0 / MLA_QK) + EPS)
        k_ref[:, sl] = (rk * (kh * gk * cos + kr_rot * sin)).astype(BF16)
        v_ref[:, sl] = (kvp[:, hw + hd * LANE:hw + (hd + 1) * LANE] + one_col).astype(BF16)


def _rot_half_cols(w):
    half = MLA_ROPE // 2
    return jnp.concatenate([-w[..., half:], w[..., :half]], axis=-1)


def _mla_weights(norm_g, w_down, q_lat_g, kv_lat_g, w_uq, w_ukv, q_g, k_g, w_o):
    nq, nkv = MLA_Q_LORA, MLA_KV_LORA
    w_dq, w_dkv, w_kr = w_down[:, :nq], w_down[:, nq:nq + nkv], w_down[:, nq + nkv:]
    pad_lo = jnp.zeros((D_MODEL, MLA_NOPE), F32)
    pad_hi = jnp.zeros((D_MODEL, LANE - MLA_QK), F32)
    kr_blk = jnp.concatenate([pad_lo, w_kr, pad_hi], axis=1)
    krot_blk = jnp.concatenate([pad_lo, _rot_half_cols(w_kr * k_g[MLA_NOPE:]), pad_hi], axis=1)
    wd = jnp.concatenate([w_dq, w_dkv, kr_blk, krot_blk], axis=1).astype(BF16)

    wq = w_uq.reshape(nq, MLA_HEADS, MLA_QK)
    zq = jnp.zeros((nq, MLA_HEADS, LANE - MLA_QK), F32)
    wq_main = jnp.concatenate([wq, zq], axis=-1).reshape(nq, MLA_HEADS * LANE)
    wq_rot = jnp.concatenate(
        [jnp.zeros((nq, MLA_HEADS, MLA_NOPE), F32),
         _rot_half_cols(wq[..., MLA_NOPE:] * q_g[MLA_NOPE:]), zq], axis=-1
    ).reshape(nq, MLA_HEADS * LANE)
    wuq = jnp.concatenate([wq_main, wq_rot], axis=1).astype(BF16)

    wkv = w_ukv.reshape(nkv, MLA_HEADS, MLA_NOPE + MLA_V)
    zk = jnp.zeros((nkv, MLA_HEADS, LANE - MLA_NOPE), F32)
    zv = jnp.zeros((nkv, MLA_HEADS, LANE - MLA_V), F32)
    wk = jnp.concatenate([wkv[..., :MLA_NOPE], zk], axis=-1).reshape(nkv, MLA_HEADS * LANE)
    wv = jnp.concatenate([wkv[..., MLA_NOPE:], zv], axis=-1).reshape(nkv, MLA_HEADS * LANE)
    wukv = jnp.concatenate([wk, wv], axis=1).astype(BF16)

    zg = jnp.zeros((LANE - MLA_QK,), F32)
    gq = jnp.concatenate([q_g, zg])[None, :]
    gk = jnp.concatenate([k_g, zg])[None, :]

    wo = w_o.reshape(MLA_HEADS, MLA_V, D_MODEL)
    wo = jnp.concatenate([wo, jnp.zeros((MLA_HEADS, LANE - MLA_V, D_MODEL), F32)], axis=1)
    wo = wo.reshape(MLA_HEADS * LANE, D_MODEL).astype(BF16)
    return (norm_g[None, :], wd, q_lat_g[None, :], kv_lat_g[None, :], wuq, wukv, gq, gk), wo


def _rope_tables(seq):
    inv = ROPE_THETA ** (-jnp.arange(0, MLA_ROPE, 2, dtype=F32) / MLA_ROPE)
    ang = jnp.arange(seq, dtype=F32)[:, None] * inv[None, :]
    ang = jnp.concatenate([ang, ang], axis=-1)
    ones = jnp.ones((seq, MLA_NOPE), F32)
    zlo = jnp.zeros((seq, MLA_NOPE), F32)
    zhi = jnp.zeros((seq, LANE - MLA_QK), F32)
    cos = jnp.concatenate([ones, jnp.cos(ang), zhi], axis=1)
    sin = jnp.concatenate([zlo, jnp.sin(ang), zhi], axis=1)
    return cos, sin


def _mla_pre(x, weights, cos, sin, tm):
    b, s, _ = x.shape
    hw = MLA_HEADS * LANE
    out = jax.ShapeDtypeStruct((b, s, hw), BF16)
    tile = lambda bi, i: (bi, i, 0)
    w_specs = [_const_spec(w.shape) for w in weights]
    return pl.pallas_call(
        _mla_pre_kernel,
        grid=(b, s // tm),
        in_specs=[pl.BlockSpec((None, tm, D_MODEL), tile)] + w_specs + [
            pl.BlockSpec((tm, LANE), lambda bi, i: (i, 0)),
            pl.BlockSpec((tm, LANE), lambda bi, i: (i, 0)),
        ],
        out_specs=[pl.BlockSpec((None, tm, hw), tile)] * 3,
        out_shape=[out, out, out],
        compiler_params=_params(("parallel", "parallel")),
        name="mla_pre",
    )(x, *weights, cos, sin)


def _flash_mla_kernel(q_ref, k_ref, v_ref, o_ref, m_sc, acc_sc, *, tk, n_kv):
    q = q_ref[...]
    m_sc[...] = jnp.full(m_sc.shape, NEG_BIG, F32)
    acc_sc[...] = jnp.zeros(acc_sc.shape, F32)

    def body(j, carry):
        rows = pl.ds(pl.multiple_of(j * tk, tk), tk)
        s = _dot_nt(q, k_ref[rows, :])
        m_old = m_sc[...]
        m_new = jnp.maximum(m_old, jnp.max(s, axis=-1, keepdims=True))
        p = jnp.exp2(s - m_new).astype(BF16)
        acc_sc[...] = jnp.exp2(m_old - m_new) * acc_sc[...] + _dot(p, v_ref[rows, :])
        m_sc[...] = m_new
        return carry

    lax.fori_loop(0, n_kv, body, 0)
    acc = acc_sc[...]
    o_ref[...] = (acc / acc[:, MLA_V:MLA_V + 1]).astype(BF16)


def _flash_mla(q, k, v, tq, tk):
    b, s, hw = q.shape
    kern = functools.partial(_flash_mla_kernel, tk=tk, n_kv=s // tk)
    return pl.pallas_call(
        kern,
        grid=(b, MLA_HEADS, s // tq),
        in_specs=[
            pl.BlockSpec((None, tq, LANE), lambda bi, h, i: (bi, i, h)),
            pl.BlockSpec((None, s, LANE), lambda bi, h, i: (bi, 0, h)),
            pl.BlockSpec((None, s, LANE), lambda bi, h, i: (bi, 0, h)),
        ],
        out_specs=pl.BlockSpec((None, tq, LANE), lambda bi, h, i: (bi, i, h)),
        out_shape=jax.ShapeDtypeStruct((b, s, hw), BF16),
        scratch_shapes=[pltpu.VMEM((tq, 1), F32), pltpu.VMEM((tq, LANE), F32)],
        compiler_params=_params(("parallel", "parallel", "arbitrary")),
        name="flash_mla",
    )(q, k, v)


def _diff_pre_kernel(x_ref, ng_ref, w_ref, gq_ref, gk_ref, q_ref, k_ref, v_ref):
    h = _rms(x_ref[...], ng_ref[...]).astype(BF16)
    qkv = _dot(h, w_ref[...])
    hw = DIFF_HEADS * LANE
    lo = lax.broadcasted_iota(jnp.int32, (1, LANE), 1) < DIFF_HD
    q_scale = (DIFF_HD ** -0.5) * LOG2E

    def half_norm(t, g):
        t2 = t * t
        ss_lo = jnp.sum(jnp.where(lo, t2, 0.0), axis=-1, keepdims=True)
        ss_hi = jnp.sum(jnp.where(lo, 0.0, t2), axis=-1, keepdims=True)
        r = lax.rsqrt(jnp.where(lo, ss_lo, ss_hi) * (1.0 / DIFF_HD) + EPS)
        return t * r * g

    for hd in range(DIFF_HEADS):
        sl = slice(hd * LANE, (hd + 1) * LANE)
        q_ref[:, sl] = (half_norm(qkv[:, sl], gq_ref[...]) * q_scale).astype(BF16)
        k_ref[:, sl] = half_norm(qkv[:, hw + hd * LANE:hw + (hd + 1) * LANE], gk_ref[...]).astype(BF16)
    v_ref[...] = qkv[:, 2 * hw:].astype(BF16)


def _diff_pre(x, norm_g, w_qkv, q_g, k_g, tm):
    b, s, _ = x.shape
    hw = DIFF_HEADS * LANE
    out = jax.ShapeDtypeStruct((b, s, hw), BF16)
    tile = lambda bi, i: (bi, i, 0)
    weights = (norm_g[None, :], w_qkv.astype(BF16), q_g.reshape(1, LANE), k_g.reshape(1, LANE))
    return pl.pallas_call(
        _diff_pre_kernel,
        grid=(b, s // tm),
        in_specs=[pl.BlockSpec((None, tm, D_MODEL), tile)] + [_const_spec(w.shape) for w in weights],
        out_specs=[pl.BlockSpec((None, tm, hw), tile)] * 3,
        out_shape=[out, out, out],
        compiler_params=_params(("parallel", "parallel")),
        name="diff_pre",
    )(x, *weights)


def _flash_diff_kernel(slope_ref, q_ref, k_ref, v_ref, dmat_ref, lam_ref, subg_ref, o_ref,
                       m_sc, l_sc, acc_sc, *, tq, tk, n_kv, lambda_init):
    q = q_ref[...]
    lo = lax.broadcasted_iota(jnp.int32, (1, LANE), 1) < DIFF_HD
    zero = jnp.zeros_like(q)
    lhs = jnp.concatenate([jnp.where(lo, q, zero), jnp.where(lo, zero, q)], axis=0)
    neg_slope = slope_ref[pl.program_id(1)]
    q0 = pl.program_id(2) * tq
    m_sc[...] = jnp.full(m_sc.shape, NEG_BIG, F32)
    l_sc[...] = jnp.zeros(l_sc.shape, F32)
    acc_sc[...] = jnp.zeros(acc_sc.shape, F32)

    def body(j, carry):
        rows = pl.ds(pl.multiple_of(j * tk, tk), tk)
        s = _dot_nt(lhs, k_ref[rows, :])
        delta = (q0 - j * tk).astype(F32)
        bias = jnp.abs(dmat_ref[...] + delta) * neg_slope
        vs = v_ref[rows, :]
        for c in range(2):
            sc = s[c * tq:(c + 1) * tq] + bias
            m_old = m_sc[c]
            m_new = jnp.maximum(m_old, jnp.max(sc, axis=-1, keepdims=True))
            p = jnp.exp2(sc - m_new)
            alpha = jnp.exp2(m_old - m_new)
            l_sc[c] = alpha * l_sc[c] + jnp.sum(p, axis=-1, keepdims=True)
            acc_sc[c] = alpha * acc_sc[c] + _dot(p.astype(BF16), vs)
            m_sc[c] = m_new
        return carry

    lax.fori_loop(0, n_kv, body, 0)
    lp = lam_ref[...]
    lam = (jnp.exp(jnp.sum(lp[0:1] * lp[1:2], axis=-1, keepdims=True))
           - jnp.exp(jnp.sum(lp[2:3] * lp[3:4], axis=-1, keepdims=True)) + lambda_init)
    o = acc_sc[0] / l_sc[0] - lam * (acc_sc[1] / l_sc[1])
    o = _rms(o, subg_ref[...]) * (1.0 - lambda_init)
    o_ref[...] = o.astype(BF16)


def _flash_diff(q, k, v, lam_p, sub_g, lambda_init, tq, tk):
    b, s, hw = q.shape
    slopes = 2.0 ** (-8.0 * jnp.arange(1, DIFF_HEADS + 1, dtype=F32) / DIFF_HEADS)
    neg_slopes = -slopes * LOG2E
    dmat = (jnp.arange(tq, dtype=F32)[:, None] - jnp.arange(tk, dtype=F32)[None, :])
    kern = functools.partial(_flash_diff_kernel, tq=tq, tk=tk, n_kv=s // tk, lambda_init=lambda_init)
    grid_spec = pltpu.PrefetchScalarGridSpec(
        num_scalar_prefetch=1,
        grid=(b, DIFF_HEADS, s // tq),
        in_specs=[
            pl.BlockSpec((None, tq, LANE), lambda bi, h, i, sl: (bi, i, h)),
            pl.BlockSpec((None, s, LANE), lambda bi, h, i, sl: (bi, 0, h)),
            pl.BlockSpec((None, s, LANE), lambda bi, h, i, sl: (bi, 0, h)),
            pl.BlockSpec((tq, tk), lambda bi, h, i, sl: (0, 0), pipeline_mode=pl.Buffered(1)),
            pl.BlockSpec((4, DIFF_HD), lambda bi, h, i, sl: (0, 0), pipeline_mode=pl.Buffered(1)),
            pl.BlockSpec((1, LANE), lambda bi, h, i, sl: (0, 0), pipeline_mode=pl.Buffered(1)),
        ],
        out_specs=pl.BlockSpec((None, tq, LANE), lambda bi, h, i, sl: (bi, i, h)),
        scratch_shapes=[
            pltpu.VMEM((2, tq, 1), F32),
            pltpu.VMEM((2, tq, 1), F32),
            pltpu.VMEM((2, tq, LANE), F32),
        ],
    )
    return pl.pallas_call(
        kern,
        grid_spec=grid_spec,
        out_shape=jax.ShapeDtypeStruct((b, s, hw), BF16),
        compiler_params=_params(("parallel", "parallel", "arbitrary")),
        name="flash_diff",
    )(neg_slopes, q, k, v, dmat, lam_p, sub_g[None, :])


def _post_xattn_kernel(x_ref, om_ref, wom_ref, ng_ref, wq_ref, qg_ref, k_ref, v_ref, wo_ref,
                       y_ref, o_sc):
    x1 = x_ref[...] + _dot(om_ref[...], wom_ref[...])
    h = _rms(x1, ng_ref[...]).astype(BF16)
    q = _dot(h, wq_ref[...])
    qg = qg_ref[...] * ((XA_HD ** -0.5) * LOG2E)
    for hd in range(XA_HEADS):
        sl = slice(hd * XA_HD, (hd + 1) * XA_HD)
        qh = _rms(q[:, sl], qg).astype(BF16)
        s = _dot_nt(qh, k_ref[:, sl])
        p = jnp.exp2(s - jnp.max(s, axis=-1, keepdims=True))
        l = jnp.sum(p, axis=-1, keepdims=True)
        o_sc[:, sl] = (_dot(p.astype(BF16), v_ref[:, sl]) / l).astype(BF16)
    y_ref[...] = x1 + _dot(o_sc[...], wo_ref[...])


def _post_xattn(x, o_mix, w_o_mix, norm_g, w_q, q_g, k_mem, v_mem, layer, w_o, tm):
    b, s, _ = x.shape
    tile = lambda bi, i: (bi, i, 0)
    mem_spec = pl.BlockSpec((None, None, N_MEM, D_MODEL), lambda bi, i: (layer, bi, 0, 0))
    sq = (D_MODEL, D_MODEL)
    return pl.pallas_call(
        _post_xattn_kernel,
        grid=(b, s // tm),
        in_specs=[
            pl.BlockSpec((None, tm, D_MODEL), tile),
            pl.BlockSpec((None, tm, o_mix.shape[-1]), tile),
            _const_spec(w_o_mix.shape),
            _const_spec((1, D_MODEL)),
            _const_spec(sq),
            _const_spec((1, XA_HD)),
            mem_spec,
            mem_spec,
            _const_spec(sq),
        ],
        out_specs=pl.BlockSpec((None, tm, D_MODEL), tile),
        out_shape=jax.ShapeDtypeStruct(x.shape, F32),
        scratch_shapes=[pltpu.VMEM((tm, D_MODEL), BF16)],
        compiler_params=_params(("parallel", "parallel")),
        name="post_xattn",
    )(x, o_mix, w_o_mix, norm_g[None, :], w_q.astype(BF16), q_g[None, :], k_mem, v_mem,
      w_o.astype(BF16))


def _ffn_kernel(x_ref, prev_ref, next_ref, ng_ref, wg_ref, wu_ref, cw_ref, cb_ref, wd_ref, y_ref,
                *, tm, n_chunks):
    i = pl.program_id(1)
    keep_prev = (i > 0).astype(F32)
    keep_next = (i < pl.num_programs(1) - 1).astype(F32)
    x = x_ref[...]
    xe = jnp.concatenate([prev_ref[...] * keep_prev, x, next_ref[...] * keep_next], axis=0)
    he = _rms(xe, ng_ref[...]).astype(BF16)
    hc = he[HALO:HALO + tm]
    cw = cw_ref[...]
    cb = cb_ref[...]
    y = x
    ck = D_FF // n_chunks
    for c in range(n_chunks):
        cs = slice(c * ck, (c + 1) * ck)
        ge = _dot(he, wg_ref[:, cs])
        u = _dot(hc, wu_ref[:, cs])
        g = (ge[HALO - 1:HALO - 1 + tm] * cw[0:1, cs] + ge[HALO:HALO + tm] * cw[1:2, cs]
             + ge[HALO + 1:HALO + 1 + tm] * cw[2:3, cs] + cb[:, cs])
        act = (g * jax.nn.sigmoid(g) * u).astype(BF16)
        y = y + _dot(act, wd_ref[cs, :])
    y_ref[...] = y


def _ffn(x, norm_g, w_gu, conv_w, conv_b, w_down, tm, n_chunks=2):
    b, s, _ = x.shape
    nh = tm // HALO
    last = s // HALO - 1
    w_g = w_gu[:, :D_FF].astype(BF16)
    w_u = w_gu[:, D_FF:].astype(BF16)
    weights = (norm_g[None, :], w_g, w_u, conv_w, conv_b[None, :], w_down.astype(BF16))
    kern = functools.partial(_ffn_kernel, tm=tm, n_chunks=n_chunks)
    return pl.pallas_call(
        kern,
        grid=(b, s // tm),
        in_specs=[
            pl.BlockSpec((None, tm, D_MODEL), lambda bi, i: (bi, i, 0)),
            pl.BlockSpec((None, HALO, D_MODEL), lambda bi, i: (bi, jnp.maximum(i * nh - 1, 0), 0)),
            pl.BlockSpec((None, HALO, D_MODEL), lambda bi, i: (bi, jnp.minimum((i + 1) * nh, last), 0)),
        ] + [_const_spec(w.shape) for w in weights],
        out_specs=pl.BlockSpec((None, tm, D_MODEL), lambda bi, i: (bi, i, 0)),
        out_shape=jax.ShapeDtypeStruct(x.shape, F32),
        compiler_params=_params(("parallel", "parallel")),
        name="ffn",
    )(x, x, x, *weights)


def _tiles(s):
    tm = min(512, s)
    return dict(tm=tm, tq_mla=min(512, s), tq_diff=min(256, s), tk=min(512, s))


def _trunk(x, mem, mla_p, diff_p, xa_p, ffn_p):
    s = x.shape[1]
    t = _tiles(s)
    xa_norm, xa_mem_norm, xa_w_q, xa_w_kv, xa_q_norm, xa_k_norm, xa_w_o = xa_p
    k_mem, v_mem = _mem_kv(mem, xa_mem_norm, xa_w_kv, xa_k_norm)
    cos, sin = _rope_tables(s)
    for i in range(DEPTH):
        j = i // 2
        if i % 2 == 0:
            weights, w_o_mix = _mla_weights(*[p[j] for p in mla_p])
            q, k, v = _mla_pre(x, weights, cos, sin, t["tm"])
            o_mix = _flash_mla(q, k, v, t["tq_mla"], t["tk"])
        else:
            norm_g, w_qkv, q_g, k_g, lam_p, sub_g, w_o = [p[j] for p in diff_p]
            lambda_init = 0.8 - 0.6 * math.exp(-0.3 * i)
            q, k, v = _diff_pre(x, norm_g, w_qkv, q_g, k_g, t["tm"])
            o_mix = _flash_diff(q, k, v, lam_p, sub_g, lambda_init, t["tq_diff"], t["tk"])
            w_o_mix = w_o.astype(BF16)
        x = _post_xattn(x, o_mix, w_o_mix, xa_norm[i], xa_w_q[i], xa_q_norm[i], k_mem, v_mem, i,
                        xa_w_o[i], t["tm"])
        x = _ffn(x, *[p[i] for p in ffn_p], t["tm"])
    return x


def kernel(x_prompt, x_sample, mem_prompt, mem_sample, mla_norm, mla_w_down, mla_q_lat_norm, mla_kv_lat_norm, mla_w_uq, mla_w_ukv, mla_q_norm, mla_k_norm, mla_w_o, diff_norm, diff_w_qkv, diff_q_norm, diff_k_norm, diff_lambda, diff_sub_norm, diff_w_o, xa_norm, xa_mem_norm, xa_w_q, xa_w_kv, xa_q_norm, xa_k_norm, xa_w_o, ffn_norm, ffn_w_gu, ffn_conv_w, ffn_conv_b, ffn_w_down):
    mla_p = (mla_norm, mla_w_down, mla_q_lat_norm, mla_kv_lat_norm, mla_w_uq, mla_w_ukv,
             mla_q_norm, mla_k_norm, mla_w_o)
    diff_p = (diff_norm, diff_w_qkv, diff_q_norm, diff_k_norm, diff_lambda, diff_sub_norm, diff_w_o)
    xa_p = (xa_norm, xa_mem_norm, xa_w_q, xa_w_kv, xa_q_norm, xa_k_norm, xa_w_o)
    ffn_p = (ffn_norm, ffn_w_gu, ffn_conv_w, ffn_conv_b, ffn_w_down)
    y_prompt = _trunk(x_prompt, mem_prompt, mla_p, diff_p, xa_p, ffn_p)
    y_sample = _trunk(x_sample, mem_sample, mla_p, diff_p, xa_p, ffn_p)
    return (y_prompt, y_sample)
```

```python
import functools
import math

import jax
import jax.numpy as jnp
from jax import lax
from jax.experimental import pallas as pl
from jax.experimental.pallas import tpu as pltpu

D_MODEL = 1024
DEPTH = 4
N_MEM = 256
EPS = 1e-6
MLA_HEADS = 8
MLA_Q_LORA = 384
MLA_KV_LORA = 256
MLA_NOPE = 64
MLA_ROPE = 32
MLA_QK = MLA_NOPE + MLA_ROPE
MLA_V = 64
ROPE_THETA = 10000.0
DIFF_HEADS = 8
DIFF_HD = 64
XA_HEADS = 4
XA_HD = D_MODEL // XA_HEADS
D_FF = 2816
CONV_W = 3

LANE = 128
HALO = 8
BF16_ROWS = 16
KV_CHUNK = 512
DIFF_VT_ROWS = LANE + BF16_ROWS
LOG2E = math.log2(math.e)
NEG_BIG = -1e30
VMEM_LIMIT = 56 * 1024 * 1024

F32 = jnp.float32
BF16 = jnp.bfloat16


def _const_spec(shape):
    nd = len(shape)
    return pl.BlockSpec(shape, lambda *_: (0,) * nd, pipeline_mode=pl.Buffered(1))


def _params(sem):
    return pltpu.CompilerParams(dimension_semantics=sem, vmem_limit_bytes=VMEM_LIMIT)


def _rms(x, g):
    return x * lax.rsqrt(jnp.mean(x * x, axis=-1, keepdims=True) + EPS) * g


def _dot(a, b):
    return jnp.dot(a, b, preferred_element_type=F32)


def _dot_nt(a, b):
    return lax.dot_general(a, b, (((1,), (1,)), ((), ())), preferred_element_type=F32)


def _mem_kv_kernel(mem_ref, g_ref, w_ref, kg_ref, k_ref, v_ref):
    h = _rms(mem_ref[...], g_ref[...]).astype(BF16)
    kv = _dot(h, w_ref[...])
    kg = kg_ref[...]
    for hd in range(XA_HEADS):
        sl = slice(hd * XA_HD, (hd + 1) * XA_HD)
        k_ref[:, sl] = _rms(kv[:, sl], kg).astype(BF16)
    v_ref[...] = kv[:, D_MODEL:].astype(BF16)


def _mem_kv(mem, mem_g, w_kv, k_g):
    bm = mem.shape[0]
    out = jax.ShapeDtypeStruct((DEPTH, bm, N_MEM, D_MODEL), BF16)
    return pl.pallas_call(
        _mem_kv_kernel,
        grid=(DEPTH, bm),
        in_specs=[
            pl.BlockSpec((None, N_MEM, D_MODEL), lambda l, b: (b, 0, 0)),
            pl.BlockSpec((None, 1, D_MODEL), lambda l, b: (l, 0, 0)),
            pl.BlockSpec((None, D_MODEL, 2 * D_MODEL), lambda l, b: (l, 0, 0)),
            pl.BlockSpec((None, 1, XA_HD), lambda l, b: (l, 0, 0)),
        ],
        out_specs=[
            pl.BlockSpec((None, None, N_MEM, D_MODEL), lambda l, b: (l, b, 0, 0)),
            pl.BlockSpec((None, None, N_MEM, D_MODEL), lambda l, b: (l, b, 0, 0)),
        ],
        out_shape=[out, out],
        compiler_params=_params(("arbitrary", "arbitrary")),
        name="mem_kv",
    )(mem, mem_g[:, None, :], w_kv.astype(BF16), k_g[:, None, :])


def _mla_pre_kernel(x_ref, ng_ref, wd_ref, qlg_ref, kvlg_ref, wuq_ref, wukv_ref,
                    gq_ref, gk_ref, cos_ref, sin_ref, q_ref, k_ref, vt_ref):
    h = _rms(x_ref[...], ng_ref[...]).astype(BF16)
    down = _dot(h, wd_ref[...])
    c_q = _rms(down[:, :MLA_Q_LORA], qlg_ref[...]).astype(BF16)
    kv0 = MLA_Q_LORA
    c_kv = _rms(down[:, kv0:kv0 + MLA_KV_LORA], kvlg_ref[...]).astype(BF16)
    kr = down[:, kv0 + MLA_KV_LORA:kv0 + MLA_KV_LORA + LANE]
    kr_rot = down[:, kv0 + MLA_KV_LORA + LANE:]
    qq = _dot(c_q, wuq_ref[...])
    kvp = _dot(c_kv, wukv_ref[...])
    cos = cos_ref[...]
    sin = sin_ref[...]
    gq = gq_ref[...]
    gk = gk_ref[...]
    hw = MLA_HEADS * LANE
    one_col = (lax.broadcasted_iota(jnp.int32, (1, LANE), 1) == MLA_V).astype(F32)
    q_scale = (MLA_QK ** -0.5) * LOG2E
    for hd in range(MLA_HEADS):
        sl = slice(hd * LANE, (hd + 1) * LANE)
        qh = qq[:, sl]
        rq = lax.rsqrt(jnp.sum(qh * qh, axis=-1, keepdims=True) * (1.0 / MLA_QK) + EPS)
        q_rot = qq[:, hw + hd * LANE:hw + (hd + 1) * LANE]
        q_ref[:, sl] = ((rq * q_scale) * (qh * gq * cos + q_rot * sin)).astype(BF16)
        kh = kvp[:, sl] + kr
        rk = lax.rsqrt(jnp.sum(kh * kh, axis=-1, keepdims=True) * (1.0 / MLA_QK) + EPS)
        k_ref[:, sl] = (rk * (kh * gk * cos + kr_rot * sin)).astype(BF16)
        vt_ref[hd] = (kvp[:, hw + hd * LANE:hw + (hd + 1) * LANE] + one_col).T.astype(BF16)


def _rot_half_cols(w):
    half = MLA_ROPE // 2
    return jnp.concatenate([-w[..., half:], w[..., :half]], axis=-1)


def _mla_weights(norm_g, w_down, q_lat_g, kv_lat_g, w_uq, w_ukv, q_g, k_g, w_o):
    nq, nkv = MLA_Q_LORA, MLA_KV_LORA
    w_dq, w_dkv, w_kr = w_down[:, :nq], w_down[:, nq:nq + nkv], w_down[:, nq + nkv:]
    pad_lo = jnp.zeros((D_MODEL, MLA_NOPE), F32)
    pad_hi = jnp.zeros((D_MODEL, LANE - MLA_QK), F32)
    kr_blk = jnp.concatenate([pad_lo, w_kr, pad_hi], axis=1)
    krot_blk = jnp.concatenate([pad_lo, _rot_half_cols(w_kr * k_g[MLA_NOPE:]), pad_hi], axis=1)
    wd = jnp.concatenate([w_dq, w_dkv, kr_blk, krot_blk], axis=1).astype(BF16)

    wq = w_uq.reshape(nq, MLA_HEADS, MLA_QK)
    zq = jnp.zeros((nq, MLA_HEADS, LANE - MLA_QK), F32)
    wq_main = jnp.concatenate([wq, zq], axis=-1).reshape(nq, MLA_HEADS * LANE)
    wq_rot = jnp.concatenate(
        [jnp.zeros((nq, MLA_HEADS, MLA_NOPE), F32),
         _rot_half_cols(wq[..., MLA_NOPE:] * q_g[MLA_NOPE:]), zq], axis=-1
    ).reshape(nq, MLA_HEADS * LANE)
    wuq = jnp.concatenate([wq_main, wq_rot], axis=1).astype(BF16)

    wkv = w_ukv.reshape(nkv, MLA_HEADS, MLA_NOPE + MLA_V)
    zk = jnp.zeros((nkv, MLA_HEADS, LANE - MLA_NOPE), F32)
    zv = jnp.zeros((nkv, MLA_HEADS, LANE - MLA_V), F32)
    wk = jnp.concatenate([wkv[..., :MLA_NOPE], zk], axis=-1).reshape(nkv, MLA_HEADS * LANE)
    wv = jnp.concatenate([wkv[..., MLA_NOPE:], zv], axis=-1).reshape(nkv, MLA_HEADS * LANE)
    wukv = jnp.concatenate([wk, wv], axis=1).astype(BF16)

    zg = jnp.zeros((LANE - MLA_QK,), F32)
    gq = jnp.concatenate([q_g, zg])[None, :]
    gk = jnp.concatenate([k_g, zg])[None, :]

    wo = w_o.reshape(MLA_HEADS, MLA_V, D_MODEL)
    wo = jnp.concatenate([wo, jnp.zeros((MLA_HEADS, LANE - MLA_V, D_MODEL), F32)], axis=1)
    wo = wo.reshape(MLA_HEADS * LANE, D_MODEL).astype(BF16)
    return (norm_g[None, :], wd, q_lat_g[None, :], kv_lat_g[None, :], wuq, wukv, gq, gk), wo


def _rope_tables(seq):
    inv = ROPE_THETA ** (-jnp.arange(0, MLA_ROPE, 2, dtype=F32) / MLA_ROPE)
    ang = jnp.arange(seq, dtype=F32)[:, None] * inv[None, :]
    ang = jnp.concatenate([ang, ang], axis=-1)
    ones = jnp.ones((seq, MLA_NOPE), F32)
    zlo = jnp.zeros((seq, MLA_NOPE), F32)
    zhi = jnp.zeros((seq, LANE - MLA_QK), F32)
    cos = jnp.concatenate([ones, jnp.cos(ang), zhi], axis=1)
    sin = jnp.concatenate([zlo, jnp.sin(ang), zhi], axis=1)
    return cos, sin


def _vt_shape_and_spec(b, heads, rows, s):
    shape = jax.ShapeDtypeStruct((b, heads, s // KV_CHUNK, rows, KV_CHUNK), BF16)
    spec = pl.BlockSpec((None, heads, None, rows, KV_CHUNK), lambda bi, i: (bi, 0, i, 0, 0))
    return shape, spec


def _mla_pre(x, weights, cos, sin):
    b, s, _ = x.shape
    tm = KV_CHUNK
    hw = MLA_HEADS * LANE
    out = jax.ShapeDtypeStruct((b, s, hw), BF16)
    vt_shape, vt_spec = _vt_shape_and_spec(b, MLA_HEADS, LANE, s)
    tile = lambda bi, i: (bi, i, 0)
    w_specs = [_const_spec(w.shape) for w in weights]
    return pl.pallas_call(
        _mla_pre_kernel,
        grid=(b, s // tm),
        in_specs=[pl.BlockSpec((None, tm, D_MODEL), tile)] + w_specs + [
            pl.BlockSpec((tm, LANE), lambda bi, i: (i, 0)),
            pl.BlockSpec((tm, LANE), lambda bi, i: (i, 0)),
        ],
        out_specs=[pl.BlockSpec((None, tm, hw), tile)] * 2 + [vt_spec],
        out_shape=[out, out, vt_shape],
        compiler_params=_params(("parallel", "parallel")),
        name="mla_pre",
    )(x, *weights, cos, sin)


def _flash_core(scores, vt_ref, s_buf, mx_buf, p_buf, al_buf, m_sc, acc_sc, n_kv):
    assert n_kv >= 2 and n_kv % 2 == 0

    def issue_scores(j, slot):
        s = scores(j)
        s_buf[slot] = s
        mx_buf[slot] = jnp.max(s, axis=0, keepdims=True)

    def softmax(slot):
        m_old = m_sc[...]
        m_new = jnp.maximum(m_old, mx_buf[slot])
        p_buf[slot] = jnp.exp2(s_buf[slot] - m_new).astype(BF16)
        al_buf[slot] = jnp.exp2(m_old - m_new)
        m_sc[...] = m_new

    def values(j, slot):
        acc_sc[...] = al_buf[slot] * acc_sc[...] + _dot(vt_ref[j], p_buf[slot])

    m_sc[...] = jnp.full(m_sc.shape, NEG_BIG, F32)
    acc_sc[...] = jnp.zeros(acc_sc.shape, F32)
    issue_scores(0, 0)
    issue_scores(1, 1)
    softmax(0)

    def pair(i, carry):
        j = 2 * i + 1
        issue_scores(j + 1, 0)
        softmax(1)
        values(j - 1, 0)
        issue_scores(j + 2, 1)
        softmax(0)
        values(j, 1)
        return carry

    lax.fori_loop(0, (n_kv - 2) // 2, pair, 0)
    softmax(1)
    values(n_kv - 2, 0)
    values(n_kv - 1, 1)


def _flash_scratch(rows, nq):
    return [
        pltpu.VMEM((2, KV_CHUNK, nq), F32),
        pltpu.VMEM((2, 1, nq), F32),
        pltpu.VMEM((2, KV_CHUNK, nq), BF16),
        pltpu.VMEM((2, 1, nq), F32),
        pltpu.VMEM((1, nq), F32),
        pltpu.VMEM((rows, nq), F32),
    ]


def _key_chunk(k_ref, j):
    return k_ref[pl.ds(pl.multiple_of(j * KV_CHUNK, KV_CHUNK), KV_CHUNK), :]


def _flash_mla_kernel(q_ref, k_ref, vt_ref, o_ref, *scratch, n_kv):
    acc_sc = scratch[-1]
    _flash_core(lambda j: _dot_nt(_key_chunk(k_ref, j), q_ref[...]), vt_ref, *scratch, n_kv)
    acc = acc_sc[...]
    o_ref[...] = (acc / acc[MLA_V:MLA_V + 1, :]).T.astype(BF16)


def _flash_mla(q, k, vt, tq):
    b, s, hw = q.shape
    n_kv = s // KV_CHUNK
    kern = functools.partial(_flash_mla_kernel, n_kv=n_kv)
    return pl.pallas_call(
        kern,
        grid=(b, MLA_HEADS, s // tq),
        in_specs=[
            pl.BlockSpec((None, tq, LANE), lambda bi, h, i: (bi, i, h)),
            pl.BlockSpec((None, s, LANE), lambda bi, h, i: (bi, 0, h)),
            pl.BlockSpec((None, None, n_kv, LANE, KV_CHUNK), lambda bi, h, i: (bi, h, 0, 0, 0)),
        ],
        out_specs=pl.BlockSpec((None, tq, LANE), lambda bi, h, i: (bi, i, h)),
        out_shape=jax.ShapeDtypeStruct((b, s, hw), BF16),
        scratch_shapes=_flash_scratch(LANE, tq),
        compiler_params=_params(("parallel", "parallel", "arbitrary")),
        name="flash_mla",
    )(q, k, vt)


def _diff_pre_kernel(x_ref, ng_ref, w_ref, gq_ref, gk_ref, q_ref, k_ref, vt_ref):
    h = _rms(x_ref[...], ng_ref[...]).astype(BF16)
    qkv = _dot(h, w_ref[...])
    hw = DIFF_HEADS * LANE
    lo = lax.broadcasted_iota(jnp.int32, (1, LANE), 1) < DIFF_HD
    q_scale = (DIFF_HD ** -0.5) * LOG2E
    tm = qkv.shape[0]
    ones_rows = (lax.broadcasted_iota(jnp.int32, (DIFF_VT_ROWS - LANE, tm), 0) == 0).astype(F32)

    def half_norm(t, g):
        t2 = t * t
        ss_lo = jnp.sum(jnp.where(lo, t2, 0.0), axis=-1, keepdims=True)
        ss_hi = jnp.sum(jnp.where(lo, 0.0, t2), axis=-1, keepdims=True)
        r = lax.rsqrt(jnp.where(lo, ss_lo, ss_hi) * (1.0 / DIFF_HD) + EPS)
        return t * r * g

    for hd in range(DIFF_HEADS):
        sl = slice(hd * LANE, (hd + 1) * LANE)
        q_ref[:, sl] = (half_norm(qkv[:, sl], gq_ref[...]) * q_scale).astype(BF16)
        k_ref[:, sl] = half_norm(qkv[:, hw + hd * LANE:hw + (hd + 1) * LANE], gk_ref[...]).astype(BF16)
        v_t = qkv[:, 2 * hw + hd * LANE:2 * hw + (hd + 1) * LANE].T
        vt_ref[hd] = jnp.concatenate([v_t, ones_rows], axis=0).astype(BF16)


def _diff_pre(x, norm_g, w_qkv, q_g, k_g):
    b, s, _ = x.shape
    tm = KV_CHUNK
    hw = DIFF_HEADS * LANE
    out = jax.ShapeDtypeStruct((b, s, hw), BF16)
    vt_shape, vt_spec = _vt_shape_and_spec(b, DIFF_HEADS, DIFF_VT_ROWS, s)
    tile = lambda bi, i: (bi, i, 0)
    weights = (norm_g[None, :], w_qkv.astype(BF16), q_g.reshape(1, LANE), k_g.reshape(1, LANE))
    return pl.pallas_call(
        _diff_pre_kernel,
        grid=(b, s // tm),
        in_specs=[pl.BlockSpec((None, tm, D_MODEL), tile)] + [_const_spec(w.shape) for w in weights],
        out_specs=[pl.BlockSpec((None, tm, hw), tile)] * 2 + [vt_spec],
        out_shape=[out, out, vt_shape],
        compiler_params=_params(("parallel", "parallel")),
        name="diff_pre",
    )(x, *weights)


def _flash_diff_kernel(slope_ref, q_ref, k_ref, vt_ref, dmat_ref, lam_ref, subg_ref, o_ref,
                       qcat_sc, *scratch, tq, n_kv, lambda_init):
    acc_sc = scratch[-1]
    lo = lax.broadcasted_iota(jnp.int32, (1, LANE), 1) < DIFF_HD
    q = q_ref[...]
    zero = jnp.zeros_like(q)
    qcat_sc[:tq] = jnp.where(lo, q, zero)
    qcat_sc[tq:] = jnp.where(lo, zero, q)
    neg_slope = slope_ref[pl.program_id(1)]
    q0 = pl.program_id(2) * tq

    def scores(j):
        delta = (j * KV_CHUNK - q0).astype(F32)
        bias = jnp.abs(dmat_ref[...] + delta) * neg_slope
        return _dot_nt(_key_chunk(k_ref, j), qcat_sc[...]) + jnp.concatenate([bias, bias], axis=1)

    _flash_core(scores, vt_ref, *scratch, n_kv)
    lp = lam_ref[...]
    lam = (jnp.exp(jnp.sum(lp[0:1] * lp[1:2], axis=-1, keepdims=True))
           - jnp.exp(jnp.sum(lp[2:3] * lp[3:4], axis=-1, keepdims=True)) + lambda_init)
    acc = acc_sc[...]
    o = acc[:LANE] / acc[LANE:LANE + 1]
    o = (o[:, :tq] - lam * o[:, tq:]).T
    o = _rms(o, subg_ref[...]) * (1.0 - lambda_init)
    o_ref[...] = o.astype(BF16)


def _flash_diff(q, k, vt, lam_p, sub_g, lambda_init, tq):
    b, s, hw = q.shape
    n_kv = s // KV_CHUNK
    slopes = 2.0 ** (-8.0 * jnp.arange(1, DIFF_HEADS + 1, dtype=F32) / DIFF_HEADS)
    neg_slopes = -slopes * LOG2E
    dmat = (jnp.arange(KV_CHUNK, dtype=F32)[:, None] - jnp.arange(tq, dtype=F32)[None, :])
    kern = functools.partial(_flash_diff_kernel, tq=tq, n_kv=n_kv, lambda_init=lambda_init)
    grid_spec = pltpu.PrefetchScalarGridSpec(
        num_scalar_prefetch=1,
        grid=(b, DIFF_HEADS, s // tq),
        in_specs=[
            pl.BlockSpec((None, tq, LANE), lambda bi, h, i, sl: (bi, i, h)),
            pl.BlockSpec((None, s, LANE), lambda bi, h, i, sl: (bi, 0, h)),
            pl.BlockSpec((None, None, n_kv, DIFF_VT_ROWS, KV_CHUNK),
                         lambda bi, h, i, sl: (bi, h, 0, 0, 0)),
            pl.BlockSpec((KV_CHUNK, tq), lambda bi, h, i, sl: (0, 0), pipeline_mode=pl.Buffered(1)),
            pl.BlockSpec((4, DIFF_HD), lambda bi, h, i, sl: (0, 0), pipeline_mode=pl.Buffered(1)),
            pl.BlockSpec((1, LANE), lambda bi, h, i, sl: (0, 0), pipeline_mode=pl.Buffered(1)),
        ],
        out_specs=pl.BlockSpec((None, tq, LANE), lambda bi, h, i, sl: (bi, i, h)),
        scratch_shapes=[pltpu.VMEM((2 * tq, LANE), BF16)] + _flash_scratch(DIFF_VT_ROWS, 2 * tq),
    )
    return pl.pallas_call(
        kern,
        grid_spec=grid_spec,
        out_shape=jax.ShapeDtypeStruct((b, s, hw), BF16),
        compiler_params=_params(("parallel", "parallel", "arbitrary")),
        name="flash_diff",
    )(neg_slopes, q, k, vt, dmat, lam_p, sub_g[None, :])


def _post_xattn_kernel(x_ref, om_ref, wom_ref, ng_ref, wq_ref, qg_ref, k_ref, v_ref, wo_ref,
                       y_ref, o_sc):
    x1 = x_ref[...] + _dot(om_ref[...], wom_ref[...])
    h = _rms(x1, ng_ref[...]).astype(BF16)
    q = _dot(h, wq_ref[...])
    qg = qg_ref[...] * ((XA_HD ** -0.5) * LOG2E)
    for hd in range(XA_HEADS):
        sl = slice(hd * XA_HD, (hd + 1) * XA_HD)
        qh = _rms(q[:, sl], qg).astype(BF16)
        s = _dot_nt(qh, k_ref[:, sl])
        p = jnp.exp2(s - jnp.max(s, axis=-1, keepdims=True))
        l = jnp.sum(p, axis=-1, keepdims=True)
        o_sc[:, sl] = (_dot(p.astype(BF16), v_ref[:, sl]) / l).astype(BF16)
    y_ref[...] = x1 + _dot(o_sc[...], wo_ref[...])


def _post_xattn(x, o_mix, w_o_mix, norm_g, w_q, q_g, k_mem, v_mem, layer, w_o, tm):
    b, s, _ = x.shape
    tile = lambda bi, i: (bi, i, 0)
    mem_spec = pl.BlockSpec((None, None, N_MEM, D_MODEL), lambda bi, i: (layer, bi, 0, 0))
    sq = (D_MODEL, D_MODEL)
    return pl.pallas_call(
        _post_xattn_kernel,
        grid=(b, s // tm),
        in_specs=[
            pl.BlockSpec((None, tm, D_MODEL), tile),
            pl.BlockSpec((None, tm, o_mix.shape[-1]), tile),
            _const_spec(w_o_mix.shape),
            _const_spec((1, D_MODEL)),
            _const_spec(sq),
            _const_spec((1, XA_HD)),
            mem_spec,
            mem_spec,
            _const_spec(sq),
        ],
        out_specs=pl.BlockSpec((None, tm, D_MODEL), tile),
        out_shape=jax.ShapeDtypeStruct(x.shape, F32),
        scratch_shapes=[pltpu.VMEM((tm, D_MODEL), BF16)],
        compiler_params=_params(("parallel", "parallel")),
        name="post_xattn",
    )(x, o_mix, w_o_mix, norm_g[None, :], w_q.astype(BF16), q_g[None, :], k_mem, v_mem,
      w_o.astype(BF16))


def _ffn_kernel(x_ref, prev_ref, next_ref, ng_ref, wg_ref, wu_ref, cw_ref, cb_ref, wd_ref, y_ref,
                *, tm, n_chunks):
    i = pl.program_id(1)
    keep_prev = (i > 0).astype(F32)
    keep_next = (i < pl.num_programs(1) - 1).astype(F32)
    x = x_ref[...]
    xe = jnp.concatenate([prev_ref[...] * keep_prev, x, next_ref[...] * keep_next], axis=0)
    he = _rms(xe, ng_ref[...]).astype(BF16)
    hc = he[HALO:HALO + tm]
    cw = cw_ref[...]
    cb = cb_ref[...]
    y = x
    ck = D_FF // n_chunks
    for c in range(n_chunks):
        cs = slice(c * ck, (c + 1) * ck)
        ge = _dot(he, wg_ref[:, cs])
        u = _dot(hc, wu_ref[:, cs])
        g = (ge[HALO - 1:HALO - 1 + tm] * cw[0:1, cs] + ge[HALO:HALO + tm] * cw[1:2, cs]
             + ge[HALO + 1:HALO + 1 + tm] * cw[2:3, cs] + cb[:, cs])
        act = (g * jax.nn.sigmoid(g) * u).astype(BF16)
        y = y + _dot(act, wd_ref[cs, :])
    y_ref[...] = y


def _ffn(x, norm_g, w_gu, conv_w, conv_b, w_down, tm, n_chunks=2):
    b, s, _ = x.shape
    nh = tm // HALO
    last = s // HALO - 1
    w_g = w_gu[:, :D_FF].astype(BF16)
    w_u = w_gu[:, D_FF:].astype(BF16)
    weights = (norm_g[None, :], w_g, w_u, conv_w, conv_b[None, :], w_down.astype(BF16))
    kern = functools.partial(_ffn_kernel, tm=tm, n_chunks=n_chunks)
    return pl.pallas_call(
        kern,
        grid=(b, s // tm),
        in_specs=[
            pl.BlockSpec((None, tm, D_MODEL), lambda bi, i: (bi, i, 0)),
            pl.BlockSpec((None, HALO, D_MODEL), lambda bi, i: (bi, jnp.maximum(i * nh - 1, 0), 0)),
            pl.BlockSpec((None, HALO, D_MODEL), lambda bi, i: (bi, jnp.minimum((i + 1) * nh, last), 0)),
        ] + [_const_spec(w.shape) for w in weights],
        out_specs=pl.BlockSpec((None, tm, D_MODEL), lambda bi, i: (bi, i, 0)),
        out_shape=jax.ShapeDtypeStruct(x.shape, F32),
        compiler_params=_params(("parallel", "parallel")),
        name="ffn",
    )(x, x, x, *weights)


def _tiles(s):
    return dict(tm=min(512, s), tq_mla=512, tq_diff=256)


def _trunk(x, mem, mla_p, diff_p, xa_p, ffn_p):
    s = x.shape[1]
    assert s % KV_CHUNK == 0
    t = _tiles(s)
    xa_norm, xa_mem_norm, xa_w_q, xa_w_kv, xa_q_norm, xa_k_norm, xa_w_o = xa_p
    k_mem, v_mem = _mem_kv(mem, xa_mem_norm, xa_w_kv, xa_k_norm)
    cos, sin = _rope_tables(s)
    for i in range(DEPTH):
        j = i // 2
        if i % 2 == 0:
            weights, w_o_mix = _mla_weights(*[p[j] for p in mla_p])
            q, k, vt = _mla_pre(x, weights, cos, sin)
            o_mix = _flash_mla(q, k, vt, t["tq_mla"])
        else:
            norm_g, w_qkv, q_g, k_g, lam_p, sub_g, w_o = [p[j] for p in diff_p]
            lambda_init = 0.8 - 0.6 * math.exp(-0.3 * i)
            q, k, vt = _diff_pre(x, norm_g, w_qkv, q_g, k_g)
            o_mix = _flash_diff(q, k, vt, lam_p, sub_g, lambda_init, t["tq_diff"])
            w_o_mix = w_o.astype(BF16)
        x = _post_xattn(x, o_mix, w_o_mix, xa_norm[i], xa_w_q[i], xa_q_norm[i], k_mem, v_mem, i,
                        xa_w_o[i], t["tm"])
        x = _ffn(x, *[p[i] for p in ffn_p], t["tm"])
    return x


def kernel(x_prompt, x_sample, mem_prompt, mem_sample, mla_norm, mla_w_down, mla_q_lat_norm, mla_kv_lat_norm, mla_w_uq, mla_w_ukv, mla_q_norm, mla_k_norm, mla_w_o, diff_norm, diff_w_qkv, diff_q_norm, diff_k_norm, diff_lambda, diff_sub_norm, diff_w_o, xa_norm, xa_mem_norm, xa_w_q, xa_w_kv, xa_q_norm, xa_k_norm, xa_w_o, ffn_norm, ffn_w_gu, ffn_conv_w, ffn_conv_b, ffn_w_down):
    mla_p = (mla_norm, mla_w_down, mla_q_lat_norm, mla_kv_lat_norm, mla_w_uq, mla_w_ukv,
             mla_q_norm, mla_k_norm, mla_w_o)
    diff_p = (diff_norm, diff_w_qkv, diff_q_norm, diff_k_norm, diff_lambda, diff_sub_norm, diff_w_o)
    xa_p = (xa_norm, xa_mem_norm, xa_w_q, xa_w_kv, xa_q_norm, xa_k_norm, xa_w_o)
    ffn_p = (ffn_norm, ffn_w_gu, ffn_conv_w, ffn_conv_b, ffn_w_down)
    y_prompt = _trunk(x_prompt, mem_prompt, mla_p, diff_p, xa_p, ffn_p)
    y_sample = _trunk(x_sample, mem_sample, mla_p, diff_p, xa_p, ffn_p)
    return (y_prompt, y_sample)
```

```python
import functools
import math

import jax
import jax.numpy as jnp
from jax import lax
from jax.experimental import pallas as pl
from jax.experimental.pallas import tpu as pltpu

D_MODEL = 1024
DEPTH = 4
N_MEM = 256
EPS = 1e-6
MLA_HEADS = 8
MLA_Q_LORA = 384
MLA_KV_LORA = 256
MLA_NOPE = 64
MLA_ROPE = 32
MLA_QK = MLA_NOPE + MLA_ROPE
MLA_V = 64
ROPE_THETA = 10000.0
DIFF_HEADS = 8
DIFF_HD = 64
XA_HEADS = 4
XA_HD = D_MODEL // XA_HEADS
D_FF = 2816
CONV_W = 3

LANE = 128
HALO = 8
BF16_ROWS = 16
KV_CHUNK = 512
DIFF_VT_ROWS = LANE + BF16_ROWS
MLA_VT_ROWS = MLA_V + BF16_ROWS
SHIFT_LANE = MLA_QK
SCORE_MARGIN = 1.02
MAX_STATIC_BOUND = 60.0
DIFF_AUG_PARTS = 3
POS_SPLIT = 64
LOG2E = math.log2(math.e)
NEG_BIG = -1e30
VMEM_LIMIT = 56 * 1024 * 1024

F32 = jnp.float32
BF16 = jnp.bfloat16


def _const_spec(shape):
    nd = len(shape)
    return pl.BlockSpec(shape, lambda *_: (0,) * nd, pipeline_mode=pl.Buffered(1))


def _params(sem):
    return pltpu.CompilerParams(dimension_semantics=sem, vmem_limit_bytes=VMEM_LIMIT)


def _rms(x, g):
    return x * lax.rsqrt(jnp.mean(x * x, axis=-1, keepdims=True) + EPS) * g


def _dot(a, b):
    return jnp.dot(a, b, preferred_element_type=F32)


def _dot_nt(a, b):
    return lax.dot_general(a, b, (((1,), (1,)), ((), ())), preferred_element_type=F32)


def _mem_kv_kernel(mem_ref, g_ref, w_ref, kg_ref, k_ref, v_ref):
    h = _rms(mem_ref[...], g_ref[...]).astype(BF16)
    kv = _dot(h, w_ref[...])
    kg = kg_ref[...]
    for hd in range(XA_HEADS):
        sl = slice(hd * XA_HD, (hd + 1) * XA_HD)
        k_ref[:, sl] = _rms(kv[:, sl], kg).astype(BF16)
    v_ref[...] = kv[:, D_MODEL:].astype(BF16)


def _mem_kv(mem, mem_g, w_kv, k_g):
    bm = mem.shape[0]
    out = jax.ShapeDtypeStruct((DEPTH, bm, N_MEM, D_MODEL), BF16)
    return pl.pallas_call(
        _mem_kv_kernel,
        grid=(DEPTH, bm),
        in_specs=[
            pl.BlockSpec((None, N_MEM, D_MODEL), lambda l, b: (b, 0, 0)),
            pl.BlockSpec((None, 1, D_MODEL), lambda l, b: (l, 0, 0)),
            pl.BlockSpec((None, D_MODEL, 2 * D_MODEL), lambda l, b: (l, 0, 0)),
            pl.BlockSpec((None, 1, XA_HD), lambda l, b: (l, 0, 0)),
        ],
        out_specs=[
            pl.BlockSpec((None, None, N_MEM, D_MODEL), lambda l, b: (l, b, 0, 0)),
            pl.BlockSpec((None, None, N_MEM, D_MODEL), lambda l, b: (l, b, 0, 0)),
        ],
        out_shape=[out, out],
        compiler_params=_params(("arbitrary", "arbitrary")),
        name="mem_kv",
    )(mem, mem_g[:, None, :], w_kv.astype(BF16), k_g[:, None, :])


def _mla_pre_kernel(x_ref, ng_ref, wd_ref, qlg_ref, kvlg_ref, wuq_ref, wukv_ref,
                    gq_ref, gk_ref, qshift_ref, cos_ref, sin_ref, q_ref, k_ref, vt_ref):
    h = _rms(x_ref[...], ng_ref[...]).astype(BF16)
    down = _dot(h, wd_ref[...])
    c_q = _rms(down[:, :MLA_Q_LORA], qlg_ref[...]).astype(BF16)
    kv0 = MLA_Q_LORA
    c_kv = _rms(down[:, kv0:kv0 + MLA_KV_LORA], kvlg_ref[...]).astype(BF16)
    kr = down[:, kv0 + MLA_KV_LORA:kv0 + MLA_KV_LORA + LANE]
    kr_rot = down[:, kv0 + MLA_KV_LORA + LANE:]
    qq = _dot(c_q, wuq_ref[...])
    kvp = _dot(c_kv, wukv_ref[...])
    cos = cos_ref[...]
    sin = sin_ref[...]
    gq = gq_ref[...]
    gk = gk_ref[...]
    hw = MLA_HEADS * LANE
    lane = lax.broadcasted_iota(jnp.int32, (1, LANE), 1)
    one_col = (lane == MLA_V).astype(F32)
    k_one = (lane == SHIFT_LANE).astype(F32)
    q_shift = qshift_ref[...]
    q_scale = (MLA_QK ** -0.5) * LOG2E
    for hd in range(MLA_HEADS):
        sl = slice(hd * LANE, (hd + 1) * LANE)
        qh = qq[:, sl]
        rq = lax.rsqrt(jnp.sum(qh * qh, axis=-1, keepdims=True) * (1.0 / MLA_QK) + EPS)
        q_rot = qq[:, hw + hd * LANE:hw + (hd + 1) * LANE]
        q_ref[:, sl] = ((rq * q_scale) * (qh * gq * cos + q_rot * sin) + q_shift).astype(BF16)
        kh = kvp[:, sl] + kr
        rk = lax.rsqrt(jnp.sum(kh * kh, axis=-1, keepdims=True) * (1.0 / MLA_QK) + EPS)
        k_ref[:, sl] = (rk * (kh * gk * cos + kr_rot * sin) + k_one).astype(BF16)
        v_t = (kvp[:, hw + hd * LANE:hw + (hd + 1) * LANE] + one_col).T
        vt_ref[hd] = v_t[:MLA_VT_ROWS].astype(BF16)


def _rot_half_cols(w):
    half = MLA_ROPE // 2
    return jnp.concatenate([-w[..., half:], w[..., :half]], axis=-1)


def _mla_weights(norm_g, w_down, q_lat_g, kv_lat_g, w_uq, w_ukv, q_g, k_g, w_o):
    nq, nkv = MLA_Q_LORA, MLA_KV_LORA
    w_dq, w_dkv, w_kr = w_down[:, :nq], w_down[:, nq:nq + nkv], w_down[:, nq + nkv:]
    pad_lo = jnp.zeros((D_MODEL, MLA_NOPE), F32)
    pad_hi = jnp.zeros((D_MODEL, LANE - MLA_QK), F32)
    kr_blk = jnp.concatenate([pad_lo, w_kr, pad_hi], axis=1)
    krot_blk = jnp.concatenate([pad_lo, _rot_half_cols(w_kr * k_g[MLA_NOPE:]), pad_hi], axis=1)
    wd = jnp.concatenate([w_dq, w_dkv, kr_blk, krot_blk], axis=1).astype(BF16)

    wq = w_uq.reshape(nq, MLA_HEADS, MLA_QK)
    zq = jnp.zeros((nq, MLA_HEADS, LANE - MLA_QK), F32)
    wq_main = jnp.concatenate([wq, zq], axis=-1).reshape(nq, MLA_HEADS * LANE)
    wq_rot = jnp.concatenate(
        [jnp.zeros((nq, MLA_HEADS, MLA_NOPE), F32),
         _rot_half_cols(wq[..., MLA_NOPE:] * q_g[MLA_NOPE:]), zq], axis=-1
    ).reshape(nq, MLA_HEADS * LANE)
    wuq = jnp.concatenate([wq_main, wq_rot], axis=1).astype(BF16)

    wkv = w_ukv.reshape(nkv, MLA_HEADS, MLA_NOPE + MLA_V)
    zk = jnp.zeros((nkv, MLA_HEADS, LANE - MLA_NOPE), F32)
    zv = jnp.zeros((nkv, MLA_HEADS, LANE - MLA_V), F32)
    wk = jnp.concatenate([wkv[..., :MLA_NOPE], zk], axis=-1).reshape(nkv, MLA_HEADS * LANE)
    wv = jnp.concatenate([wkv[..., MLA_NOPE:], zv], axis=-1).reshape(nkv, MLA_HEADS * LANE)
    wukv = jnp.concatenate([wk, wv], axis=1).astype(BF16)

    zg = jnp.zeros((LANE - MLA_QK,), F32)
    gq = jnp.concatenate([q_g, zg])[None, :]
    gk = jnp.concatenate([k_g, zg])[None, :]

    wo = w_o.reshape(MLA_HEADS, MLA_V, D_MODEL)
    wo = jnp.concatenate([wo, jnp.zeros((MLA_HEADS, LANE - MLA_V, D_MODEL), F32)], axis=1)
    wo = wo.reshape(MLA_HEADS * LANE, D_MODEL).astype(BF16)
    bound = _score_bound(MLA_QK, q_g, k_g)
    q_shift = -bound * (jnp.arange(LANE) == SHIFT_LANE).astype(F32)[None, :]
    weights = (norm_g[None, :], wd, q_lat_g[None, :], kv_lat_g[None, :], wuq, wukv, gq, gk, q_shift)
    return weights, wo, bound


def _score_bound(dim, q_g, k_g):
    return SCORE_MARGIN * math.sqrt(dim) * LOG2E * jnp.max(jnp.abs(q_g)) * jnp.max(jnp.abs(k_g))


def _rope_tables(seq):
    inv = ROPE_THETA ** (-jnp.arange(0, MLA_ROPE, 2, dtype=F32) / MLA_ROPE)
    ang = jnp.arange(seq, dtype=F32)[:, None] * inv[None, :]
    ang = jnp.concatenate([ang, ang], axis=-1)
    ones = jnp.ones((seq, MLA_NOPE), F32)
    zlo = jnp.zeros((seq, MLA_NOPE), F32)
    zhi = jnp.zeros((seq, LANE - MLA_QK), F32)
    cos = jnp.concatenate([ones, jnp.cos(ang), zhi], axis=1)
    sin = jnp.concatenate([zlo, jnp.sin(ang), zhi], axis=1)
    return cos, sin


def _vt_shape_and_spec(b, heads, rows, s):
    shape = jax.ShapeDtypeStruct((b, heads, s // KV_CHUNK, rows, KV_CHUNK), BF16)
    spec = pl.BlockSpec((None, heads, None, rows, KV_CHUNK), lambda bi, i: (bi, 0, i, 0, 0))
    return shape, spec


def _mla_pre(x, weights, cos, sin):
    b, s, _ = x.shape
    tm = KV_CHUNK
    hw = MLA_HEADS * LANE
    out = jax.ShapeDtypeStruct((b, s, hw), BF16)
    vt_shape, vt_spec = _vt_shape_and_spec(b, MLA_HEADS, MLA_VT_ROWS, s)
    tile = lambda bi, i: (bi, i, 0)
    w_specs = [_const_spec(w.shape) for w in weights]
    return pl.pallas_call(
        _mla_pre_kernel,
        grid=(b, s // tm),
        in_specs=[pl.BlockSpec((None, tm, D_MODEL), tile)] + w_specs + [
            pl.BlockSpec((tm, LANE), lambda bi, i: (i, 0)),
            pl.BlockSpec((tm, LANE), lambda bi, i: (i, 0)),
        ],
        out_specs=[pl.BlockSpec((None, tm, hw), tile)] * 2 + [vt_spec],
        out_shape=[out, out, vt_shape],
        compiler_params=_params(("parallel", "parallel")),
        name="mla_pre",
    )(x, *weights, cos, sin)


def _flash_core(scores, vt_ref, s_buf, mx_buf, p_buf, al_buf, m_sc, acc_sc, n_kv):
    assert n_kv >= 2 and n_kv % 2 == 0

    def issue_scores(j, slot):
        s = scores(j)
        s_buf[slot] = s
        mx_buf[slot] = jnp.max(s, axis=0, keepdims=True)

    def softmax(slot):
        m_old = m_sc[...]
        m_new = jnp.maximum(m_old, mx_buf[slot])
        p_buf[slot] = jnp.exp2(s_buf[slot] - m_new).astype(BF16)
        al_buf[slot] = jnp.exp2(m_old - m_new)
        m_sc[...] = m_new

    def values(j, slot):
        acc_sc[...] = al_buf[slot] * acc_sc[...] + _dot(vt_ref[j], p_buf[slot])

    m_sc[...] = jnp.full(m_sc.shape, NEG_BIG, F32)
    acc_sc[...] = jnp.zeros(acc_sc.shape, F32)
    issue_scores(0, 0)
    issue_scores(1, 1)
    softmax(0)

    def pair(i, carry):
        j = 2 * i + 1
        issue_scores(j + 1, 0)
        softmax(1)
        values(j - 1, 0)
        issue_scores(j + 2, 1)
        softmax(0)
        values(j, 1)
        return carry

    lax.fori_loop(0, (n_kv - 2) // 2, pair, 0)
    softmax(1)
    values(n_kv - 2, 0)
    values(n_kv - 1, 1)


def _flash_scratch(rows, nq):
    return [
        pltpu.VMEM((2, KV_CHUNK, nq), F32),
        pltpu.VMEM((2, 1, nq), F32),
        pltpu.VMEM((2, KV_CHUNK, nq), BF16),
        pltpu.VMEM((2, 1, nq), F32),
        pltpu.VMEM((1, nq), F32),
        pltpu.VMEM((rows, nq), F32),
    ]


def _key_chunk(k_ref, j):
    return k_ref[pl.ds(pl.multiple_of(j * KV_CHUNK, KV_CHUNK), KV_CHUNK), :]


def _mla_finalize(acc_sc, o_ref):
    acc = acc_sc[...]
    o_t = acc[:MLA_V] / acc[MLA_V:MLA_V + 1]
    o_ref[...] = jnp.concatenate([o_t, jnp.zeros((LANE - MLA_V, o_t.shape[1]), F32)], axis=0).T.astype(BF16)


def _flash_mla_kernel(q_ref, k_ref, vt_ref, o_ref, *scratch, n_kv):
    acc_sc = scratch[-1]
    _flash_core(lambda j: _dot_nt(_key_chunk(k_ref, j), q_ref[...]), vt_ref, *scratch, n_kv)
    _mla_finalize(acc_sc, o_ref)


def _flash_mla(q, k, vt, tq):
    b, s, hw = q.shape
    n_kv = s // KV_CHUNK
    kern = functools.partial(_flash_mla_kernel, n_kv=n_kv)
    return pl.pallas_call(
        kern,
        grid=(b, MLA_HEADS, s // tq),
        in_specs=[
            pl.BlockSpec((None, tq, LANE), lambda bi, h, i: (bi, i, h)),
            pl.BlockSpec((None, s, LANE), lambda bi, h, i: (bi, 0, h)),
            pl.BlockSpec((None, None, n_kv, MLA_VT_ROWS, KV_CHUNK), lambda bi, h, i: (bi, h, 0, 0, 0)),
        ],
        out_specs=pl.BlockSpec((None, tq, LANE), lambda bi, h, i: (bi, i, h)),
        out_shape=jax.ShapeDtypeStruct((b, s, hw), BF16),
        scratch_shapes=_flash_scratch(MLA_VT_ROWS, tq),
        compiler_params=_params(("parallel", "parallel", "arbitrary")),
        name="flash_mla",
    )(q, k, vt)


def _static_core(first_probs, probs, chunk_of, vt_ref, p_buf, acc_sc, n_kv):
    assert n_kv >= 2 and n_kv % 2 == 0

    def values(t, slot, first=False):
        pv = _dot(vt_ref[chunk_of(t)], p_buf[slot])
        acc_sc[...] = pv if first else acc_sc[...] + pv

    p_buf[0] = first_probs()
    p_buf[1] = probs(1)
    values(0, 0, first=True)

    def pair(i, carry):
        t = 2 * i + 1
        p_buf[0] = probs(t + 1)
        values(t, 1)
        p_buf[1] = probs(t + 2)
        values(t + 1, 0)
        return carry

    lax.fori_loop(0, (n_kv - 2) // 2, pair, 0, unroll=True)
    values(n_kv - 1, 1)


def _static_scratch(rows, nq):
    return [pltpu.VMEM((2, KV_CHUNK, nq), BF16), pltpu.VMEM((rows, nq), F32)]


def _flash_mla_static_kernel(q_ref, k_ref, vt_ref, o_ref, p_buf, acc_sc, *, n_kv):
    probs = lambda t: jnp.exp2(_dot_nt(_key_chunk(k_ref, t), q_ref[...])).astype(BF16)
    _static_core(lambda: probs(0), probs, lambda t: t, vt_ref, p_buf, acc_sc, n_kv)
    _mla_finalize(acc_sc, o_ref)


def _flash_mla_static(q, k, vt, tq):
    b, s, hw = q.shape
    n_kv = s // KV_CHUNK
    kern = functools.partial(_flash_mla_static_kernel, n_kv=n_kv)
    return pl.pallas_call(
        kern,
        grid=(b, MLA_HEADS, s // tq),
        in_specs=[
            pl.BlockSpec((None, tq, LANE), lambda bi, h, i: (bi, i, h)),
            pl.BlockSpec((None, s, LANE), lambda bi, h, i: (bi, 0, h)),
            pl.BlockSpec((None, None, n_kv, MLA_VT_ROWS, KV_CHUNK), lambda bi, h, i: (bi, h, 0, 0, 0)),
        ],
        out_specs=pl.BlockSpec((None, tq, LANE), lambda bi, h, i: (bi, i, h)),
        out_shape=jax.ShapeDtypeStruct((b, s, hw), BF16),
        scratch_shapes=_static_scratch(MLA_VT_ROWS, tq),
        compiler_params=_params(("parallel", "parallel", "arbitrary")),
        name="flash_mla_static",
    )(q, k, vt)


def _pos_lanes(pos0, rows, first_lane):
    pos = pos0 + lax.broadcasted_iota(jnp.int32, (rows, LANE), 0)
    lane = lax.broadcasted_iota(jnp.int32, (rows, LANE), 1) - first_lane
    hi = lax.shift_right_logical(pos, int(math.log2(POS_SPLIT))).astype(F32)
    lo = (pos & (POS_SPLIT - 1)).astype(F32)
    p = DIFF_AUG_PARTS
    in_hi = (lane >= 0) & (lane < p)
    in_lo = (lane >= p) & (lane < 2 * p)
    return jnp.where(in_hi, hi, jnp.where(in_lo, lo, 0.0))


def _diff_pre_kernel(x_ref, ng_ref, w_ref, gq_ref, gk_ref, kaug_ref, q_ref, k_ref, vt_ref):
    h = _rms(x_ref[...], ng_ref[...]).astype(BF16)
    qkv = _dot(h, w_ref[...])
    hw = DIFF_HEADS * LANE
    lo = lax.broadcasted_iota(jnp.int32, (1, LANE), 1) < DIFF_HD
    q_scale = (DIFF_HD ** -0.5) * LOG2E
    tm = qkv.shape[0]
    k_pos = _pos_lanes(pl.program_id(1) * tm, tm, 0)
    ones_rows = (lax.broadcasted_iota(jnp.int32, (DIFF_VT_ROWS - LANE, tm), 0) == 0).astype(F32)

    def half_norm(t, g):
        t2 = t * t
        ss_lo = jnp.sum(jnp.where(lo, t2, 0.0), axis=-1, keepdims=True)
        ss_hi = jnp.sum(jnp.where(lo, 0.0, t2), axis=-1, keepdims=True)
        r = lax.rsqrt(jnp.where(lo, ss_lo, ss_hi) * (1.0 / DIFF_HD) + EPS)
        return t * r * g

    for hd in range(DIFF_HEADS):
        sl = slice(hd * LANE, (hd + 1) * LANE)
        q_ref[:, sl] = (half_norm(qkv[:, sl], gq_ref[...]) * q_scale).astype(BF16)
        ks = slice(2 * hd * LANE, (2 * hd + 1) * LANE)
        k_ref[:, ks] = half_norm(qkv[:, hw + hd * LANE:hw + (hd + 1) * LANE], gk_ref[...]).astype(BF16)
        k_ref[:, (2 * hd + 1) * LANE:(2 * hd + 2) * LANE] = (k_pos + kaug_ref[hd]).astype(BF16)
        v_t = qkv[:, 2 * hw + hd * LANE:2 * hw + (hd + 1) * LANE].T
        vt_ref[hd] = jnp.concatenate([v_t, ones_rows], axis=0).astype(BF16)


def _slope_pieces():
    slopes = 2.0 ** (-8.0 * jnp.arange(1, DIFF_HEADS + 1, dtype=F32) / DIFF_HEADS) * LOG2E
    pieces, rest = [], slopes
    for _ in range(DIFF_AUG_PARTS):
        piece = rest.astype(BF16).astype(F32)
        pieces.append(piece)
        rest = rest - piece
    return jnp.stack(pieces, axis=1)


def _diff_bias_tables(bound):
    p = DIFF_AUG_PARTS
    pieces = _slope_pieces()
    zeros = jnp.zeros((DIFF_HEADS, LANE - 4 * p - 1), F32)
    one = jnp.ones((DIFF_HEADS, 1), F32)
    zp = jnp.zeros((DIFF_HEADS, p), F32)
    k_tab = jnp.concatenate([zp, zp, POS_SPLIT * pieces, pieces, one, zeros], axis=1)[:, None, :]
    shift = -bound * one
    after = jnp.concatenate([-POS_SPLIT * pieces, -pieces, zp, zp, shift, zeros], axis=1)
    before = jnp.concatenate([POS_SPLIT * pieces, pieces, zp, zp, shift, zeros], axis=1)
    diag = jnp.concatenate([zp, zp, zp, zp, shift, zeros], axis=1)
    return k_tab, jnp.stack([after, before, diag], axis=1)


def _diff_pre(x, norm_g, w_qkv, q_g, k_g, k_tab):
    b, s, _ = x.shape
    tm = KV_CHUNK
    hw = DIFF_HEADS * LANE
    out = jax.ShapeDtypeStruct((b, s, hw), BF16)
    out_k = jax.ShapeDtypeStruct((b, s, 2 * hw), BF16)
    vt_shape, vt_spec = _vt_shape_and_spec(b, DIFF_HEADS, DIFF_VT_ROWS, s)
    tile = lambda bi, i: (bi, i, 0)
    weights = (norm_g[None, :], w_qkv.astype(BF16), q_g.reshape(1, LANE), k_g.reshape(1, LANE), k_tab)
    return pl.pallas_call(
        _diff_pre_kernel,
        grid=(b, s // tm),
        in_specs=[pl.BlockSpec((None, tm, D_MODEL), tile)] + [_const_spec(w.shape) for w in weights],
        out_specs=[pl.BlockSpec((None, tm, hw), tile), pl.BlockSpec((None, tm, 2 * hw), tile), vt_spec],
        out_shape=[out, out_k, vt_shape],
        compiler_params=_params(("parallel", "parallel")),
        name="diff_pre",
    )(x, *weights)


def _flash_diff_kernel(slope_ref, q_ref, k_ref, vt_ref, dmat_ref, lam_ref, subg_ref, o_ref,
                       qcat_sc, *scratch, tq, n_kv, lambda_init):
    acc_sc = scratch[-1]
    lo = lax.broadcasted_iota(jnp.int32, (1, LANE), 1) < DIFF_HD
    q = q_ref[...]
    zero = jnp.zeros_like(q)
    qcat_sc[:tq] = jnp.where(lo, q, zero)
    qcat_sc[tq:] = jnp.where(lo, zero, q)
    neg_slope = slope_ref[pl.program_id(1)]
    q0 = pl.program_id(2) * tq

    def scores(j):
        delta = (j * KV_CHUNK - q0).astype(F32)
        bias = jnp.abs(dmat_ref[...] + delta) * neg_slope
        return _dot_nt(_key_chunk(k_ref, j), qcat_sc[...]) + jnp.concatenate([bias, bias], axis=1)

    _flash_core(scores, vt_ref, *scratch, n_kv)
    _diff_finalize(acc_sc, lam_ref, subg_ref, o_ref, tq, lambda_init)


def _diff_finalize(acc_sc, lam_ref, subg_ref, o_ref, tq, lambda_init):
    lp = lam_ref[...]
    lam = (jnp.exp(jnp.sum(lp[0:1] * lp[1:2], axis=-1, keepdims=True))
           - jnp.exp(jnp.sum(lp[2:3] * lp[3:4], axis=-1, keepdims=True)) + lambda_init)
    acc = acc_sc[...]
    o = acc[:LANE] / acc[LANE:LANE + 1]
    o = (o[:, :tq] - lam * o[:, tq:]).T
    o = _rms(o, subg_ref[...]) * (1.0 - lambda_init)
    o_ref[...] = o.astype(BF16)


def _flash_diff_static_kernel(slope_ref, q_ref, k_ref, vt_ref, dmat_ref, lam_ref, subg_ref, qtab_ref,
                              o_ref, qa_sc, p_buf, acc_sc, *, tq, n_kv, lambda_init):
    lo = lax.broadcasted_iota(jnp.int32, (1, LANE), 1) < DIFF_HD
    q = q_ref[...]
    zero = jnp.zeros_like(q)
    halves = (jnp.where(lo, q, zero), jnp.where(lo, zero, q))
    q0 = pl.program_id(2) * tq
    q_pos = _pos_lanes(q0, tq, 2 * DIFF_AUG_PARTS)
    tab = qtab_ref[...]
    for v, sign in enumerate((1.0, -1.0, 0.0)):
        aug = (tab[v:v + 1] + sign * q_pos).astype(BF16)
        for c in range(2):
            qa_sc[v, c * tq:(c + 1) * tq, :LANE] = halves[c]
            qa_sc[v, c * tq:(c + 1) * tq, LANE:] = aug
    jd = lax.shift_right_logical(q0, int(math.log2(KV_CHUNK)))
    neg_slope = slope_ref[pl.program_id(1)]

    def first_probs():
        delta = (jd * KV_CHUNK - q0).astype(F32)
        bias = jnp.abs(dmat_ref[...] + delta) * neg_slope
        s = _dot_nt(_key_chunk(k_ref, jd), qa_sc[2]) + jnp.concatenate([bias, bias], axis=1)
        return jnp.exp2(s).astype(BF16)

    def chunk_of(t):
        if isinstance(t, int) and t == 0:
            return jd
        return t - 1 + (t - 1 >= jd).astype(jnp.int32)

    def probs(t):
        j = chunk_of(t)
        before = (j < jd).astype(jnp.int32)
        return jnp.exp2(_dot_nt(_key_chunk(k_ref, j), qa_sc[before])).astype(BF16)

    _static_core(first_probs, probs, chunk_of, vt_ref, p_buf, acc_sc, n_kv)
    _diff_finalize(acc_sc, lam_ref, subg_ref, o_ref, tq, lambda_init)


def _flash_diff(q, k, vt, lam_p, sub_g, q_tab, lambda_init, tq, static_shift):
    b, s, hw = q.shape
    n_kv = s // KV_CHUNK
    slopes = 2.0 ** (-8.0 * jnp.arange(1, DIFF_HEADS + 1, dtype=F32) / DIFF_HEADS)
    neg_slopes = -slopes * LOG2E
    dmat = (jnp.arange(KV_CHUNK, dtype=F32)[:, None] - jnp.arange(tq, dtype=F32)[None, :])
    const = dict(pipeline_mode=pl.Buffered(1))
    in_specs = [
        pl.BlockSpec((None, tq, LANE), lambda bi, h, i, sl: (bi, i, h)),
        None,
        pl.BlockSpec((None, None, n_kv, DIFF_VT_ROWS, KV_CHUNK), lambda bi, h, i, sl: (bi, h, 0, 0, 0)),
        pl.BlockSpec((KV_CHUNK, tq), lambda bi, h, i, sl: (0, 0), **const),
        pl.BlockSpec((4, DIFF_HD), lambda bi, h, i, sl: (0, 0), **const),
        pl.BlockSpec((1, LANE), lambda bi, h, i, sl: (0, 0), **const),
    ]
    args = [neg_slopes, q, k, vt, dmat, lam_p, sub_g[None, :]]
    if static_shift:
        kern = functools.partial(_flash_diff_static_kernel, tq=tq, n_kv=n_kv, lambda_init=lambda_init)
        in_specs[1] = pl.BlockSpec((None, s, 2 * LANE), lambda bi, h, i, sl: (bi, 0, h))
        in_specs.append(pl.BlockSpec((None, 3, LANE), lambda bi, h, i, sl: (h, 0, 0)))
        args.append(q_tab)
        scratch = [pltpu.VMEM((3, 2 * tq, 2 * LANE), BF16)] + _static_scratch(DIFF_VT_ROWS, 2 * tq)
        name = "flash_diff_static"
    else:
        kern = functools.partial(_flash_diff_kernel, tq=tq, n_kv=n_kv, lambda_init=lambda_init)
        in_specs[1] = pl.BlockSpec((None, s, LANE), lambda bi, h, i, sl: (bi, 0, 2 * h))
        scratch = [pltpu.VMEM((2 * tq, LANE), BF16)] + _flash_scratch(DIFF_VT_ROWS, 2 * tq)
        name = "flash_diff"
    grid_spec = pltpu.PrefetchScalarGridSpec(
        num_scalar_prefetch=1,
        grid=(b, DIFF_HEADS, s // tq),
        in_specs=in_specs,
        out_specs=pl.BlockSpec((None, tq, LANE), lambda bi, h, i, sl: (bi, i, h)),
        scratch_shapes=scratch,
    )
    return pl.pallas_call(
        kern,
        grid_spec=grid_spec,
        out_shape=jax.ShapeDtypeStruct((b, s, hw), BF16),
        compiler_params=_params(("parallel", "parallel", "arbitrary")),
        name=name,
    )(*args)


def _post_xattn_kernel(x_ref, om_ref, wom_ref, ng_ref, wq_ref, qg_ref, k_ref, v_ref, wo_ref,
                       y_ref, o_sc):
    x1 = x_ref[...] + _dot(om_ref[...], wom_ref[...])
    h = _rms(x1, ng_ref[...]).astype(BF16)
    q = _dot(h, wq_ref[...])
    qg = qg_ref[...] * ((XA_HD ** -0.5) * LOG2E)
    for hd in range(XA_HEADS):
        sl = slice(hd * XA_HD, (hd + 1) * XA_HD)
        qh = _rms(q[:, sl], qg).astype(BF16)
        s = _dot_nt(qh, k_ref[:, sl])
        p = jnp.exp2(s - jnp.max(s, axis=-1, keepdims=True))
        l = jnp.sum(p, axis=-1, keepdims=True)
        o_sc[:, sl] = (_dot(p.astype(BF16), v_ref[:, sl]) / l).astype(BF16)
    y_ref[...] = x1 + _dot(o_sc[...], wo_ref[...])


def _post_xattn(x, o_mix, w_o_mix, norm_g, w_q, q_g, k_mem, v_mem, layer, w_o, tm):
    b, s, _ = x.shape
    tile = lambda bi, i: (bi, i, 0)
    mem_spec = pl.BlockSpec((None, None, N_MEM, D_MODEL), lambda bi, i: (layer, bi, 0, 0))
    sq = (D_MODEL, D_MODEL)
    return pl.pallas_call(
        _post_xattn_kernel,
        grid=(b, s // tm),
        in_specs=[
            pl.BlockSpec((None, tm, D_MODEL), tile),
            pl.BlockSpec((None, tm, o_mix.shape[-1]), tile),
            _const_spec(w_o_mix.shape),
            _const_spec((1, D_MODEL)),
            _const_spec(sq),
            _const_spec((1, XA_HD)),
            mem_spec,
            mem_spec,
            _const_spec(sq),
        ],
        out_specs=pl.BlockSpec((None, tm, D_MODEL), tile),
        out_shape=jax.ShapeDtypeStruct(x.shape, F32),
        scratch_shapes=[pltpu.VMEM((tm, D_MODEL), BF16)],
        compiler_params=_params(("parallel", "parallel")),
        name="post_xattn",
    )(x, o_mix, w_o_mix, norm_g[None, :], w_q.astype(BF16), q_g[None, :], k_mem, v_mem,
      w_o.astype(BF16))


def _ffn_kernel(x_ref, prev_ref, next_ref, ng_ref, wg_ref, wu_ref, cw_ref, cb_ref, wd_ref, y_ref,
                *, tm, n_chunks):
    i = pl.program_id(1)
    keep_prev = (i > 0).astype(F32)
    keep_next = (i < pl.num_programs(1) - 1).astype(F32)
    x = x_ref[...]
    xe = jnp.concatenate([prev_ref[...] * keep_prev, x, next_ref[...] * keep_next], axis=0)
    he = _rms(xe, ng_ref[...]).astype(BF16)
    hc = he[HALO:HALO + tm]
    cw = cw_ref[...]
    cb = cb_ref[...]
    y = x
    ck = D_FF // n_chunks
    for c in range(n_chunks):
        cs = slice(c * ck, (c + 1) * ck)
        ge = _dot(he, wg_ref[:, cs])
        u = _dot(hc, wu_ref[:, cs])
        g = (ge[HALO - 1:HALO - 1 + tm] * cw[0:1, cs] + ge[HALO:HALO + tm] * cw[1:2, cs]
             + ge[HALO + 1:HALO + 1 + tm] * cw[2:3, cs] + cb[:, cs])
        act = (g * jax.nn.sigmoid(g) * u).astype(BF16)
        y = y + _dot(act, wd_ref[cs, :])
    y_ref[...] = y


def _ffn(x, norm_g, w_gu, conv_w, conv_b, w_down, tm, n_chunks=2):
    b, s, _ = x.shape
    nh = tm // HALO
    last = s // HALO - 1
    w_g = w_gu[:, :D_FF].astype(BF16)
    w_u = w_gu[:, D_FF:].astype(BF16)
    weights = (norm_g[None, :], w_g, w_u, conv_w, conv_b[None, :], w_down.astype(BF16))
    kern = functools.partial(_ffn_kernel, tm=tm, n_chunks=n_chunks)
    return pl.pallas_call(
        kern,
        grid=(b, s // tm),
        in_specs=[
            pl.BlockSpec((None, tm, D_MODEL), lambda bi, i: (bi, i, 0)),
            pl.BlockSpec((None, HALO, D_MODEL), lambda bi, i: (bi, jnp.maximum(i * nh - 1, 0), 0)),
            pl.BlockSpec((None, HALO, D_MODEL), lambda bi, i: (bi, jnp.minimum((i + 1) * nh, last), 0)),
        ] + [_const_spec(w.shape) for w in weights],
        out_specs=pl.BlockSpec((None, tm, D_MODEL), lambda bi, i: (bi, i, 0)),
        out_shape=jax.ShapeDtypeStruct(x.shape, F32),
        compiler_params=_params(("parallel", "parallel")),
        name="ffn",
    )(x, x, x, *weights)


def _tiles(s):
    return dict(tm=min(512, s), tq_mla=512, tq_diff=256)


def _mixer_weights(mla_p, diff_p):
    layers = []
    for i in range(DEPTH):
        j = i // 2
        if i % 2 == 0:
            weights, w_o_mix, bound = _mla_weights(*[p[j] for p in mla_p])
            layers.append(dict(weights=weights, w_o_mix=w_o_mix, bound=bound))
        else:
            norm_g, w_qkv, q_g, k_g, lam_p, sub_g, w_o = [p[j] for p in diff_p]
            bound = _score_bound(DIFF_HD, q_g, k_g)
            k_tab, q_tab = _diff_bias_tables(bound)
            layers.append(dict(pre=(norm_g, w_qkv, q_g, k_g, k_tab), lam_p=lam_p, sub_g=sub_g,
                               q_tab=q_tab, w_o_mix=w_o.astype(BF16), bound=bound))
    return layers


def _trunk(x, mem, mixers, xa_p, ffn_p):
    s = x.shape[1]
    assert s % KV_CHUNK == 0
    t = _tiles(s)
    xa_norm, xa_mem_norm, xa_w_q, xa_w_kv, xa_q_norm, xa_k_norm, xa_w_o = xa_p
    k_mem, v_mem = _mem_kv(mem, xa_mem_norm, xa_w_kv, xa_k_norm)
    cos, sin = _rope_tables(s)
    for i in range(DEPTH):
        mx = mixers[i]
        static_ok = mx["bound"] <= MAX_STATIC_BOUND
        if i % 2 == 0:
            q, k, vt = _mla_pre(x, mx["weights"], cos, sin)
            o_mix = lax.cond(static_ok,
                             lambda q, k, vt: _flash_mla_static(q, k, vt, t["tq_mla"]),
                             lambda q, k, vt: _flash_mla(q, k, vt, t["tq_mla"]), q, k, vt)
        else:
            lambda_init = 0.8 - 0.6 * math.exp(-0.3 * i)
            q, k, vt = _diff_pre(x, *mx["pre"])
            flash = lambda static: functools.partial(
                _flash_diff, lambda_init=lambda_init, tq=t["tq_diff"], static_shift=static)
            o_mix = lax.cond(static_ok, flash(True), flash(False),
                             q, k, vt, mx["lam_p"], mx["sub_g"], mx["q_tab"])
        x = _post_xattn(x, o_mix, mx["w_o_mix"], xa_norm[i], xa_w_q[i], xa_q_norm[i], k_mem, v_mem, i,
                        xa_w_o[i], t["tm"])
        x = _ffn(x, *[p[i] for p in ffn_p], t["tm"])
    return x


def kernel(x_prompt, x_sample, mem_prompt, mem_sample, mla_norm, mla_w_down, mla_q_lat_norm, mla_kv_lat_norm, mla_w_uq, mla_w_ukv, mla_q_norm, mla_k_norm, mla_w_o, diff_norm, diff_w_qkv, diff_q_norm, diff_k_norm, diff_lambda, diff_sub_norm, diff_w_o, xa_norm, xa_mem_norm, xa_w_q, xa_w_kv, xa_q_norm, xa_k_norm, xa_w_o, ffn_norm, ffn_w_gu, ffn_conv_w, ffn_conv_b, ffn_w_down):
    mla_p = (mla_norm, mla_w_down, mla_q_lat_norm, mla_kv_lat_norm, mla_w_uq, mla_w_ukv,
             mla_q_norm, mla_k_norm, mla_w_o)
    diff_p = (diff_norm, diff_w_qkv, diff_q_norm, diff_k_norm, diff_lambda, diff_sub_norm, diff_w_o)
    xa_p = (xa_norm, xa_mem_norm, xa_w_q, xa_w_kv, xa_q_norm, xa_k_norm, xa_w_o)
    ffn_p = (ffn_norm, ffn_w_gu, ffn_conv_w, ffn_conv_b, ffn_w_down)
    mixers = _mixer_weights(mla_p, diff_p)
    y_prompt = _trunk(x_prompt, mem_prompt, mixers, xa_p, ffn_p)
    y_sample = _trunk(x_sample, mem_sample, mixers, xa_p, ffn_p)
    return (y_prompt, y_sample)
```

```python
import functools
import math

import jax
import jax.numpy as jnp
from jax import lax
from jax.experimental import pallas as pl
from jax.experimental.pallas import tpu as pltpu

D_MODEL = 1024
DEPTH = 4
N_MEM = 256
EPS = 1e-6
MLA_HEADS = 8
MLA_Q_LORA = 384
MLA_KV_LORA = 256
MLA_NOPE = 64
MLA_ROPE = 32
MLA_QK = MLA_NOPE + MLA_ROPE
MLA_V = 64
ROPE_THETA = 10000.0
DIFF_HEADS = 8
DIFF_HD = 64
XA_HEADS = 4
XA_HD = D_MODEL // XA_HEADS
D_FF = 2816
CONV_W = 3

LANE = 128
HALO = 8
BF16_ROWS = 16
KV_CHUNK = 512
DIFF_VT_ROWS = LANE + BF16_ROWS
MLA_VT_ROWS = MLA_V + BF16_ROWS
SHIFT_LANE = MLA_QK
SCORE_MARGIN = 1.02
MAX_STATIC_BOUND = 60.0
DIFF_AUG_PARTS = 3
POS_SPLIT = 64
LOG2E = math.log2(math.e)
NEG_BIG = -1e30
VMEM_LIMIT = 56 * 1024 * 1024

F32 = jnp.float32
BF16 = jnp.bfloat16


def _const_spec(shape):
    nd = len(shape)
    return pl.BlockSpec(shape, lambda *_: (0,) * nd, pipeline_mode=pl.Buffered(1))


def _params(sem):
    return pltpu.CompilerParams(dimension_semantics=sem, vmem_limit_bytes=VMEM_LIMIT)


def _rms(x, g):
    return x * lax.rsqrt(jnp.mean(x * x, axis=-1, keepdims=True) + EPS) * g


def _dot(a, b):
    return jnp.dot(a, b, preferred_element_type=F32)


def _dot_nt(a, b):
    return lax.dot_general(a, b, (((1,), (1,)), ((), ())), preferred_element_type=F32)


def _mem_kv_kernel(mem_ref, g_ref, w_ref, kg_ref, k_ref, v_ref):
    h = _rms(mem_ref[...], g_ref[...]).astype(BF16)
    kv = _dot(h, w_ref[...])
    kg = kg_ref[...]
    for hd in range(XA_HEADS):
        sl = slice(hd * XA_HD, (hd + 1) * XA_HD)
        k_ref[:, sl] = _rms(kv[:, sl], kg).astype(BF16)
    v_ref[...] = kv[:, D_MODEL:].astype(BF16)


def _mem_kv(mem, mem_g, w_kv, k_g):
    bm = mem.shape[0]
    out = jax.ShapeDtypeStruct((DEPTH, bm, N_MEM, D_MODEL), BF16)
    return pl.pallas_call(
        _mem_kv_kernel,
        grid=(DEPTH, bm),
        in_specs=[
            pl.BlockSpec((None, N_MEM, D_MODEL), lambda l, b: (b, 0, 0)),
            pl.BlockSpec((None, 1, D_MODEL), lambda l, b: (l, 0, 0)),
            pl.BlockSpec((None, D_MODEL, 2 * D_MODEL), lambda l, b: (l, 0, 0)),
            pl.BlockSpec((None, 1, XA_HD), lambda l, b: (l, 0, 0)),
        ],
        out_specs=[
            pl.BlockSpec((None, None, N_MEM, D_MODEL), lambda l, b: (l, b, 0, 0)),
            pl.BlockSpec((None, None, N_MEM, D_MODEL), lambda l, b: (l, b, 0, 0)),
        ],
        out_shape=[out, out],
        compiler_params=_params(("arbitrary", "arbitrary")),
        name="mem_kv",
    )(mem, mem_g[:, None, :], w_kv.astype(BF16), k_g[:, None, :])


def _mla_pre_kernel(x_ref, ng_ref, wd_ref, qlg_ref, kvlg_ref, wuq_ref, wukv_ref,
                    gq_ref, gk_ref, qshift_ref, cos_ref, sin_ref, q_ref, k_ref, vt_ref):
    h = _rms(x_ref[...], ng_ref[...]).astype(BF16)
    down = _dot(h, wd_ref[...])
    c_q = _rms(down[:, :MLA_Q_LORA], qlg_ref[...]).astype(BF16)
    kv0 = MLA_Q_LORA
    c_kv = _rms(down[:, kv0:kv0 + MLA_KV_LORA], kvlg_ref[...]).astype(BF16)
    kr = down[:, kv0 + MLA_KV_LORA:kv0 + MLA_KV_LORA + LANE]
    kr_rot = down[:, kv0 + MLA_KV_LORA + LANE:]
    qq = _dot(c_q, wuq_ref[...])
    kvp = _dot(c_kv, wukv_ref[...])
    cos = cos_ref[...]
    sin = sin_ref[...]
    gq = gq_ref[...]
    gk = gk_ref[...]
    hw = MLA_HEADS * LANE
    lane = lax.broadcasted_iota(jnp.int32, (1, LANE), 1)
    one_col = (lane == MLA_V).astype(F32)
    k_one = (lane == SHIFT_LANE).astype(F32)
    q_shift = qshift_ref[...]
    q_scale = (MLA_QK ** -0.5) * LOG2E
    for hd in range(MLA_HEADS):
        sl = slice(hd * LANE, (hd + 1) * LANE)
        qh = qq[:, sl]
        rq = lax.rsqrt(jnp.sum(qh * qh, axis=-1, keepdims=True) * (1.0 / MLA_QK) + EPS)
        q_rot = qq[:, hw + hd * LANE:hw + (hd + 1) * LANE]
        q_ref[:, sl] = ((rq * q_scale) * (qh * gq * cos + q_rot * sin) + q_shift).astype(BF16)
        kh = kvp[:, sl] + kr
        rk = lax.rsqrt(jnp.sum(kh * kh, axis=-1, keepdims=True) * (1.0 / MLA_QK) + EPS)
        k_ref[:, sl] = (rk * (kh * gk * cos + kr_rot * sin) + k_one).astype(BF16)
        v_t = (kvp[:, hw + hd * LANE:hw + (hd + 1) * LANE] + one_col).T
        vt_ref[hd] = v_t[:MLA_VT_ROWS].astype(BF16)


def _rot_half_cols(w):
    half = MLA_ROPE // 2
    return jnp.concatenate([-w[..., half:], w[..., :half]], axis=-1)


def _mla_weights(norm_g, w_down, q_lat_g, kv_lat_g, w_uq, w_ukv, q_g, k_g, w_o):
    nq, nkv = MLA_Q_LORA, MLA_KV_LORA
    w_dq, w_dkv, w_kr = w_down[:, :nq], w_down[:, nq:nq + nkv], w_down[:, nq + nkv:]
    pad_lo = jnp.zeros((D_MODEL, MLA_NOPE), F32)
    pad_hi = jnp.zeros((D_MODEL, LANE - MLA_QK), F32)
    kr_blk = jnp.concatenate([pad_lo, w_kr, pad_hi], axis=1)
    krot_blk = jnp.concatenate([pad_lo, _rot_half_cols(w_kr * k_g[MLA_NOPE:]), pad_hi], axis=1)
    wd = jnp.concatenate([w_dq, w_dkv, kr_blk, krot_blk], axis=1).astype(BF16)

    wq = w_uq.reshape(nq, MLA_HEADS, MLA_QK)
    zq = jnp.zeros((nq, MLA_HEADS, LANE - MLA_QK), F32)
    wq_main = jnp.concatenate([wq, zq], axis=-1).reshape(nq, MLA_HEADS * LANE)
    wq_rot = jnp.concatenate(
        [jnp.zeros((nq, MLA_HEADS, MLA_NOPE), F32),
         _rot_half_cols(wq[..., MLA_NOPE:] * q_g[MLA_NOPE:]), zq], axis=-1
    ).reshape(nq, MLA_HEADS * LANE)
    wuq = jnp.concatenate([wq_main, wq_rot], axis=1).astype(BF16)

    wkv = w_ukv.reshape(nkv, MLA_HEADS, MLA_NOPE + MLA_V)
    zk = jnp.zeros((nkv, MLA_HEADS, LANE - MLA_NOPE), F32)
    zv = jnp.zeros((nkv, MLA_HEADS, LANE - MLA_V), F32)
    wk = jnp.concatenate([wkv[..., :MLA_NOPE], zk], axis=-1).reshape(nkv, MLA_HEADS * LANE)
    wv = jnp.concatenate([wkv[..., MLA_NOPE:], zv], axis=-1).reshape(nkv, MLA_HEADS * LANE)
    wukv = jnp.concatenate([wk, wv], axis=1).astype(BF16)

    zg = jnp.zeros((LANE - MLA_QK,), F32)
    gq = jnp.concatenate([q_g, zg])[None, :]
    gk = jnp.concatenate([k_g, zg])[None, :]

    wo = w_o.reshape(MLA_HEADS, MLA_V, D_MODEL)
    wo = jnp.concatenate([wo, jnp.zeros((MLA_HEADS, LANE - MLA_V, D_MODEL), F32)], axis=1)
    wo = wo.reshape(MLA_HEADS * LANE, D_MODEL).astype(BF16)
    bound = _score_bound(MLA_QK, q_g, k_g)
    q_shift = -bound * (jnp.arange(LANE) == SHIFT_LANE).astype(F32)[None, :]
    weights = (norm_g[None, :], wd, q_lat_g[None, :], kv_lat_g[None, :], wuq, wukv, gq, gk, q_shift)
    return weights, wo, bound


def _score_bound(dim, q_g, k_g):
    return SCORE_MARGIN * math.sqrt(dim) * LOG2E * jnp.max(jnp.abs(q_g)) * jnp.max(jnp.abs(k_g))


def _rope_tables(seq):
    inv = ROPE_THETA ** (-jnp.arange(0, MLA_ROPE, 2, dtype=F32) / MLA_ROPE)
    ang = jnp.arange(seq, dtype=F32)[:, None] * inv[None, :]
    ang = jnp.concatenate([ang, ang], axis=-1)
    ones = jnp.ones((seq, MLA_NOPE), F32)
    zlo = jnp.zeros((seq, MLA_NOPE), F32)
    zhi = jnp.zeros((seq, LANE - MLA_QK), F32)
    cos = jnp.concatenate([ones, jnp.cos(ang), zhi], axis=1)
    sin = jnp.concatenate([zlo, jnp.sin(ang), zhi], axis=1)
    return cos, sin


def _vt_shape_and_spec(b, heads, rows, s):
    shape = jax.ShapeDtypeStruct((b, heads, s // KV_CHUNK, rows, KV_CHUNK), BF16)
    spec = pl.BlockSpec((None, heads, None, rows, KV_CHUNK), lambda bi, i: (bi, 0, i, 0, 0))
    return shape, spec


def _mla_pre(x, weights, cos, sin):
    b, s, _ = x.shape
    tm = KV_CHUNK
    hw = MLA_HEADS * LANE
    out = jax.ShapeDtypeStruct((b, s, hw), BF16)
    vt_shape, vt_spec = _vt_shape_and_spec(b, MLA_HEADS, MLA_VT_ROWS, s)
    tile = lambda bi, i: (bi, i, 0)
    w_specs = [_const_spec(w.shape) for w in weights]
    return pl.pallas_call(
        _mla_pre_kernel,
        grid=(b, s // tm),
        in_specs=[pl.BlockSpec((None, tm, D_MODEL), tile)] + w_specs + [
            pl.BlockSpec((tm, LANE), lambda bi, i: (i, 0)),
            pl.BlockSpec((tm, LANE), lambda bi, i: (i, 0)),
        ],
        out_specs=[pl.BlockSpec((None, tm, hw), tile)] * 2 + [vt_spec],
        out_shape=[out, out, vt_shape],
        compiler_params=_params(("parallel", "parallel")),
        name="mla_pre",
    )(x, *weights, cos, sin)


def _flash_core(scores, vt_ref, s_buf, mx_buf, p_buf, al_buf, m_sc, acc_sc, n_kv):
    assert n_kv >= 2 and n_kv % 2 == 0

    def issue_scores(j, slot):
        s = scores(j)
        s_buf[slot] = s
        mx_buf[slot] = jnp.max(s, axis=0, keepdims=True)

    def softmax(slot):
        m_old = m_sc[...]
        m_new = jnp.maximum(m_old, mx_buf[slot])
        p_buf[slot] = jnp.exp2(s_buf[slot] - m_new).astype(BF16)
        al_buf[slot] = jnp.exp2(m_old - m_new)
        m_sc[...] = m_new

    def values(j, slot):
        acc_sc[...] = al_buf[slot] * acc_sc[...] + _dot(vt_ref[j], p_buf[slot])

    m_sc[...] = jnp.full(m_sc.shape, NEG_BIG, F32)
    acc_sc[...] = jnp.zeros(acc_sc.shape, F32)
    issue_scores(0, 0)
    issue_scores(1, 1)
    softmax(0)

    def pair(i, carry):
        j = 2 * i + 1
        issue_scores(j + 1, 0)
        softmax(1)
        values(j - 1, 0)
        issue_scores(j + 2, 1)
        softmax(0)
        values(j, 1)
        return carry

    lax.fori_loop(0, (n_kv - 2) // 2, pair, 0)
    softmax(1)
    values(n_kv - 2, 0)
    values(n_kv - 1, 1)


def _flash_scratch(rows, nq):
    return [
        pltpu.VMEM((2, KV_CHUNK, nq), F32),
        pltpu.VMEM((2, 1, nq), F32),
        pltpu.VMEM((2, KV_CHUNK, nq), BF16),
        pltpu.VMEM((2, 1, nq), F32),
        pltpu.VMEM((1, nq), F32),
        pltpu.VMEM((rows, nq), F32),
    ]


def _key_chunk(k_ref, j):
    return k_ref[pl.ds(pl.multiple_of(j * KV_CHUNK, KV_CHUNK), KV_CHUNK), :]


def _mla_store(o_t, o_ref):
    o_ref[...] = jnp.concatenate([o_t, jnp.zeros((LANE - MLA_V, o_t.shape[1]), F32)], axis=0).T.astype(BF16)


def _mla_finalize(acc_sc, o_ref):
    acc = acc_sc[...]
    _mla_store(acc[:MLA_V] / acc[MLA_V:MLA_V + 1], o_ref)


def _flash_mla_kernel(q_ref, k_ref, vt_ref, o_ref, *scratch, n_kv):
    acc_sc = scratch[-1]
    _flash_core(lambda j: _dot_nt(_key_chunk(k_ref, j), q_ref[...]), vt_ref, *scratch, n_kv)
    _mla_finalize(acc_sc, o_ref)


def _flash_mla(q, k, vt, tq):
    b, s, hw = q.shape
    n_kv = s // KV_CHUNK
    kern = functools.partial(_flash_mla_kernel, n_kv=n_kv)
    return pl.pallas_call(
        kern,
        grid=(b, MLA_HEADS, s // tq),
        in_specs=[
            pl.BlockSpec((None, tq, LANE), lambda bi, h, i: (bi, i, h)),
            pl.BlockSpec((None, s, LANE), lambda bi, h, i: (bi, 0, h)),
            pl.BlockSpec((None, None, n_kv, MLA_VT_ROWS, KV_CHUNK), lambda bi, h, i: (bi, h, 0, 0, 0)),
        ],
        out_specs=pl.BlockSpec((None, tq, LANE), lambda bi, h, i: (bi, i, h)),
        out_shape=jax.ShapeDtypeStruct((b, s, hw), BF16),
        scratch_shapes=_flash_scratch(MLA_VT_ROWS, tq),
        compiler_params=_params(("parallel", "parallel", "arbitrary")),
        name="flash_mla",
    )(q, k, vt)


def _static_core(first_probs, probs, chunk_of, vt_ref, p_buf, acc_sc, n_kv):
    assert n_kv >= 2 and n_kv % 2 == 0

    def values(t, slot, first=False):
        pv = _dot(vt_ref[chunk_of(t)], p_buf[slot])
        acc_sc[...] = pv if first else acc_sc[...] + pv

    p_buf[0] = first_probs()
    p_buf[1] = probs(1)
    values(0, 0, first=True)

    def pair(i, carry):
        t = 2 * i + 1
        p_buf[0] = probs(t + 1)
        values(t, 1)
        p_buf[1] = probs(t + 2)
        values(t + 1, 0)
        return carry

    lax.fori_loop(0, (n_kv - 2) // 2, pair, 0, unroll=True)
    values(n_kv - 1, 1)


def _static_scratch(rows, nq):
    return [pltpu.VMEM((2, KV_CHUNK, nq), BF16), pltpu.VMEM((rows, nq), F32)]


def _flash_mla_static_kernel(q_ref, k_ref, vt_ref, o_ref, p_buf, acc_sc, *, n_kv):
    probs = lambda t: jnp.exp2(_dot_nt(_key_chunk(k_ref, t), q_ref[...])).astype(BF16)
    _static_core(lambda: probs(0), probs, lambda t: t, vt_ref, p_buf, acc_sc, n_kv)
    _mla_finalize(acc_sc, o_ref)


def _flash_mla_static(q, k, vt, tq):
    b, s, hw = q.shape
    n_kv = s // KV_CHUNK
    kern = functools.partial(_flash_mla_static_kernel, n_kv=n_kv)
    return pl.pallas_call(
        kern,
        grid=(b, MLA_HEADS, s // tq),
        in_specs=[
            pl.BlockSpec((None, tq, LANE), lambda bi, h, i: (bi, i, h)),
            pl.BlockSpec((None, s, LANE), lambda bi, h, i: (bi, 0, h)),
            pl.BlockSpec((None, None, n_kv, MLA_VT_ROWS, KV_CHUNK), lambda bi, h, i: (bi, h, 0, 0, 0)),
        ],
        out_specs=pl.BlockSpec((None, tq, LANE), lambda bi, h, i: (bi, i, h)),
        out_shape=jax.ShapeDtypeStruct((b, s, hw), BF16),
        scratch_shapes=_static_scratch(MLA_VT_ROWS, tq),
        compiler_params=_params(("parallel", "parallel", "arbitrary")),
        name="flash_mla_static",
    )(q, k, vt)


def _pos_lanes(pos0, rows, first_lane):
    pos = pos0 + lax.broadcasted_iota(jnp.int32, (rows, LANE), 0)
    lane = lax.broadcasted_iota(jnp.int32, (rows, LANE), 1) - first_lane
    hi = lax.shift_right_logical(pos, int(math.log2(POS_SPLIT))).astype(F32)
    lo = (pos & (POS_SPLIT - 1)).astype(F32)
    p = DIFF_AUG_PARTS
    in_hi = (lane >= 0) & (lane < p)
    in_lo = (lane >= p) & (lane < 2 * p)
    return jnp.where(in_hi, hi, jnp.where(in_lo, lo, 0.0))


def _diff_pre_kernel(x_ref, ng_ref, w_ref, gq_ref, gk_ref, kaug_ref, q_ref, k_ref, vt_ref):
    h = _rms(x_ref[...], ng_ref[...]).astype(BF16)
    qkv = _dot(h, w_ref[...])
    hw = DIFF_HEADS * LANE
    lo = lax.broadcasted_iota(jnp.int32, (1, LANE), 1) < DIFF_HD
    q_scale = (DIFF_HD ** -0.5) * LOG2E
    tm = qkv.shape[0]
    k_pos = _pos_lanes(pl.program_id(1) * tm, tm, 0)
    ones_rows = (lax.broadcasted_iota(jnp.int32, (DIFF_VT_ROWS - LANE, tm), 0) == 0).astype(F32)

    def half_norm(t, g):
        t2 = t * t
        ss_lo = jnp.sum(jnp.where(lo, t2, 0.0), axis=-1, keepdims=True)
        ss_hi = jnp.sum(jnp.where(lo, 0.0, t2), axis=-1, keepdims=True)
        r = lax.rsqrt(jnp.where(lo, ss_lo, ss_hi) * (1.0 / DIFF_HD) + EPS)
        return t * r * g

    for hd in range(DIFF_HEADS):
        sl = slice(hd * LANE, (hd + 1) * LANE)
        q_ref[:, sl] = (half_norm(qkv[:, sl], gq_ref[...]) * q_scale).astype(BF16)
        ks = slice(2 * hd * LANE, (2 * hd + 1) * LANE)
        k_ref[:, ks] = half_norm(qkv[:, hw + hd * LANE:hw + (hd + 1) * LANE], gk_ref[...]).astype(BF16)
        k_ref[:, (2 * hd + 1) * LANE:(2 * hd + 2) * LANE] = (k_pos + kaug_ref[hd]).astype(BF16)
        v_t = qkv[:, 2 * hw + hd * LANE:2 * hw + (hd + 1) * LANE].T
        vt_ref[hd] = jnp.concatenate([v_t, ones_rows], axis=0).astype(BF16)


def _slope_pieces():
    slopes = 2.0 ** (-8.0 * jnp.arange(1, DIFF_HEADS + 1, dtype=F32) / DIFF_HEADS) * LOG2E
    pieces, rest = [], slopes
    for _ in range(DIFF_AUG_PARTS):
        piece = rest.astype(BF16).astype(F32)
        pieces.append(piece)
        rest = rest - piece
    return jnp.stack(pieces, axis=1)


def _diff_bias_tables(bound):
    p = DIFF_AUG_PARTS
    pieces = _slope_pieces()
    zeros = jnp.zeros((DIFF_HEADS, LANE - 4 * p - 1), F32)
    one = jnp.ones((DIFF_HEADS, 1), F32)
    zp = jnp.zeros((DIFF_HEADS, p), F32)
    k_tab = jnp.concatenate([zp, zp, POS_SPLIT * pieces, pieces, one, zeros], axis=1)[:, None, :]
    shift = -bound * one
    after = jnp.concatenate([-POS_SPLIT * pieces, -pieces, zp, zp, shift, zeros], axis=1)
    before = jnp.concatenate([POS_SPLIT * pieces, pieces, zp, zp, shift, zeros], axis=1)
    diag = jnp.concatenate([zp, zp, zp, zp, shift, zeros], axis=1)
    return k_tab, jnp.stack([after, before, diag], axis=1)


def _diff_pre(x, norm_g, w_qkv, q_g, k_g, k_tab):
    b, s, _ = x.shape
    tm = KV_CHUNK
    hw = DIFF_HEADS * LANE
    out = jax.ShapeDtypeStruct((b, s, hw), BF16)
    out_k = jax.ShapeDtypeStruct((b, s, 2 * hw), BF16)
    vt_shape, vt_spec = _vt_shape_and_spec(b, DIFF_HEADS, DIFF_VT_ROWS, s)
    tile = lambda bi, i: (bi, i, 0)
    weights = (norm_g[None, :], w_qkv.astype(BF16), q_g.reshape(1, LANE), k_g.reshape(1, LANE), k_tab)
    return pl.pallas_call(
        _diff_pre_kernel,
        grid=(b, s // tm),
        in_specs=[pl.BlockSpec((None, tm, D_MODEL), tile)] + [_const_spec(w.shape) for w in weights],
        out_specs=[pl.BlockSpec((None, tm, hw), tile), pl.BlockSpec((None, tm, 2 * hw), tile), vt_spec],
        out_shape=[out, out_k, vt_shape],
        compiler_params=_params(("parallel", "parallel")),
        name="diff_pre",
    )(x, *weights)


def _flash_diff_kernel(slope_ref, q_ref, k_ref, vt_ref, dmat_ref, lam_ref, subg_ref, o_ref,
                       qcat_sc, *scratch, tq, n_kv, lambda_init):
    acc_sc = scratch[-1]
    lo = lax.broadcasted_iota(jnp.int32, (1, LANE), 1) < DIFF_HD
    q = q_ref[...]
    zero = jnp.zeros_like(q)
    qcat_sc[:tq] = jnp.where(lo, q, zero)
    qcat_sc[tq:] = jnp.where(lo, zero, q)
    neg_slope = slope_ref[pl.program_id(1)]
    q0 = pl.program_id(2) * tq

    def scores(j):
        delta = (j * KV_CHUNK - q0).astype(F32)
        bias = jnp.abs(dmat_ref[...] + delta) * neg_slope
        return _dot_nt(_key_chunk(k_ref, j), qcat_sc[...]) + jnp.concatenate([bias, bias], axis=1)

    _flash_core(scores, vt_ref, *scratch, n_kv)
    _diff_finalize(acc_sc, lam_ref, subg_ref, o_ref, tq, lambda_init)


def _diff_finalize(acc_sc, lam_ref, subg_ref, o_ref, tq, lambda_init):
    acc = acc_sc[...]
    _diff_store(acc[:LANE] / acc[LANE:LANE + 1], lam_ref, subg_ref, o_ref, tq, lambda_init)


def _diff_store(o, lam_ref, subg_ref, o_ref, tq, lambda_init):
    lp = lam_ref[...]
    lam = (jnp.exp(jnp.sum(lp[0:1] * lp[1:2], axis=-1, keepdims=True))
           - jnp.exp(jnp.sum(lp[2:3] * lp[3:4], axis=-1, keepdims=True)) + lambda_init)
    o = (o[:, :tq] - lam * o[:, tq:]).T
    o = _rms(o, subg_ref[...]) * (1.0 - lambda_init)
    o_ref[...] = o.astype(BF16)


def _flash_diff_static_kernel(slope_ref, q_ref, k_ref, vt_ref, dmat_ref, lam_ref, subg_ref, qtab_ref,
                              o_ref, qa_sc, p_buf, acc_sc, *, tq, n_kv, lambda_init):
    lo = lax.broadcasted_iota(jnp.int32, (1, LANE), 1) < DIFF_HD
    q = q_ref[...]
    zero = jnp.zeros_like(q)
    halves = (jnp.where(lo, q, zero), jnp.where(lo, zero, q))
    q0 = pl.program_id(2) * tq
    q_pos = _pos_lanes(q0, tq, 2 * DIFF_AUG_PARTS)
    tab = qtab_ref[...]
    for v, sign in enumerate((1.0, -1.0, 0.0)):
        aug = (tab[v:v + 1] + sign * q_pos).astype(BF16)
        for c in range(2):
            qa_sc[v, c * tq:(c + 1) * tq, :LANE] = halves[c]
            qa_sc[v, c * tq:(c + 1) * tq, LANE:] = aug
    jd = lax.shift_right_logical(q0, int(math.log2(KV_CHUNK)))
    neg_slope = slope_ref[pl.program_id(1)]

    def first_probs():
        delta = (jd * KV_CHUNK - q0).astype(F32)
        bias = jnp.abs(dmat_ref[...] + delta) * neg_slope
        s = _dot_nt(_key_chunk(k_ref, jd), qa_sc[2]) + jnp.concatenate([bias, bias], axis=1)
        return jnp.exp2(s).astype(BF16)

    def chunk_of(t):
        if isinstance(t, int) and t == 0:
            return jd
        return t - 1 + (t - 1 >= jd).astype(jnp.int32)

    def probs(t):
        j = chunk_of(t)
        before = (j < jd).astype(jnp.int32)
        return jnp.exp2(_dot_nt(_key_chunk(k_ref, j), qa_sc[before])).astype(BF16)

    _static_core(first_probs, probs, chunk_of, vt_ref, p_buf, acc_sc, n_kv)
    _diff_finalize(acc_sc, lam_ref, subg_ref, o_ref, tq, lambda_init)


def _flash_diff(q, k, vt, lam_p, sub_g, q_tab, lambda_init, tq, static_shift):
    b, s, hw = q.shape
    n_kv = s // KV_CHUNK
    slopes = 2.0 ** (-8.0 * jnp.arange(1, DIFF_HEADS + 1, dtype=F32) / DIFF_HEADS)
    neg_slopes = -slopes * LOG2E
    dmat = (jnp.arange(KV_CHUNK, dtype=F32)[:, None] - jnp.arange(tq, dtype=F32)[None, :])
    const = dict(pipeline_mode=pl.Buffered(1))
    in_specs = [
        pl.BlockSpec((None, tq, LANE), lambda bi, h, i, sl: (bi, i, h)),
        None,
        pl.BlockSpec((None, None, n_kv, DIFF_VT_ROWS, KV_CHUNK), lambda bi, h, i, sl: (bi, h, 0, 0, 0)),
        pl.BlockSpec((KV_CHUNK, tq), lambda bi, h, i, sl: (0, 0), **const),
        pl.BlockSpec((4, DIFF_HD), lambda bi, h, i, sl: (0, 0), **const),
        pl.BlockSpec((1, LANE), lambda bi, h, i, sl: (0, 0), **const),
    ]
    args = [neg_slopes, q, k, vt, dmat, lam_p, sub_g[None, :]]
    if static_shift:
        kern = functools.partial(_flash_diff_static_kernel, tq=tq, n_kv=n_kv, lambda_init=lambda_init)
        in_specs[1] = pl.BlockSpec((None, s, 2 * LANE), lambda bi, h, i, sl: (bi, 0, h))
        in_specs.append(pl.BlockSpec((None, 3, LANE), lambda bi, h, i, sl: (h, 0, 0)))
        args.append(q_tab)
        scratch = [pltpu.VMEM((3, 2 * tq, 2 * LANE), BF16)] + _static_scratch(DIFF_VT_ROWS, 2 * tq)
        name = "flash_diff_static"
    else:
        kern = functools.partial(_flash_diff_kernel, tq=tq, n_kv=n_kv, lambda_init=lambda_init)
        in_specs[1] = pl.BlockSpec((None, s, LANE), lambda bi, h, i, sl: (bi, 0, 2 * h))
        scratch = [pltpu.VMEM((2 * tq, LANE), BF16)] + _flash_scratch(DIFF_VT_ROWS, 2 * tq)
        name = "flash_diff"
    grid_spec = pltpu.PrefetchScalarGridSpec(
        num_scalar_prefetch=1,
        grid=(b, DIFF_HEADS, s // tq),
        in_specs=in_specs,
        out_specs=pl.BlockSpec((None, tq, LANE), lambda bi, h, i, sl: (bi, i, h)),
        scratch_shapes=scratch,
    )
    return pl.pallas_call(
        kern,
        grid_spec=grid_spec,
        out_shape=jax.ShapeDtypeStruct((b, s, hw), BF16),
        compiler_params=_params(("parallel", "parallel", "arbitrary")),
        name=name,
    )(*args)


def _post_xattn_kernel(x_ref, om_ref, wom_ref, ng_ref, wq_ref, qg_ref, k_ref, v_ref, wo_ref,
                       y_ref, o_sc):
    x1 = x_ref[...] + _dot(om_ref[...], wom_ref[...])
    h = _rms(x1, ng_ref[...]).astype(BF16)
    q = _dot(h, wq_ref[...])
    qg = qg_ref[...] * ((XA_HD ** -0.5) * LOG2E)
    for hd in range(XA_HEADS):
        sl = slice(hd * XA_HD, (hd + 1) * XA_HD)
        qh = _rms(q[:, sl], qg).astype(BF16)
        s = _dot_nt(qh, k_ref[:, sl])
        p = jnp.exp2(s - jnp.max(s, axis=-1, keepdims=True))
        l = jnp.sum(p, axis=-1, keepdims=True)
        o_sc[:, sl] = (_dot(p.astype(BF16), v_ref[:, sl]) / l).astype(BF16)
    y_ref[...] = x1 + _dot(o_sc[...], wo_ref[...])


def _post_xattn(x, o_mix, w_o_mix, norm_g, w_q, q_g, k_mem, v_mem, layer, w_o, tm):
    b, s, _ = x.shape
    tile = lambda bi, i: (bi, i, 0)
    mem_spec = pl.BlockSpec((None, None, N_MEM, D_MODEL), lambda bi, i: (layer, bi, 0, 0))
    sq = (D_MODEL, D_MODEL)
    return pl.pallas_call(
        _post_xattn_kernel,
        grid=(b, s // tm),
        in_specs=[
            pl.BlockSpec((None, tm, D_MODEL), tile),
            pl.BlockSpec((None, tm, o_mix.shape[-1]), tile),
            _const_spec(w_o_mix.shape),
            _const_spec((1, D_MODEL)),
            _const_spec(sq),
            _const_spec((1, XA_HD)),
            mem_spec,
            mem_spec,
            _const_spec(sq),
        ],
        out_specs=pl.BlockSpec((None, tm, D_MODEL), tile),
        out_shape=jax.ShapeDtypeStruct(x.shape, F32),
        scratch_shapes=[pltpu.VMEM((tm, D_MODEL), BF16)],
        compiler_params=_params(("parallel", "parallel")),
        name="post_xattn",
    )(x, o_mix, w_o_mix, norm_g[None, :], w_q.astype(BF16), q_g[None, :], k_mem, v_mem,
      w_o.astype(BF16))


def _ffn_kernel(x_ref, prev_ref, next_ref, ng_ref, wg_ref, wu_ref, cw_ref, cb_ref, wd_ref, y_ref,
                *, tm, n_chunks):
    i = pl.program_id(1)
    keep_prev = (i > 0).astype(F32)
    keep_next = (i < pl.num_programs(1) - 1).astype(F32)
    x = x_ref[...]
    xe = jnp.concatenate([prev_ref[...] * keep_prev, x, next_ref[...] * keep_next], axis=0)
    he = _rms(xe, ng_ref[...]).astype(BF16)
    hc = he[HALO:HALO + tm]
    cw = cw_ref[...]
    cb = cb_ref[...]
    y = x
    ck = D_FF // n_chunks
    for c in range(n_chunks):
        cs = slice(c * ck, (c + 1) * ck)
        ge = _dot(he, wg_ref[:, cs])
        u = _dot(hc, wu_ref[:, cs])
        g = (ge[HALO - 1:HALO - 1 + tm] * cw[0:1, cs] + ge[HALO:HALO + tm] * cw[1:2, cs]
             + ge[HALO + 1:HALO + 1 + tm] * cw[2:3, cs] + cb[:, cs])
        act = (g * jax.nn.sigmoid(g) * u).astype(BF16)
        y = y + _dot(act, wd_ref[cs, :])
    y_ref[...] = y


def _ffn(x, norm_g, w_gu, conv_w, conv_b, w_down, tm, n_chunks=2):
    b, s, _ = x.shape
    nh = tm // HALO
    last = s // HALO - 1
    w_g = w_gu[:, :D_FF].astype(BF16)
    w_u = w_gu[:, D_FF:].astype(BF16)
    weights = (norm_g[None, :], w_g, w_u, conv_w, conv_b[None, :], w_down.astype(BF16))
    kern = functools.partial(_ffn_kernel, tm=tm, n_chunks=n_chunks)
    return pl.pallas_call(
        kern,
        grid=(b, s // tm),
        in_specs=[
            pl.BlockSpec((None, tm, D_MODEL), lambda bi, i: (bi, i, 0)),
            pl.BlockSpec((None, HALO, D_MODEL), lambda bi, i: (bi, jnp.maximum(i * nh - 1, 0), 0)),
            pl.BlockSpec((None, HALO, D_MODEL), lambda bi, i: (bi, jnp.minimum((i + 1) * nh, last), 0)),
        ] + [_const_spec(w.shape) for w in weights],
        out_specs=pl.BlockSpec((None, tm, D_MODEL), lambda bi, i: (bi, i, 0)),
        out_shape=jax.ShapeDtypeStruct(x.shape, F32),
        compiler_params=_params(("parallel", "parallel")),
        name="ffn",
    )(x, x, x, *weights)


def _tiles(s):
    return dict(tm=min(512, s), tq_mla=1024, tq_diff=512)


def _mixer_weights(mla_p, diff_p):
    layers = []
    for i in range(DEPTH):
        j = i // 2
        if i % 2 == 0:
            weights, w_o_mix, bound = _mla_weights(*[p[j] for p in mla_p])
            layers.append(dict(weights=weights, w_o_mix=w_o_mix, bound=bound))
        else:
            norm_g, w_qkv, q_g, k_g, lam_p, sub_g, w_o = [p[j] for p in diff_p]
            bound = _score_bound(DIFF_HD, q_g, k_g)
            k_tab, q_tab = _diff_bias_tables(bound)
            layers.append(dict(pre=(norm_g, w_qkv, q_g, k_g, k_tab), lam_p=lam_p, sub_g=sub_g,
                               q_tab=q_tab, w_o_mix=w_o.astype(BF16), bound=bound))
    return layers


def _trunk(x, mem, mixers, xa_p, ffn_p):
    s = x.shape[1]
    assert s % KV_CHUNK == 0
    t = _tiles(s)
    xa_norm, xa_mem_norm, xa_w_q, xa_w_kv, xa_q_norm, xa_k_norm, xa_w_o = xa_p
    k_mem, v_mem = _mem_kv(mem, xa_mem_norm, xa_w_kv, xa_k_norm)
    cos, sin = _rope_tables(s)
    for i in range(DEPTH):
        mx = mixers[i]
        static_ok = mx["bound"] <= MAX_STATIC_BOUND
        if i % 2 == 0:
            q, k, vt = _mla_pre(x, mx["weights"], cos, sin)
            o_mix = lax.cond(static_ok,
                             lambda q, k, vt: _flash_mla_static(q, k, vt, t["tq_mla"]),
                             lambda q, k, vt: _flash_mla(q, k, vt, t["tq_mla"]), q, k, vt)
        else:
            lambda_init = 0.8 - 0.6 * math.exp(-0.3 * i)
            q, k, vt = _diff_pre(x, *mx["pre"])
            flash = lambda static: functools.partial(
                _flash_diff, lambda_init=lambda_init, tq=t["tq_diff"], static_shift=static)
            o_mix = lax.cond(static_ok, flash(True), flash(False),
                             q, k, vt, mx["lam_p"], mx["sub_g"], mx["q_tab"])
        x = _post_xattn(x, o_mix, mx["w_o_mix"], xa_norm[i], xa_w_q[i], xa_q_norm[i], k_mem, v_mem, i,
                        xa_w_o[i], t["tm"])
        x = _ffn(x, *[p[i] for p in ffn_p], t["tm"])
    return x


def kernel(x_prompt, x_sample, mem_prompt, mem_sample, mla_norm, mla_w_down, mla_q_lat_norm, mla_kv_lat_norm, mla_w_uq, mla_w_ukv, mla_q_norm, mla_k_norm, mla_w_o, diff_norm, diff_w_qkv, diff_q_norm, diff_k_norm, diff_lambda, diff_sub_norm, diff_w_o, xa_norm, xa_mem_norm, xa_w_q, xa_w_kv, xa_q_norm, xa_k_norm, xa_w_o, ffn_norm, ffn_w_gu, ffn_conv_w, ffn_conv_b, ffn_w_down):
    mla_p = (mla_norm, mla_w_down, mla_q_lat_norm, mla_kv_lat_norm, mla_w_uq, mla_w_ukv,
             mla_q_norm, mla_k_norm, mla_w_o)
    diff_p = (diff_norm, diff_w_qkv, diff_q_norm, diff_k_norm, diff_lambda, diff_sub_norm, diff_w_o)
    xa_p = (xa_norm, xa_mem_norm, xa_w_q, xa_w_kv, xa_q_norm, xa_k_norm, xa_w_o)
    ffn_p = (ffn_norm, ffn_w_gu, ffn_conv_w, ffn_conv_b, ffn_w_down)
    mixers = _mixer_weights(mla_p, diff_p)
    y_prompt = _trunk(x_prompt, mem_prompt, mixers, xa_p, ffn_p)
    y_sample = _trunk(x_sample, mem_sample, mixers, xa_p, ffn_p)
    return (y_prompt, y_sample)
```

```python
import functools
import math

import jax
import jax.numpy as jnp
from jax import lax
from jax.experimental import pallas as pl
from jax.experimental.pallas import tpu as pltpu

D_MODEL = 1024
DEPTH = 4
N_MEM = 256
EPS = 1e-6
MLA_HEADS = 8
MLA_Q_LORA = 384
MLA_KV_LORA = 256
MLA_NOPE = 64
MLA_ROPE = 32
MLA_QK = MLA_NOPE + MLA_ROPE
MLA_V = 64
ROPE_THETA = 10000.0
DIFF_HEADS = 8
DIFF_HD = 64
XA_HEADS = 4
XA_HD = D_MODEL // XA_HEADS
D_FF = 2816
CONV_W = 3

LANE = 128
HALO = 8
BF16_ROWS = 16
MXU_DIM = 256
KV_CHUNK = 512
DIFF_VT_ROWS = LANE + BF16_ROWS
MLA_VT_ROWS = MLA_V + BF16_ROWS
SHIFT_LANE = MLA_QK
SCORE_MARGIN = 1.02
MAX_STATIC_BOUND = 60.0
DIFF_AUG_PARTS = 3
POS_SPLIT = 64
LOG2E = math.log2(math.e)
NEG_BIG = -1e30
VMEM_LIMIT = 56 * 1024 * 1024

F32 = jnp.float32
BF16 = jnp.bfloat16


def _const_spec(shape):
    nd = len(shape)
    return pl.BlockSpec(shape, lambda *_: (0,) * nd, pipeline_mode=pl.Buffered(1))


def _params(sem):
    return pltpu.CompilerParams(dimension_semantics=sem, vmem_limit_bytes=VMEM_LIMIT)


def _rms(x, g):
    return x * lax.rsqrt(jnp.mean(x * x, axis=-1, keepdims=True) + EPS) * g


def _dot(a, b):
    return jnp.dot(a, b, preferred_element_type=F32)


def _dot_nt(a, b):
    return lax.dot_general(a, b, (((1,), (1,)), ((), ())), preferred_element_type=F32)


def _mem_kv_kernel(mem_ref, g_ref, w_ref, kg_ref, k_ref, v_ref):
    h = _rms(mem_ref[...], g_ref[...]).astype(BF16)
    kv = _dot(h, w_ref[...])
    kg = kg_ref[...]
    for hd in range(XA_HEADS):
        sl = slice(hd * XA_HD, (hd + 1) * XA_HD)
        k_ref[:, sl] = _rms(kv[:, sl], kg).astype(BF16)
    v_ref[...] = kv[:, D_MODEL:].astype(BF16)


def _mem_kv(mem, mem_g, w_kv, k_g):
    bm = mem.shape[0]
    out = jax.ShapeDtypeStruct((DEPTH, bm, N_MEM, D_MODEL), BF16)
    return pl.pallas_call(
        _mem_kv_kernel,
        grid=(DEPTH, bm),
        in_specs=[
            pl.BlockSpec((None, N_MEM, D_MODEL), lambda l, b: (b, 0, 0)),
            pl.BlockSpec((None, 1, D_MODEL), lambda l, b: (l, 0, 0)),
            pl.BlockSpec((None, D_MODEL, 2 * D_MODEL), lambda l, b: (l, 0, 0)),
            pl.BlockSpec((None, 1, XA_HD), lambda l, b: (l, 0, 0)),
        ],
        out_specs=[
            pl.BlockSpec((None, None, N_MEM, D_MODEL), lambda l, b: (l, b, 0, 0)),
            pl.BlockSpec((None, None, N_MEM, D_MODEL), lambda l, b: (l, b, 0, 0)),
        ],
        out_shape=[out, out],
        compiler_params=_params(("arbitrary", "arbitrary")),
        name="mem_kv",
    )(mem, mem_g[:, None, :], w_kv.astype(BF16), k_g[:, None, :])


def _mla_pre_kernel(x_ref, ng_ref, wd_ref, qlg_ref, kvlg_ref, wuq_ref, wukv_ref,
                    gq_ref, gk_ref, qshift_ref, cos_ref, sin_ref, q_ref, k_ref, vt_ref):
    h = _rms(x_ref[...], ng_ref[...]).astype(BF16)
    down = _dot(h, wd_ref[...])
    c_q = _rms(down[:, :MLA_Q_LORA], qlg_ref[...]).astype(BF16)
    kv0 = MLA_Q_LORA
    c_kv = _rms(down[:, kv0:kv0 + MLA_KV_LORA], kvlg_ref[...]).astype(BF16)
    kr = down[:, kv0 + MLA_KV_LORA:kv0 + MLA_KV_LORA + LANE]
    kr_rot = down[:, kv0 + MLA_KV_LORA + LANE:]
    qq = _dot(c_q, wuq_ref[...])
    kvp = _dot(c_kv, wukv_ref[...])
    cos = cos_ref[...]
    sin = sin_ref[...]
    gq = gq_ref[...]
    gk = gk_ref[...]
    hw = MLA_HEADS * LANE
    lane = lax.broadcasted_iota(jnp.int32, (1, LANE), 1)
    one_col = (lane == MLA_V).astype(F32)
    k_one = (lane == SHIFT_LANE).astype(F32)
    q_shift = qshift_ref[...]
    q_scale = (MLA_QK ** -0.5) * LOG2E
    for hd in range(MLA_HEADS):
        sl = slice(hd * LANE, (hd + 1) * LANE)
        qh = qq[:, sl]
        rq = lax.rsqrt(jnp.sum(qh * qh, axis=-1, keepdims=True) * (1.0 / MLA_QK) + EPS)
        q_rot = qq[:, hw + hd * LANE:hw + (hd + 1) * LANE]
        q_ref[:, sl] = ((rq * q_scale) * (qh * gq * cos + q_rot * sin) + q_shift).astype(BF16)
        kh = kvp[:, sl] + kr
        rk = lax.rsqrt(jnp.sum(kh * kh, axis=-1, keepdims=True) * (1.0 / MLA_QK) + EPS)
        k_ref[:, sl] = (rk * (kh * gk * cos + kr_rot * sin) + k_one).astype(BF16)
        v_t = (kvp[:, hw + hd * LANE:hw + (hd + 1) * LANE] + one_col).T
        vt_ref[hd] = v_t[:MLA_VT_ROWS].astype(BF16)


def _rot_half_cols(w):
    half = MLA_ROPE // 2
    return jnp.concatenate([-w[..., half:], w[..., :half]], axis=-1)


def _mla_weights(norm_g, w_down, q_lat_g, kv_lat_g, w_uq, w_ukv, q_g, k_g, w_o):
    nq, nkv = MLA_Q_LORA, MLA_KV_LORA
    w_dq, w_dkv, w_kr = w_down[:, :nq], w_down[:, nq:nq + nkv], w_down[:, nq + nkv:]
    pad_lo = jnp.zeros((D_MODEL, MLA_NOPE), F32)
    pad_hi = jnp.zeros((D_MODEL, LANE - MLA_QK), F32)
    kr_blk = jnp.concatenate([pad_lo, w_kr, pad_hi], axis=1)
    krot_blk = jnp.concatenate([pad_lo, _rot_half_cols(w_kr * k_g[MLA_NOPE:]), pad_hi], axis=1)
    wd = jnp.concatenate([w_dq, w_dkv, kr_blk, krot_blk], axis=1).astype(BF16)

    wq = w_uq.reshape(nq, MLA_HEADS, MLA_QK)
    zq = jnp.zeros((nq, MLA_HEADS, LANE - MLA_QK), F32)
    wq_main = jnp.concatenate([wq, zq], axis=-1).reshape(nq, MLA_HEADS * LANE)
    wq_rot = jnp.concatenate(
        [jnp.zeros((nq, MLA_HEADS, MLA_NOPE), F32),
         _rot_half_cols(wq[..., MLA_NOPE:] * q_g[MLA_NOPE:]), zq], axis=-1
    ).reshape(nq, MLA_HEADS * LANE)
    wuq = jnp.concatenate([wq_main, wq_rot], axis=1).astype(BF16)

    wkv = w_ukv.reshape(nkv, MLA_HEADS, MLA_NOPE + MLA_V)
    zk = jnp.zeros((nkv, MLA_HEADS, LANE - MLA_NOPE), F32)
    zv = jnp.zeros((nkv, MLA_HEADS, LANE - MLA_V), F32)
    wk = jnp.concatenate([wkv[..., :MLA_NOPE], zk], axis=-1).reshape(nkv, MLA_HEADS * LANE)
    wv = jnp.concatenate([wkv[..., MLA_NOPE:], zv], axis=-1).reshape(nkv, MLA_HEADS * LANE)
    wukv = jnp.concatenate([wk, wv], axis=1).astype(BF16)

    zg = jnp.zeros((LANE - MLA_QK,), F32)
    gq = jnp.concatenate([q_g, zg])[None, :]
    gk = jnp.concatenate([k_g, zg])[None, :]

    wo = w_o.reshape(MLA_HEADS, MLA_V, D_MODEL)
    wo = jnp.concatenate([wo, jnp.zeros((MLA_HEADS, LANE - MLA_V, D_MODEL), F32)], axis=1)
    wo = wo.reshape(MLA_HEADS * LANE, D_MODEL).astype(BF16)
    bound = _score_bound(MLA_QK, q_g, k_g)
    q_shift = -bound * (jnp.arange(LANE) == SHIFT_LANE).astype(F32)[None, :]
    weights = (norm_g[None, :], wd, q_lat_g[None, :], kv_lat_g[None, :], wuq, wukv, gq, gk, q_shift)
    return weights, wo, bound


def _score_bound(dim, q_g, k_g):
    return SCORE_MARGIN * math.sqrt(dim) * LOG2E * jnp.max(jnp.abs(q_g)) * jnp.max(jnp.abs(k_g))


def _rope_tables(seq):
    inv = ROPE_THETA ** (-jnp.arange(0, MLA_ROPE, 2, dtype=F32) / MLA_ROPE)
    ang = jnp.arange(seq, dtype=F32)[:, None] * inv[None, :]
    ang = jnp.concatenate([ang, ang], axis=-1)
    ones = jnp.ones((seq, MLA_NOPE), F32)
    zlo = jnp.zeros((seq, MLA_NOPE), F32)
    zhi = jnp.zeros((seq, LANE - MLA_QK), F32)
    cos = jnp.concatenate([ones, jnp.cos(ang), zhi], axis=1)
    sin = jnp.concatenate([zlo, jnp.sin(ang), zhi], axis=1)
    return cos, sin


def _vt_shape_and_spec(b, heads, rows, s):
    shape = jax.ShapeDtypeStruct((b, heads, s // KV_CHUNK, rows, KV_CHUNK), BF16)
    spec = pl.BlockSpec((None, heads, None, rows, KV_CHUNK), lambda bi, i: (bi, 0, i, 0, 0))
    return shape, spec


def _mla_pre(x, weights, cos, sin):
    b, s, _ = x.shape
    tm = KV_CHUNK
    hw = MLA_HEADS * LANE
    out = jax.ShapeDtypeStruct((b, s, hw), BF16)
    vt_shape, vt_spec = _vt_shape_and_spec(b, MLA_HEADS, MLA_VT_ROWS, s)
    tile = lambda bi, i: (bi, i, 0)
    w_specs = [_const_spec(w.shape) for w in weights]
    return pl.pallas_call(
        _mla_pre_kernel,
        grid=(b, s // tm),
        in_specs=[pl.BlockSpec((None, tm, D_MODEL), tile)] + w_specs + [
            pl.BlockSpec((tm, LANE), lambda bi, i: (i, 0)),
            pl.BlockSpec((tm, LANE), lambda bi, i: (i, 0)),
        ],
        out_specs=[pl.BlockSpec((None, tm, hw), tile)] * 2 + [vt_spec],
        out_shape=[out, out, vt_shape],
        compiler_params=_params(("parallel", "parallel")),
        name="mla_pre",
    )(x, *weights, cos, sin)


def _flash_core(scores, vt_ref, s_buf, mx_buf, p_buf, al_buf, m_sc, acc_sc, n_kv):
    assert n_kv >= 2 and n_kv % 2 == 0

    def issue_scores(j, slot):
        s = scores(j)
        s_buf[slot] = s
        mx_buf[slot] = jnp.max(s, axis=0, keepdims=True)

    def softmax(slot):
        m_old = m_sc[...]
        m_new = jnp.maximum(m_old, mx_buf[slot])
        p_buf[slot] = jnp.exp2(s_buf[slot] - m_new).astype(BF16)
        al_buf[slot] = jnp.exp2(m_old - m_new)
        m_sc[...] = m_new

    def values(j, slot):
        acc_sc[...] = al_buf[slot] * acc_sc[...] + _dot(vt_ref[j], p_buf[slot])

    m_sc[...] = jnp.full(m_sc.shape, NEG_BIG, F32)
    acc_sc[...] = jnp.zeros(acc_sc.shape, F32)
    issue_scores(0, 0)
    issue_scores(1, 1)
    softmax(0)

    def pair(i, carry):
        j = 2 * i + 1
        issue_scores(j + 1, 0)
        softmax(1)
        values(j - 1, 0)
        issue_scores(j + 2, 1)
        softmax(0)
        values(j, 1)
        return carry

    lax.fori_loop(0, (n_kv - 2) // 2, pair, 0)
    softmax(1)
    values(n_kv - 2, 0)
    values(n_kv - 1, 1)


def _flash_scratch(rows, nq):
    return [
        pltpu.VMEM((2, KV_CHUNK, nq), F32),
        pltpu.VMEM((2, 1, nq), F32),
        pltpu.VMEM((2, KV_CHUNK, nq), BF16),
        pltpu.VMEM((2, 1, nq), F32),
        pltpu.VMEM((1, nq), F32),
        pltpu.VMEM((rows, nq), F32),
    ]


def _key_chunk(k_ref, j):
    return k_ref[pl.ds(pl.multiple_of(j * KV_CHUNK, KV_CHUNK), KV_CHUNK), :]


def _mla_store(o_t, o_ref):
    o_ref[...] = jnp.concatenate([o_t, jnp.zeros((LANE - MLA_V, o_t.shape[1]), F32)], axis=0).T.astype(BF16)


def _mla_finalize(acc_sc, o_ref):
    acc = acc_sc[...]
    _mla_store(acc[:MLA_V] / acc[MLA_V:MLA_V + 1], o_ref)


def _flash_mla_kernel(q_ref, k_ref, vt_ref, o_ref, *scratch, n_kv):
    acc_sc = scratch[-1]
    _flash_core(lambda j: _dot_nt(_key_chunk(k_ref, j), q_ref[...]), vt_ref, *scratch, n_kv)
    _mla_finalize(acc_sc, o_ref)


def _flash_mla(q, k, vt, tq):
    b, s, hw = q.shape
    n_kv = s // KV_CHUNK
    kern = functools.partial(_flash_mla_kernel, n_kv=n_kv)
    return pl.pallas_call(
        kern,
        grid=(b, MLA_HEADS, s // tq),
        in_specs=[
            pl.BlockSpec((None, tq, LANE), lambda bi, h, i: (bi, i, h)),
            pl.BlockSpec((None, s, LANE), lambda bi, h, i: (bi, 0, h)),
            pl.BlockSpec((None, None, n_kv, MLA_VT_ROWS, KV_CHUNK), lambda bi, h, i: (bi, h, 0, 0, 0)),
        ],
        out_specs=pl.BlockSpec((None, tq, LANE), lambda bi, h, i: (bi, i, h)),
        out_shape=jax.ShapeDtypeStruct((b, s, hw), BF16),
        scratch_shapes=_flash_scratch(MLA_VT_ROWS, tq),
        compiler_params=_params(("parallel", "parallel", "arbitrary")),
        name="flash_mla",
    )(q, k, vt)


def _static_core(first_probs, probs, chunk_of, vt_ref, p_buf, acc_sc, n_kv):
    assert n_kv >= 2 and n_kv % 2 == 0

    def values(t, slot, first=False):
        pv = _dot(vt_ref[chunk_of(t)], p_buf[slot])
        acc_sc[...] = pv if first else acc_sc[...] + pv

    p_buf[0] = first_probs()
    p_buf[1] = probs(1)
    values(0, 0, first=True)

    def pair(i, carry):
        t = 2 * i + 1
        p_buf[0] = probs(t + 1)
        values(t, 1)
        p_buf[1] = probs(t + 2)
        values(t + 1, 0)
        return carry

    lax.fori_loop(0, (n_kv - 2) // 2, pair, 0, unroll=True)
    values(n_kv - 1, 1)


def _static_scratch(rows, nq):
    return [pltpu.VMEM((2, KV_CHUNK, nq), BF16), pltpu.VMEM((rows, nq), F32)]


def _flash_mla_static_kernel(q_ref, k_ref, vt_ref, o_ref, p_buf, acc_sc, *, n_kv):
    probs = lambda t: jnp.exp2(_dot_nt(_key_chunk(k_ref, t), q_ref[...])).astype(BF16)
    _static_core(lambda: probs(0), probs, lambda t: t, vt_ref, p_buf, acc_sc, n_kv)
    _mla_finalize(acc_sc, o_ref)


def _flash_mla_static(q, k, vt, tq):
    b, s, hw = q.shape
    n_kv = s // KV_CHUNK
    kern = functools.partial(_flash_mla_static_kernel, n_kv=n_kv)
    return pl.pallas_call(
        kern,
        grid=(b, MLA_HEADS, s // tq),
        in_specs=[
            pl.BlockSpec((None, tq, LANE), lambda bi, h, i: (bi, i, h)),
            pl.BlockSpec((None, s, LANE), lambda bi, h, i: (bi, 0, h)),
            pl.BlockSpec((None, None, n_kv, MLA_VT_ROWS, KV_CHUNK), lambda bi, h, i: (bi, h, 0, 0, 0)),
        ],
        out_specs=pl.BlockSpec((None, tq, LANE), lambda bi, h, i: (bi, i, h)),
        out_shape=jax.ShapeDtypeStruct((b, s, hw), BF16),
        scratch_shapes=_static_scratch(MLA_VT_ROWS, tq),
        compiler_params=_params(("parallel", "parallel", "arbitrary")),
        name="flash_mla_static",
    )(q, k, vt)


def _pos_lanes(pos0, rows, first_lane):
    pos = pos0 + lax.broadcasted_iota(jnp.int32, (rows, LANE), 0)
    lane = lax.broadcasted_iota(jnp.int32, (rows, LANE), 1) - first_lane
    hi = lax.shift_right_logical(pos, int(math.log2(POS_SPLIT))).astype(F32)
    lo = (pos & (POS_SPLIT - 1)).astype(F32)
    p = DIFF_AUG_PARTS
    in_hi = (lane >= 0) & (lane < p)
    in_lo = (lane >= p) & (lane < 2 * p)
    return jnp.where(in_hi, hi, jnp.where(in_lo, lo, 0.0))


def _diff_pre_kernel(x_ref, ng_ref, w_ref, gq_ref, gk_ref, kaug_ref, q_ref, k_ref, vt_ref):
    h = _rms(x_ref[...], ng_ref[...]).astype(BF16)
    qkv = _dot(h, w_ref[...])
    hw = DIFF_HEADS * LANE
    lo = lax.broadcasted_iota(jnp.int32, (1, LANE), 1) < DIFF_HD
    q_scale = (DIFF_HD ** -0.5) * LOG2E
    tm = qkv.shape[0]
    k_pos = _pos_lanes(pl.program_id(1) * tm, tm, 0)
    ones_rows = (lax.broadcasted_iota(jnp.int32, (DIFF_VT_ROWS - LANE, tm), 0) == 0).astype(F32)

    def half_norm(t, g):
        t2 = t * t
        ss_lo = jnp.sum(jnp.where(lo, t2, 0.0), axis=-1, keepdims=True)
        ss_hi = jnp.sum(jnp.where(lo, 0.0, t2), axis=-1, keepdims=True)
        r = lax.rsqrt(jnp.where(lo, ss_lo, ss_hi) * (1.0 / DIFF_HD) + EPS)
        return t * r * g

    for hd in range(DIFF_HEADS):
        sl = slice(hd * LANE, (hd + 1) * LANE)
        q_ref[:, sl] = (half_norm(qkv[:, sl], gq_ref[...]) * q_scale).astype(BF16)
        ks = slice(2 * hd * LANE, (2 * hd + 1) * LANE)
        k_ref[:, ks] = half_norm(qkv[:, hw + hd * LANE:hw + (hd + 1) * LANE], gk_ref[...]).astype(BF16)
        k_ref[:, (2 * hd + 1) * LANE:(2 * hd + 2) * LANE] = (k_pos + kaug_ref[hd]).astype(BF16)
        v_t = qkv[:, 2 * hw + hd * LANE:2 * hw + (hd + 1) * LANE].T
        vt_ref[hd] = jnp.concatenate([v_t, ones_rows], axis=0).astype(BF16)


def _slope_pieces():
    slopes = 2.0 ** (-8.0 * jnp.arange(1, DIFF_HEADS + 1, dtype=F32) / DIFF_HEADS) * LOG2E
    pieces, rest = [], slopes
    for _ in range(DIFF_AUG_PARTS):
        piece = rest.astype(BF16).astype(F32)
        pieces.append(piece)
        rest = rest - piece
    return jnp.stack(pieces, axis=1)


def _diff_bias_tables(bound):
    p = DIFF_AUG_PARTS
    pieces = _slope_pieces()
    zeros = jnp.zeros((DIFF_HEADS, LANE - 4 * p - 1), F32)
    one = jnp.ones((DIFF_HEADS, 1), F32)
    zp = jnp.zeros((DIFF_HEADS, p), F32)
    k_tab = jnp.concatenate([zp, zp, POS_SPLIT * pieces, pieces, one, zeros], axis=1)[:, None, :]
    shift = -bound * one
    after = jnp.concatenate([-POS_SPLIT * pieces, -pieces, zp, zp, shift, zeros], axis=1)
    before = jnp.concatenate([POS_SPLIT * pieces, pieces, zp, zp, shift, zeros], axis=1)
    diag = jnp.concatenate([zp, zp, zp, zp, shift, zeros], axis=1)
    return k_tab, jnp.stack([after, before, diag], axis=1)


def _diff_pre(x, norm_g, w_qkv, q_g, k_g, k_tab):
    b, s, _ = x.shape
    tm = KV_CHUNK
    hw = DIFF_HEADS * LANE
    out = jax.ShapeDtypeStruct((b, s, hw), BF16)
    out_k = jax.ShapeDtypeStruct((b, s, 2 * hw), BF16)
    vt_shape, vt_spec = _vt_shape_and_spec(b, DIFF_HEADS, DIFF_VT_ROWS, s)
    tile = lambda bi, i: (bi, i, 0)
    weights = (norm_g[None, :], w_qkv.astype(BF16), q_g.reshape(1, LANE), k_g.reshape(1, LANE), k_tab)
    return pl.pallas_call(
        _diff_pre_kernel,
        grid=(b, s // tm),
        in_specs=[pl.BlockSpec((None, tm, D_MODEL), tile)] + [_const_spec(w.shape) for w in weights],
        out_specs=[pl.BlockSpec((None, tm, hw), tile), pl.BlockSpec((None, tm, 2 * hw), tile), vt_spec],
        out_shape=[out, out_k, vt_shape],
        compiler_params=_params(("parallel", "parallel")),
        name="diff_pre",
    )(x, *weights)


def _flash_diff_kernel(slope_ref, q_ref, k_ref, vt_ref, dmat_ref, lam_ref, subg_ref, o_ref,
                       qcat_sc, *scratch, tq, n_kv, lambda_init):
    acc_sc = scratch[-1]
    lo = lax.broadcasted_iota(jnp.int32, (1, LANE), 1) < DIFF_HD
    q = q_ref[...]
    zero = jnp.zeros_like(q)
    qcat_sc[:tq] = jnp.where(lo, q, zero)
    qcat_sc[tq:] = jnp.where(lo, zero, q)
    neg_slope = slope_ref[pl.program_id(1)]
    q0 = pl.program_id(2) * tq

    def scores(j):
        delta = (j * KV_CHUNK - q0).astype(F32)
        bias = jnp.abs(dmat_ref[...] + delta) * neg_slope
        return _dot_nt(_key_chunk(k_ref, j), qcat_sc[...]) + jnp.concatenate([bias, bias], axis=1)

    _flash_core(scores, vt_ref, *scratch, n_kv)
    _diff_finalize(acc_sc, lam_ref, subg_ref, o_ref, tq, lambda_init)


def _diff_finalize(acc_sc, lam_ref, subg_ref, o_ref, tq, lambda_init):
    acc = acc_sc[...]
    _diff_store(acc[:LANE] / acc[LANE:LANE + 1], lam_ref, subg_ref, o_ref, tq, lambda_init)


def _diff_store(o, lam_ref, subg_ref, o_ref, tq, lambda_init):
    lp = lam_ref[...]
    lam = (jnp.exp(jnp.sum(lp[0:1] * lp[1:2], axis=-1, keepdims=True))
           - jnp.exp(jnp.sum(lp[2:3] * lp[3:4], axis=-1, keepdims=True)) + lambda_init)
    o = (o[:, :tq] - lam * o[:, tq:]).T
    o = _rms(o, subg_ref[...]) * (1.0 - lambda_init)
    o_ref[...] = o.astype(BF16)


def _flash_diff_static_kernel(slope_ref, q_ref, k_ref, vt_ref, dmat_ref, lam_ref, subg_ref, qtab_ref,
                              o_ref, qa_sc, p_buf, acc_sc, *, tq, n_kv, lambda_init):
    lo = lax.broadcasted_iota(jnp.int32, (1, LANE), 1) < DIFF_HD
    q = q_ref[...]
    zero = jnp.zeros_like(q)
    halves = (jnp.where(lo, q, zero), jnp.where(lo, zero, q))
    n_d = tq // KV_CHUNK
    q0 = pl.program_id(2) * tq
    q_pos = _pos_lanes(q0, tq, 2 * DIFF_AUG_PARTS)
    tab = qtab_ref[...]
    aug = [(tab[v:v + 1] + sign * q_pos).astype(BF16) for v, sign in enumerate((1.0, -1.0, 0.0))]
    for v in range(2 + n_d):
        for e in range(n_d):
            kind = v if v < 2 else (2 if e == v - 2 else (1 if e > v - 2 else 0))
            blk = slice(e * KV_CHUNK, (e + 1) * KV_CHUNK)
            for c in range(2):
                rows = slice(c * tq + e * KV_CHUNK, c * tq + (e + 1) * KV_CHUNK)
                qa_sc[v, rows, :LANE] = halves[c][blk]
                qa_sc[v, rows, LANE:] = aug[kind][blk]
    jd = lax.shift_right_logical(q0, int(math.log2(KV_CHUNK)))
    neg_slope = slope_ref[pl.program_id(1)]

    def diag_probs(d):
        s = _dot_nt(_key_chunk(k_ref, jd + d), qa_sc[2 + d])
        bias = jnp.abs(dmat_ref[:, :KV_CHUNK]) * neg_slope
        cols = []
        for c in range(2):
            for e in range(n_d):
                blk = s[:, c * tq + e * KV_CHUNK:c * tq + (e + 1) * KV_CHUNK]
                cols.append(blk + bias if e == d else blk)
        return jnp.exp2(jnp.concatenate(cols, axis=1)).astype(BF16)

    def chunk_of(t):
        r = t - n_d
        return jnp.where(t < n_d, jd + t, r + n_d * (r >= jd).astype(jnp.int32))

    def probs(t):
        if isinstance(t, int) and t < n_d:
            return diag_probs(t)
        j = chunk_of(t)
        before = (j < jd).astype(jnp.int32)
        return jnp.exp2(_dot_nt(_key_chunk(k_ref, j), qa_sc[before])).astype(BF16)

    _static_core(lambda: diag_probs(0), probs, chunk_of, vt_ref, p_buf, acc_sc, n_kv)
    _diff_finalize(acc_sc, lam_ref, subg_ref, o_ref, tq, lambda_init)


def _flash_diff(q, k, vt, lam_p, sub_g, q_tab, lambda_init, tq, static_shift):
    b, s, hw = q.shape
    n_kv = s // KV_CHUNK
    slopes = 2.0 ** (-8.0 * jnp.arange(1, DIFF_HEADS + 1, dtype=F32) / DIFF_HEADS)
    neg_slopes = -slopes * LOG2E
    dmat = (jnp.arange(KV_CHUNK, dtype=F32)[:, None] - jnp.arange(tq, dtype=F32)[None, :])
    const = dict(pipeline_mode=pl.Buffered(1))
    in_specs = [
        pl.BlockSpec((None, tq, LANE), lambda bi, h, i, sl: (bi, i, h)),
        None,
        pl.BlockSpec((None, None, n_kv, DIFF_VT_ROWS, KV_CHUNK), lambda bi, h, i, sl: (bi, h, 0, 0, 0)),
        pl.BlockSpec((KV_CHUNK, tq), lambda bi, h, i, sl: (0, 0), **const),
        pl.BlockSpec((4, DIFF_HD), lambda bi, h, i, sl: (0, 0), **const),
        pl.BlockSpec((1, LANE), lambda bi, h, i, sl: (0, 0), **const),
    ]
    args = [neg_slopes, q, k, vt, dmat, lam_p, sub_g[None, :]]
    if static_shift:
        kern = functools.partial(_flash_diff_static_kernel, tq=tq, n_kv=n_kv, lambda_init=lambda_init)
        in_specs[1] = pl.BlockSpec((None, s, 2 * LANE), lambda bi, h, i, sl: (bi, 0, h))
        in_specs.append(pl.BlockSpec((None, 3, LANE), lambda bi, h, i, sl: (h, 0, 0)))
        args.append(q_tab)
        assert tq % KV_CHUNK == 0
        n_variants = 2 + tq // KV_CHUNK
        scratch = ([pltpu.VMEM((n_variants, 2 * tq, 2 * LANE), BF16)]
                   + _static_scratch(DIFF_VT_ROWS, 2 * tq))
        name = "flash_diff_static"
    else:
        kern = functools.partial(_flash_diff_kernel, tq=tq, n_kv=n_kv, lambda_init=lambda_init)
        in_specs[1] = pl.BlockSpec((None, s, LANE), lambda bi, h, i, sl: (bi, 0, 2 * h))
        scratch = [pltpu.VMEM((2 * tq, LANE), BF16)] + _flash_scratch(DIFF_VT_ROWS, 2 * tq)
        name = "flash_diff"
    grid_spec = pltpu.PrefetchScalarGridSpec(
        num_scalar_prefetch=1,
        grid=(b, DIFF_HEADS, s // tq),
        in_specs=in_specs,
        out_specs=pl.BlockSpec((None, tq, LANE), lambda bi, h, i, sl: (bi, i, h)),
        scratch_shapes=scratch,
    )
    return pl.pallas_call(
        kern,
        grid_spec=grid_spec,
        out_shape=jax.ShapeDtypeStruct((b, s, hw), BF16),
        compiler_params=_params(("parallel", "parallel", "arbitrary")),
        name=name,
    )(*args)


def _post_xattn_kernel(x_ref, om_ref, wom_ref, ng_ref, wq_ref, qg_ref, k_ref, v_ref, wo_ref,
                       y_ref, o_sc):
    x1 = x_ref[...] + _dot(om_ref[...], wom_ref[...])
    h = _rms(x1, ng_ref[...]).astype(BF16)
    q = _dot(h, wq_ref[...])
    qg = qg_ref[...] * ((XA_HD ** -0.5) * LOG2E)
    for hd in range(XA_HEADS):
        sl = slice(hd * XA_HD, (hd + 1) * XA_HD)
        qh = _rms(q[:, sl], qg).astype(BF16)
        s = _dot_nt(qh, k_ref[:, sl])
        p = jnp.exp2(s - jnp.max(s, axis=-1, keepdims=True))
        l = jnp.sum(p, axis=-1, keepdims=True)
        o_sc[:, sl] = (_dot(p.astype(BF16), v_ref[:, sl]) / l).astype(BF16)
    y_ref[...] = x1 + _dot(o_sc[...], wo_ref[...])


def _post_xattn(x, o_mix, w_o_mix, norm_g, w_q, q_g, k_mem, v_mem, layer, w_o, tm):
    b, s, _ = x.shape
    tile = lambda bi, i: (bi, i, 0)
    mem_spec = pl.BlockSpec((None, None, N_MEM, D_MODEL), lambda bi, i: (layer, bi, 0, 0))
    sq = (D_MODEL, D_MODEL)
    return pl.pallas_call(
        _post_xattn_kernel,
        grid=(b, s // tm),
        in_specs=[
            pl.BlockSpec((None, tm, D_MODEL), tile),
            pl.BlockSpec((None, tm, o_mix.shape[-1]), tile),
            _const_spec(w_o_mix.shape),
            _const_spec((1, D_MODEL)),
            _const_spec(sq),
            _const_spec((1, XA_HD)),
            mem_spec,
            mem_spec,
            _const_spec(sq),
        ],
        out_specs=pl.BlockSpec((None, tm, D_MODEL), tile),
        out_shape=jax.ShapeDtypeStruct(x.shape, F32),
        scratch_shapes=[pltpu.VMEM((tm, D_MODEL), BF16)],
        compiler_params=_params(("parallel", "parallel")),
        name="post_xattn",
    )(x, o_mix, w_o_mix, norm_g[None, :], w_q.astype(BF16), q_g[None, :], k_mem, v_mem,
      w_o.astype(BF16))


def _ffn_kernel(x_ref, prev_ref, next_ref, ng_ref, wg_ref, wu_ref, cw_ref, cb_ref, wd_ref, y_ref,
                *, tm, n_chunks):
    i = pl.program_id(1)
    keep_prev = (i > 0).astype(F32)
    keep_next = (i < pl.num_programs(1) - 1).astype(F32)
    x = x_ref[...]
    xe = jnp.concatenate([prev_ref[...] * keep_prev, x, next_ref[...] * keep_next], axis=0)
    he = _rms(xe, ng_ref[...]).astype(BF16)
    hc = he[HALO:HALO + tm]
    cw = cw_ref[...]
    cb = cb_ref[...]
    y = x
    n_tiles = D_FF // MXU_DIM
    bounds = [MXU_DIM * ((n_tiles * c + n_chunks - 1) // n_chunks) for c in range(n_chunks)] + [D_FF]
    for c in range(n_chunks):
        cs = slice(bounds[c], bounds[c + 1])
        ge = _dot(he, wg_ref[:, cs])
        u = _dot(hc, wu_ref[:, cs])
        g = (ge[HALO - 1:HALO - 1 + tm] * cw[0:1, cs] + ge[HALO:HALO + tm] * cw[1:2, cs]
             + ge[HALO + 1:HALO + 1 + tm] * cw[2:3, cs] + cb[:, cs])
        act = (g * jax.nn.sigmoid(g) * u).astype(BF16)
        y = y + _dot(act, wd_ref[cs, :])
    y_ref[...] = y


def _ffn(x, norm_g, w_gu, conv_w, conv_b, w_down, tm, n_chunks=1):
    b, s, _ = x.shape
    nh = tm // HALO
    last = s // HALO - 1
    w_g = w_gu[:, :D_FF].astype(BF16)
    w_u = w_gu[:, D_FF:].astype(BF16)
    weights = (norm_g[None, :], w_g, w_u, conv_w, conv_b[None, :], w_down.astype(BF16))
    kern = functools.partial(_ffn_kernel, tm=tm, n_chunks=n_chunks)
    return pl.pallas_call(
        kern,
        grid=(b, s // tm),
        in_specs=[
            pl.BlockSpec((None, tm, D_MODEL), lambda bi, i: (bi, i, 0)),
            pl.BlockSpec((None, HALO, D_MODEL), lambda bi, i: (bi, jnp.maximum(i * nh - 1, 0), 0)),
            pl.BlockSpec((None, HALO, D_MODEL), lambda bi, i: (bi, jnp.minimum((i + 1) * nh, last), 0)),
        ] + [_const_spec(w.shape) for w in weights],
        out_specs=pl.BlockSpec((None, tm, D_MODEL), lambda bi, i: (bi, i, 0)),
        out_shape=jax.ShapeDtypeStruct(x.shape, F32),
        compiler_params=_params(("parallel", "parallel")),
        name="ffn",
    )(x, x, x, *weights)


def _tiles(s):
    return dict(tm=min(512, s), tq_mla=min(2048, s), tq_diff=min(1024, s))


def _mixer_weights(mla_p, diff_p):
    layers = []
    for i in range(DEPTH):
        j = i // 2
        if i % 2 == 0:
            weights, w_o_mix, bound = _mla_weights(*[p[j] for p in mla_p])
            layers.append(dict(weights=weights, w_o_mix=w_o_mix, bound=bound))
        else:
            norm_g, w_qkv, q_g, k_g, lam_p, sub_g, w_o = [p[j] for p in diff_p]
            bound = _score_bound(DIFF_HD, q_g, k_g)
            k_tab, q_tab = _diff_bias_tables(bound)
            layers.append(dict(pre=(norm_g, w_qkv, q_g, k_g, k_tab), lam_p=lam_p, sub_g=sub_g,
                               q_tab=q_tab, w_o_mix=w_o.astype(BF16), bound=bound))
    return layers


def _trunk(x, mem, mixers, xa_p, ffn_p):
    s = x.shape[1]
    assert s % KV_CHUNK == 0
    t = _tiles(s)
    xa_norm, xa_mem_norm, xa_w_q, xa_w_kv, xa_q_norm, xa_k_norm, xa_w_o = xa_p
    k_mem, v_mem = _mem_kv(mem, xa_mem_norm, xa_w_kv, xa_k_norm)
    cos, sin = _rope_tables(s)
    for i in range(DEPTH):
        mx = mixers[i]
        static_ok = mx["bound"] <= MAX_STATIC_BOUND
        if i % 2 == 0:
            q, k, vt = _mla_pre(x, mx["weights"], cos, sin)
            o_mix = lax.cond(static_ok,
                             lambda q, k, vt: _flash_mla_static(q, k, vt, t["tq_mla"]),
                             lambda q, k, vt: _flash_mla(q, k, vt, t["tq_mla"]), q, k, vt)
        else:
            lambda_init = 0.8 - 0.6 * math.exp(-0.3 * i)
            q, k, vt = _diff_pre(x, *mx["pre"])
            flash = lambda static: functools.partial(
                _flash_diff, lambda_init=lambda_init, tq=t["tq_diff"], static_shift=static)
            o_mix = lax.cond(static_ok, flash(True), flash(False),
                             q, k, vt, mx["lam_p"], mx["sub_g"], mx["q_tab"])
        x = _post_xattn(x, o_mix, mx["w_o_mix"], xa_norm[i], xa_w_q[i], xa_q_norm[i], k_mem, v_mem, i,
                        xa_w_o[i], t["tm"])
        x = _ffn(x, *[p[i] for p in ffn_p], t["tm"])
    return x


def kernel(x_prompt, x_sample, mem_prompt, mem_sample, mla_norm, mla_w_down, mla_q_lat_norm, mla_kv_lat_norm, mla_w_uq, mla_w_ukv, mla_q_norm, mla_k_norm, mla_w_o, diff_norm, diff_w_qkv, diff_q_norm, diff_k_norm, diff_lambda, diff_sub_norm, diff_w_o, xa_norm, xa_mem_norm, xa_w_q, xa_w_kv, xa_q_norm, xa_k_norm, xa_w_o, ffn_norm, ffn_w_gu, ffn_conv_w, ffn_conv_b, ffn_w_down):
    mla_p = (mla_norm, mla_w_down, mla_q_lat_norm, mla_kv_lat_norm, mla_w_uq, mla_w_ukv,
             mla_q_norm, mla_k_norm, mla_w_o)
    diff_p = (diff_norm, diff_w_qkv, diff_q_norm, diff_k_norm, diff_lambda, diff_sub_norm, diff_w_o)
    xa_p = (xa_norm, xa_mem_norm, xa_w_q, xa_w_kv, xa_q_norm, xa_k_norm, xa_w_o)
    ffn_p = (ffn_norm, ffn_w_gu, ffn_conv_w, ffn_conv_b, ffn_w_down)
    mixers = _mixer_weights(mla_p, diff_p)
    y_prompt = _trunk(x_prompt, mem_prompt, mixers, xa_p, ffn_p)
    y_sample = _trunk(x_sample, mem_sample, mixers, xa_p, ffn_p)
    return (y_prompt, y_sample)
```

```python
import functools
import math

import jax
import jax.numpy as jnp
from jax import lax
from jax.experimental import pallas as pl
from jax.experimental.pallas import tpu as pltpu

D_MODEL = 1024
DEPTH = 4
N_MEM = 256
EPS = 1e-6
MLA_HEADS = 8
MLA_Q_LORA = 384
MLA_KV_LORA = 256
MLA_NOPE = 64
MLA_ROPE = 32
MLA_QK = MLA_NOPE + MLA_ROPE
MLA_V = 64
ROPE_THETA = 10000.0
DIFF_HEADS = 8
DIFF_HD = 64
XA_HEADS = 4
XA_HD = D_MODEL // XA_HEADS
D_FF = 2816
CONV_W = 3

LANE = 128
HALO = 8
BF16_ROWS = 16
MXU_DIM = 256
KV_CHUNK = 512
DIFF_VT_ROWS = LANE + BF16_ROWS
MLA_VT_ROWS = MLA_V + BF16_ROWS
SHIFT_LANE = MLA_QK
SCORE_MARGIN = 1.02
MAX_STATIC_BOUND = 60.0
DIFF_AUG_PARTS = 3
POS_SPLIT = 64
LOG2E = math.log2(math.e)
NEG_BIG = -1e30
VMEM_LIMIT = 56 * 1024 * 1024

F32 = jnp.float32
BF16 = jnp.bfloat16


def _const_spec(shape):
    nd = len(shape)
    return pl.BlockSpec(shape, lambda *_: (0,) * nd, pipeline_mode=pl.Buffered(1))


def _params(sem):
    return pltpu.CompilerParams(dimension_semantics=sem, vmem_limit_bytes=VMEM_LIMIT)


def _rms(x, g):
    return x * lax.rsqrt(jnp.mean(x * x, axis=-1, keepdims=True) + EPS) * g


def _dot(a, b):
    return jnp.dot(a, b, preferred_element_type=F32)


def _dot_nt(a, b):
    return lax.dot_general(a, b, (((1,), (1,)), ((), ())), preferred_element_type=F32)


def _mem_kv_kernel(mem_ref, g_ref, w_ref, kg_ref, k_ref, v_ref):
    h = _rms(mem_ref[...], g_ref[...]).astype(BF16)
    kv = _dot(h, w_ref[...])
    kg = kg_ref[...]
    for hd in range(XA_HEADS):
        sl = slice(hd * XA_HD, (hd + 1) * XA_HD)
        k_ref[:, sl] = _rms(kv[:, sl], kg).astype(BF16)
    v_ref[...] = kv[:, D_MODEL:].astype(BF16)


def _mem_kv(mem, mem_g, w_kv, k_g):
    bm = mem.shape[0]
    out = jax.ShapeDtypeStruct((DEPTH, bm, N_MEM, D_MODEL), BF16)
    return pl.pallas_call(
        _mem_kv_kernel,
        grid=(DEPTH, bm),
        in_specs=[
            pl.BlockSpec((None, N_MEM, D_MODEL), lambda l, b: (b, 0, 0)),
            pl.BlockSpec((None, 1, D_MODEL), lambda l, b: (l, 0, 0)),
            pl.BlockSpec((None, D_MODEL, 2 * D_MODEL), lambda l, b: (l, 0, 0)),
            pl.BlockSpec((None, 1, XA_HD), lambda l, b: (l, 0, 0)),
        ],
        out_specs=[
            pl.BlockSpec((None, None, N_MEM, D_MODEL), lambda l, b: (l, b, 0, 0)),
            pl.BlockSpec((None, None, N_MEM, D_MODEL), lambda l, b: (l, b, 0, 0)),
        ],
        out_shape=[out, out],
        compiler_params=_params(("arbitrary", "arbitrary")),
        name="mem_kv",
    )(mem, mem_g[:, None, :], w_kv.astype(BF16), k_g[:, None, :])


def _mla_pre_kernel(x_ref, ng_ref, wd_ref, qlg_ref, kvlg_ref, wuq_ref, wukv_ref,
                    gq_ref, gk_ref, qshift_ref, cos_ref, sin_ref, q_ref, k_ref, vt_ref):
    h = _rms(x_ref[...], ng_ref[...]).astype(BF16)
    down = _dot(h, wd_ref[...])
    c_q = _rms(down[:, :MLA_Q_LORA], qlg_ref[...]).astype(BF16)
    kv0 = MLA_Q_LORA
    c_kv = _rms(down[:, kv0:kv0 + MLA_KV_LORA], kvlg_ref[...]).astype(BF16)
    kr = down[:, kv0 + MLA_KV_LORA:kv0 + MLA_KV_LORA + LANE]
    kr_rot = down[:, kv0 + MLA_KV_LORA + LANE:]
    qq = _dot(c_q, wuq_ref[...])
    kvp = _dot(c_kv, wukv_ref[...])
    cos = cos_ref[...]
    sin = sin_ref[...]
    gq = gq_ref[...]
    gk = gk_ref[...]
    hw = MLA_HEADS * LANE
    lane = lax.broadcasted_iota(jnp.int32, (1, LANE), 1)
    one_col = (lane == MLA_V).astype(F32)
    k_one = (lane == SHIFT_LANE).astype(F32)
    q_shift = qshift_ref[...]
    q_scale = (MLA_QK ** -0.5) * LOG2E
    for hd in range(MLA_HEADS):
        sl = slice(hd * LANE, (hd + 1) * LANE)
        qh = qq[:, sl]
        rq = lax.rsqrt(jnp.sum(qh * qh, axis=-1, keepdims=True) * (1.0 / MLA_QK) + EPS)
        q_rot = qq[:, hw + hd * LANE:hw + (hd + 1) * LANE]
        q_ref[:, sl] = ((rq * q_scale) * (qh * gq * cos + q_rot * sin) + q_shift).astype(BF16)
        kh = kvp[:, sl] + kr
        rk = lax.rsqrt(jnp.sum(kh * kh, axis=-1, keepdims=True) * (1.0 / MLA_QK) + EPS)
        k_ref[:, sl] = (rk * (kh * gk * cos + kr_rot * sin) + k_one).astype(BF16)
        v_t = (kvp[:, hw + hd * LANE:hw + (hd + 1) * LANE] + one_col).T
        vt_ref[hd] = v_t[:MLA_VT_ROWS].astype(BF16)


def _rot_half_cols(w):
    half = MLA_ROPE // 2
    return jnp.concatenate([-w[..., half:], w[..., :half]], axis=-1)


def _mla_weights(norm_g, w_down, q_lat_g, kv_lat_g, w_uq, w_ukv, q_g, k_g, w_o):
    nq, nkv = MLA_Q_LORA, MLA_KV_LORA
    w_dq, w_dkv, w_kr = w_down[:, :nq], w_down[:, nq:nq + nkv], w_down[:, nq + nkv:]
    pad_lo = jnp.zeros((D_MODEL, MLA_NOPE), F32)
    pad_hi = jnp.zeros((D_MODEL, LANE - MLA_QK), F32)
    kr_blk = jnp.concatenate([pad_lo, w_kr, pad_hi], axis=1)
    krot_blk = jnp.concatenate([pad_lo, _rot_half_cols(w_kr * k_g[MLA_NOPE:]), pad_hi], axis=1)
    wd = jnp.concatenate([w_dq, w_dkv, kr_blk, krot_blk], axis=1).astype(BF16)

    wq = w_uq.reshape(nq, MLA_HEADS, MLA_QK)
    zq = jnp.zeros((nq, MLA_HEADS, LANE - MLA_QK), F32)
    wq_main = jnp.concatenate([wq, zq], axis=-1).reshape(nq, MLA_HEADS * LANE)
    wq_rot = jnp.concatenate(
        [jnp.zeros((nq, MLA_HEADS, MLA_NOPE), F32),
         _rot_half_cols(wq[..., MLA_NOPE:] * q_g[MLA_NOPE:]), zq], axis=-1
    ).reshape(nq, MLA_HEADS * LANE)
    wuq = jnp.concatenate([wq_main, wq_rot], axis=1).astype(BF16)

    wkv = w_ukv.reshape(nkv, MLA_HEADS, MLA_NOPE + MLA_V)
    zk = jnp.zeros((nkv, MLA_HEADS, LANE - MLA_NOPE), F32)
    zv = jnp.zeros((nkv, MLA_HEADS, LANE - MLA_V), F32)
    wk = jnp.concatenate([wkv[..., :MLA_NOPE], zk], axis=-1).reshape(nkv, MLA_HEADS * LANE)
    wv = jnp.concatenate([wkv[..., MLA_NOPE:], zv], axis=-1).reshape(nkv, MLA_HEADS * LANE)
    wukv = jnp.concatenate([wk, wv], axis=1).astype(BF16)

    zg = jnp.zeros((LANE - MLA_QK,), F32)
    gq = jnp.concatenate([q_g, zg])[None, :]
    gk = jnp.concatenate([k_g, zg])[None, :]

    wo = w_o.reshape(MLA_HEADS, MLA_V, D_MODEL)
    wo = jnp.concatenate([wo, jnp.zeros((MLA_HEADS, LANE - MLA_V, D_MODEL), F32)], axis=1)
    wo = wo.reshape(MLA_HEADS * LANE, D_MODEL).astype(BF16)
    bound = _score_bound(MLA_QK, q_g, k_g)
    q_shift = -bound * (jnp.arange(LANE) == SHIFT_LANE).astype(F32)[None, :]
    weights = (norm_g[None, :], wd, q_lat_g[None, :], kv_lat_g[None, :], wuq, wukv, gq, gk, q_shift)
    return weights, wo, bound


def _score_bound(dim, q_g, k_g):
    return SCORE_MARGIN * math.sqrt(dim) * LOG2E * jnp.max(jnp.abs(q_g)) * jnp.max(jnp.abs(k_g))


def _rope_tables(seq):
    inv = ROPE_THETA ** (-jnp.arange(0, MLA_ROPE, 2, dtype=F32) / MLA_ROPE)
    ang = jnp.arange(seq, dtype=F32)[:, None] * inv[None, :]
    ang = jnp.concatenate([ang, ang], axis=-1)
    ones = jnp.ones((seq, MLA_NOPE), F32)
    zlo = jnp.zeros((seq, MLA_NOPE), F32)
    zhi = jnp.zeros((seq, LANE - MLA_QK), F32)
    cos = jnp.concatenate([ones, jnp.cos(ang), zhi], axis=1)
    sin = jnp.concatenate([zlo, jnp.sin(ang), zhi], axis=1)
    return cos, sin


def _vt_shape_and_spec(b, heads, rows, s):
    shape = jax.ShapeDtypeStruct((b, heads, s // KV_CHUNK, rows, KV_CHUNK), BF16)
    spec = pl.BlockSpec((None, heads, None, rows, KV_CHUNK), lambda bi, i: (bi, 0, i, 0, 0))
    return shape, spec


def _mla_pre(x, weights, cos, sin):
    b, s, _ = x.shape
    tm = KV_CHUNK
    hw = MLA_HEADS * LANE
    out = jax.ShapeDtypeStruct((b, s, hw), BF16)
    vt_shape, vt_spec = _vt_shape_and_spec(b, MLA_HEADS, MLA_VT_ROWS, s)
    tile = lambda bi, i: (bi, i, 0)
    w_specs = [_const_spec(w.shape) for w in weights]
    return pl.pallas_call(
        _mla_pre_kernel,
        grid=(b, s // tm),
        in_specs=[pl.BlockSpec((None, tm, D_MODEL), tile)] + w_specs + [
            pl.BlockSpec((tm, LANE), lambda bi, i: (i, 0)),
            pl.BlockSpec((tm, LANE), lambda bi, i: (i, 0)),
        ],
        out_specs=[pl.BlockSpec((None, tm, hw), tile)] * 2 + [vt_spec],
        out_shape=[out, out, vt_shape],
        compiler_params=_params(("parallel", "parallel")),
        name="mla_pre",
    )(x, *weights, cos, sin)


def _flash_core(scores, vt_ref, s_buf, mx_buf, p_buf, al_buf, m_sc, acc_sc, n_kv):
    assert n_kv >= 2 and n_kv % 2 == 0

    def issue_scores(j, slot):
        s = scores(j)
        s_buf[slot] = s
        mx_buf[slot] = jnp.max(s, axis=0, keepdims=True)

    def softmax(slot):
        m_old = m_sc[...]
        m_new = jnp.maximum(m_old, mx_buf[slot])
        p_buf[slot] = jnp.exp2(s_buf[slot] - m_new).astype(BF16)
        al_buf[slot] = jnp.exp2(m_old - m_new)
        m_sc[...] = m_new

    def values(j, slot):
        acc_sc[...] = al_buf[slot] * acc_sc[...] + _dot(vt_ref[j], p_buf[slot])

    m_sc[...] = jnp.full(m_sc.shape, NEG_BIG, F32)
    acc_sc[...] = jnp.zeros(acc_sc.shape, F32)
    issue_scores(0, 0)
    issue_scores(1, 1)
    softmax(0)

    def pair(i, carry):
        j = 2 * i + 1
        issue_scores(j + 1, 0)
        softmax(1)
        values(j - 1, 0)
        issue_scores(j + 2, 1)
        softmax(0)
        values(j, 1)
        return carry

    lax.fori_loop(0, (n_kv - 2) // 2, pair, 0)
    softmax(1)
    values(n_kv - 2, 0)
    values(n_kv - 1, 1)


def _flash_scratch(rows, nq):
    return [
        pltpu.VMEM((2, KV_CHUNK, nq), F32),
        pltpu.VMEM((2, 1, nq), F32),
        pltpu.VMEM((2, KV_CHUNK, nq), BF16),
        pltpu.VMEM((2, 1, nq), F32),
        pltpu.VMEM((1, nq), F32),
        pltpu.VMEM((rows, nq), F32),
    ]


def _key_chunk(k_ref, j):
    return k_ref[pl.ds(pl.multiple_of(j * KV_CHUNK, KV_CHUNK), KV_CHUNK), :]


def _mla_store(o_t, o_ref):
    o_ref[...] = jnp.concatenate([o_t, jnp.zeros((LANE - MLA_V, o_t.shape[1]), F32)], axis=0).T.astype(BF16)


def _mla_finalize(acc_sc, o_ref):
    acc = acc_sc[...]
    _mla_store(acc[:MLA_V] / acc[MLA_V:MLA_V + 1], o_ref)


def _flash_mla_kernel(q_ref, k_ref, vt_ref, o_ref, *scratch, n_kv):
    acc_sc = scratch[-1]
    _flash_core(lambda j: _dot_nt(_key_chunk(k_ref, j), q_ref[...]), vt_ref, *scratch, n_kv)
    _mla_finalize(acc_sc, o_ref)


def _flash_mla(q, k, vt, tq):
    b, s, hw = q.shape
    n_kv = s // KV_CHUNK
    kern = functools.partial(_flash_mla_kernel, n_kv=n_kv)
    return pl.pallas_call(
        kern,
        grid=(b, MLA_HEADS, s // tq),
        in_specs=[
            pl.BlockSpec((None, tq, LANE), lambda bi, h, i: (bi, i, h)),
            pl.BlockSpec((None, s, LANE), lambda bi, h, i: (bi, 0, h)),
            pl.BlockSpec((None, None, n_kv, MLA_VT_ROWS, KV_CHUNK), lambda bi, h, i: (bi, h, 0, 0, 0)),
        ],
        out_specs=pl.BlockSpec((None, tq, LANE), lambda bi, h, i: (bi, i, h)),
        out_shape=jax.ShapeDtypeStruct((b, s, hw), BF16),
        scratch_shapes=_flash_scratch(MLA_VT_ROWS, tq),
        compiler_params=_params(("parallel", "parallel", "arbitrary")),
        name="flash_mla",
    )(q, k, vt)


def _static_core(first_probs, probs, chunk_of, vt_ref, p_buf, acc_sc, n_kv):
    assert n_kv >= 2 and n_kv % 2 == 0

    def values(t, slot, first=False):
        pv = _dot(vt_ref[chunk_of(t)], p_buf[slot])
        acc_sc[...] = pv if first else acc_sc[...] + pv

    p_buf[0] = first_probs()
    p_buf[1] = probs(1)
    values(0, 0, first=True)

    def pair(i, carry):
        t = 2 * i + 1
        p_buf[0] = probs(t + 1)
        values(t, 1)
        p_buf[1] = probs(t + 2)
        values(t + 1, 0)
        return carry

    lax.fori_loop(0, (n_kv - 2) // 2, pair, 0, unroll=True)
    values(n_kv - 1, 1)


def _static_scratch(rows, nq):
    return [pltpu.VMEM((2, KV_CHUNK, nq), BF16), pltpu.VMEM((rows, nq), F32)]


def _query_tile_loop(n_tiles, tile_fn):
    def body(i, carry):
        tile_fn(i)
        return carry
    lax.fori_loop(0, n_tiles, body, 0)


def _flash_mla_static_kernel(q_ref, k_ref, vt_ref, o_ref, p_buf, acc_sc, *, tq, n_kv):
    def tile(i):
        rows = pl.ds(pl.multiple_of(i * tq, tq), tq)
        probs = lambda t: jnp.exp2(_dot_nt(_key_chunk(k_ref, t), q_ref[rows, :])).astype(BF16)
        _static_core(lambda: probs(0), probs, lambda t: t, vt_ref, p_buf, acc_sc, n_kv)
        _mla_finalize(acc_sc, o_ref.at[rows, :])

    _query_tile_loop(q_ref.shape[0] // tq, tile)


def _flash_mla_static(q, k, vt, tq):
    b, s, hw = q.shape
    n_kv = s // KV_CHUNK
    kern = functools.partial(_flash_mla_static_kernel, tq=tq, n_kv=n_kv)
    head_slab = pl.BlockSpec((None, s, LANE), lambda bi, h: (bi, 0, h))
    return pl.pallas_call(
        kern,
        grid=(b, MLA_HEADS),
        in_specs=[
            head_slab,
            head_slab,
            pl.BlockSpec((None, None, n_kv, MLA_VT_ROWS, KV_CHUNK), lambda bi, h: (bi, h, 0, 0, 0)),
        ],
        out_specs=head_slab,
        out_shape=jax.ShapeDtypeStruct((b, s, hw), BF16),
        scratch_shapes=_static_scratch(MLA_VT_ROWS, tq),
        compiler_params=_params(("parallel", "parallel")),
        name="flash_mla_static",
    )(q, k, vt)


def _pos_lanes(pos0, rows, first_lane):
    pos = pos0 + lax.broadcasted_iota(jnp.int32, (rows, LANE), 0)
    lane = lax.broadcasted_iota(jnp.int32, (rows, LANE), 1) - first_lane
    hi = lax.shift_right_logical(pos, int(math.log2(POS_SPLIT))).astype(F32)
    lo = (pos & (POS_SPLIT - 1)).astype(F32)
    p = DIFF_AUG_PARTS
    in_hi = (lane >= 0) & (lane < p)
    in_lo = (lane >= p) & (lane < 2 * p)
    return jnp.where(in_hi, hi, jnp.where(in_lo, lo, 0.0))


def _diff_pre_kernel(x_ref, ng_ref, w_ref, gq_ref, gk_ref, kaug_ref, q_ref, k_ref, vt_ref):
    h = _rms(x_ref[...], ng_ref[...]).astype(BF16)
    qkv = _dot(h, w_ref[...])
    hw = DIFF_HEADS * LANE
    lo = lax.broadcasted_iota(jnp.int32, (1, LANE), 1) < DIFF_HD
    q_scale = (DIFF_HD ** -0.5) * LOG2E
    tm = qkv.shape[0]
    k_pos = _pos_lanes(pl.program_id(1) * tm, tm, 0)
    ones_rows = (lax.broadcasted_iota(jnp.int32, (DIFF_VT_ROWS - LANE, tm), 0) == 0).astype(F32)

    def half_norm(t, g):
        t2 = t * t
        ss_lo = jnp.sum(jnp.where(lo, t2, 0.0), axis=-1, keepdims=True)
        ss_hi = jnp.sum(jnp.where(lo, 0.0, t2), axis=-1, keepdims=True)
        r = lax.rsqrt(jnp.where(lo, ss_lo, ss_hi) * (1.0 / DIFF_HD) + EPS)
        return t * r * g

    for hd in range(DIFF_HEADS):
        sl = slice(hd * LANE, (hd + 1) * LANE)
        q_ref[:, sl] = (half_norm(qkv[:, sl], gq_ref[...]) * q_scale).astype(BF16)
        ks = slice(2 * hd * LANE, (2 * hd + 1) * LANE)
        k_ref[:, ks] = half_norm(qkv[:, hw + hd * LANE:hw + (hd + 1) * LANE], gk_ref[...]).astype(BF16)
        k_ref[:, (2 * hd + 1) * LANE:(2 * hd + 2) * LANE] = (k_pos + kaug_ref[hd]).astype(BF16)
        v_t = qkv[:, 2 * hw + hd * LANE:2 * hw + (hd + 1) * LANE].T
        vt_ref[hd] = jnp.concatenate([v_t, ones_rows], axis=0).astype(BF16)


def _slope_pieces():
    slopes = 2.0 ** (-8.0 * jnp.arange(1, DIFF_HEADS + 1, dtype=F32) / DIFF_HEADS) * LOG2E
    pieces, rest = [], slopes
    for _ in range(DIFF_AUG_PARTS):
        piece = rest.astype(BF16).astype(F32)
        pieces.append(piece)
        rest = rest - piece
    return jnp.stack(pieces, axis=1)


def _diff_bias_tables(bound):
    p = DIFF_AUG_PARTS
    pieces = _slope_pieces()
    zeros = jnp.zeros((DIFF_HEADS, LANE - 4 * p - 1), F32)
    one = jnp.ones((DIFF_HEADS, 1), F32)
    zp = jnp.zeros((DIFF_HEADS, p), F32)
    k_tab = jnp.concatenate([zp, zp, POS_SPLIT * pieces, pieces, one, zeros], axis=1)[:, None, :]
    shift = -bound * one
    after = jnp.concatenate([-POS_SPLIT * pieces, -pieces, zp, zp, shift, zeros], axis=1)
    before = jnp.concatenate([POS_SPLIT * pieces, pieces, zp, zp, shift, zeros], axis=1)
    diag = jnp.concatenate([zp, zp, zp, zp, shift, zeros], axis=1)
    return k_tab, jnp.stack([after, before, diag], axis=1)


def _diff_pre(x, norm_g, w_qkv, q_g, k_g, k_tab):
    b, s, _ = x.shape
    tm = KV_CHUNK
    hw = DIFF_HEADS * LANE
    out = jax.ShapeDtypeStruct((b, s, hw), BF16)
    out_k = jax.ShapeDtypeStruct((b, s, 2 * hw), BF16)
    vt_shape, vt_spec = _vt_shape_and_spec(b, DIFF_HEADS, DIFF_VT_ROWS, s)
    tile = lambda bi, i: (bi, i, 0)
    weights = (norm_g[None, :], w_qkv.astype(BF16), q_g.reshape(1, LANE), k_g.reshape(1, LANE), k_tab)
    return pl.pallas_call(
        _diff_pre_kernel,
        grid=(b, s // tm),
        in_specs=[pl.BlockSpec((None, tm, D_MODEL), tile)] + [_const_spec(w.shape) for w in weights],
        out_specs=[pl.BlockSpec((None, tm, hw), tile), pl.BlockSpec((None, tm, 2 * hw), tile), vt_spec],
        out_shape=[out, out_k, vt_shape],
        compiler_params=_params(("parallel", "parallel")),
        name="diff_pre",
    )(x, *weights)


def _flash_diff_kernel(slope_ref, q_ref, k_ref, vt_ref, dmat_ref, lam_ref, subg_ref, o_ref,
                       qcat_sc, *scratch, tq, n_kv, lambda_init):
    acc_sc = scratch[-1]
    lo = lax.broadcasted_iota(jnp.int32, (1, LANE), 1) < DIFF_HD
    q = q_ref[...]
    zero = jnp.zeros_like(q)
    qcat_sc[:tq] = jnp.where(lo, q, zero)
    qcat_sc[tq:] = jnp.where(lo, zero, q)
    neg_slope = slope_ref[pl.program_id(1)]
    q0 = pl.program_id(2) * tq

    def scores(j):
        delta = (j * KV_CHUNK - q0).astype(F32)
        bias = jnp.abs(dmat_ref[...] + delta) * neg_slope
        return _dot_nt(_key_chunk(k_ref, j), qcat_sc[...]) + jnp.concatenate([bias, bias], axis=1)

    _flash_core(scores, vt_ref, *scratch, n_kv)
    _diff_finalize(acc_sc, lam_ref, subg_ref, o_ref, tq, lambda_init)


def _diff_finalize(acc_sc, lam_ref, subg_ref, o_ref, tq, lambda_init):
    acc = acc_sc[...]
    _diff_store(acc[:LANE] / acc[LANE:LANE + 1], lam_ref, subg_ref, o_ref, tq, lambda_init)


def _diff_store(o, lam_ref, subg_ref, o_ref, tq, lambda_init):
    lp = lam_ref[...]
    lam = (jnp.exp(jnp.sum(lp[0:1] * lp[1:2], axis=-1, keepdims=True))
           - jnp.exp(jnp.sum(lp[2:3] * lp[3:4], axis=-1, keepdims=True)) + lambda_init)
    o = (o[:, :tq] - lam * o[:, tq:]).T
    o = _rms(o, subg_ref[...]) * (1.0 - lambda_init)
    o_ref[...] = o.astype(BF16)


def _flash_diff_static_kernel(slope_ref, q_ref, k_ref, vt_ref, dmat_ref, lam_ref, subg_ref, qtab_ref,
                              o_ref, qa_sc, p_buf, acc_sc, *, tq, n_kv, lambda_init):
    tile = functools.partial(_flash_diff_static_tile, slope_ref, q_ref, k_ref, vt_ref, dmat_ref, lam_ref,
                             subg_ref, qtab_ref, o_ref, qa_sc, p_buf, acc_sc, tq, n_kv, lambda_init)
    _query_tile_loop(q_ref.shape[0] // tq, tile)


def _flash_diff_static_tile(slope_ref, q_ref, k_ref, vt_ref, dmat_ref, lam_ref, subg_ref, qtab_ref,
                            o_ref, qa_sc, p_buf, acc_sc, tq, n_kv, lambda_init, i):
    lo = lax.broadcasted_iota(jnp.int32, (1, LANE), 1) < DIFF_HD
    q0 = pl.multiple_of(i * tq, tq)
    q = q_ref[pl.ds(q0, tq), :]
    zero = jnp.zeros_like(q)
    halves = (jnp.where(lo, q, zero), jnp.where(lo, zero, q))
    n_d = tq // KV_CHUNK
    q_pos = _pos_lanes(q0, tq, 2 * DIFF_AUG_PARTS)
    tab = qtab_ref[...]
    aug = [(tab[v:v + 1] + sign * q_pos).astype(BF16) for v, sign in enumerate((1.0, -1.0, 0.0))]
    for v in range(2 + n_d):
        for e in range(n_d):
            kind = v if v < 2 else (2 if e == v - 2 else (1 if e > v - 2 else 0))
            blk = slice(e * KV_CHUNK, (e + 1) * KV_CHUNK)
            for c in range(2):
                rows = slice(c * tq + e * KV_CHUNK, c * tq + (e + 1) * KV_CHUNK)
                qa_sc[v, rows, :LANE] = halves[c][blk]
                qa_sc[v, rows, LANE:] = aug[kind][blk]
    jd = lax.shift_right_logical(q0, int(math.log2(KV_CHUNK)))
    neg_slope = slope_ref[pl.program_id(1)]

    def diag_probs(d):
        s = _dot_nt(_key_chunk(k_ref, jd + d), qa_sc[2 + d])
        bias = jnp.abs(dmat_ref[:, :KV_CHUNK]) * neg_slope
        cols = []
        for c in range(2):
            for e in range(n_d):
                blk = s[:, c * tq + e * KV_CHUNK:c * tq + (e + 1) * KV_CHUNK]
                cols.append(blk + bias if e == d else blk)
        return jnp.exp2(jnp.concatenate(cols, axis=1)).astype(BF16)

    def chunk_of(t):
        r = t - n_d
        return jnp.where(t < n_d, jd + t, r + n_d * (r >= jd).astype(jnp.int32))

    def probs(t):
        if isinstance(t, int) and t < n_d:
            return diag_probs(t)
        j = chunk_of(t)
        before = (j < jd).astype(jnp.int32)
        return jnp.exp2(_dot_nt(_key_chunk(k_ref, j), qa_sc[before])).astype(BF16)

    _static_core(lambda: diag_probs(0), probs, chunk_of, vt_ref, p_buf, acc_sc, n_kv)
    _diff_finalize(acc_sc, lam_ref, subg_ref, o_ref.at[pl.ds(q0, tq), :], tq, lambda_init)


def _flash_diff(q, k, vt, lam_p, sub_g, q_tab, lambda_init, tq, static_shift):
    b, s, hw = q.shape
    n_kv = s // KV_CHUNK
    slopes = 2.0 ** (-8.0 * jnp.arange(1, DIFF_HEADS + 1, dtype=F32) / DIFF_HEADS)
    neg_slopes = -slopes * LOG2E
    dmat = (jnp.arange(KV_CHUNK, dtype=F32)[:, None] - jnp.arange(tq, dtype=F32)[None, :])
    const = dict(pipeline_mode=pl.Buffered(1))
    origin = lambda *_: (0, 0)
    vt_spec = pl.BlockSpec((None, None, n_kv, DIFF_VT_ROWS, KV_CHUNK), lambda bi, h, *_: (bi, h, 0, 0, 0))
    small_specs = [
        pl.BlockSpec((KV_CHUNK, tq), origin, **const),
        pl.BlockSpec((4, DIFF_HD), origin, **const),
        pl.BlockSpec((1, LANE), origin, **const),
    ]
    args = [neg_slopes, q, k, vt, dmat, lam_p, sub_g[None, :]]
    if static_shift:
        assert tq % KV_CHUNK == 0
        kern = functools.partial(_flash_diff_static_kernel, tq=tq, n_kv=n_kv, lambda_init=lambda_init)
        grid = (b, DIFF_HEADS)
        q_spec = pl.BlockSpec((None, s, LANE), lambda bi, h, sl: (bi, 0, h))
        k_spec = pl.BlockSpec((None, s, 2 * LANE), lambda bi, h, sl: (bi, 0, h))
        in_specs = [q_spec, k_spec, vt_spec] + small_specs + [
            pl.BlockSpec((None, 3, LANE), lambda bi, h, sl: (h, 0, 0))]
        args.append(q_tab)
        n_variants = 2 + tq // KV_CHUNK
        scratch = ([pltpu.VMEM((n_variants, 2 * tq, 2 * LANE), BF16)]
                   + _static_scratch(DIFF_VT_ROWS, 2 * tq))
        name = "flash_diff_static"
    else:
        kern = functools.partial(_flash_diff_kernel, tq=tq, n_kv=n_kv, lambda_init=lambda_init)
        grid = (b, DIFF_HEADS, s // tq)
        q_spec = pl.BlockSpec((None, tq, LANE), lambda bi, h, i, sl: (bi, i, h))
        k_spec = pl.BlockSpec((None, s, LANE), lambda bi, h, i, sl: (bi, 0, 2 * h))
        in_specs = [q_spec, k_spec, vt_spec] + small_specs
        scratch = [pltpu.VMEM((2 * tq, LANE), BF16)] + _flash_scratch(DIFF_VT_ROWS, 2 * tq)
        name = "flash_diff"
    grid_spec = pltpu.PrefetchScalarGridSpec(
        num_scalar_prefetch=1, grid=grid, in_specs=in_specs, out_specs=q_spec, scratch_shapes=scratch)
    return pl.pallas_call(
        kern,
        grid_spec=grid_spec,
        out_shape=jax.ShapeDtypeStruct((b, s, hw), BF16),
        compiler_params=_params(("parallel", "parallel") + (("arbitrary",) if len(grid) == 3 else ())),
        name=name,
    )(*args)


def _post_xattn_kernel(x_ref, om_ref, wom_ref, ng_ref, wq_ref, qg_ref, k_ref, v_ref, wo_ref,
                       y_ref, o_sc):
    x1 = x_ref[...] + _dot(om_ref[...], wom_ref[...])
    h = _rms(x1, ng_ref[...]).astype(BF16)
    q = _dot(h, wq_ref[...])
    qg = qg_ref[...] * ((XA_HD ** -0.5) * LOG2E)
    for hd in range(XA_HEADS):
        sl = slice(hd * XA_HD, (hd + 1) * XA_HD)
        qh = _rms(q[:, sl], qg).astype(BF16)
        s = _dot_nt(qh, k_ref[:, sl])
        p = jnp.exp2(s - jnp.max(s, axis=-1, keepdims=True))
        l = jnp.sum(p, axis=-1, keepdims=True)
        o_sc[:, sl] = (_dot(p.astype(BF16), v_ref[:, sl]) / l).astype(BF16)
    y_ref[...] = x1 + _dot(o_sc[...], wo_ref[...])


def _post_xattn(x, o_mix, w_o_mix, norm_g, w_q, q_g, k_mem, v_mem, layer, w_o, tm):
    b, s, _ = x.shape
    tile = lambda bi, i: (bi, i, 0)
    mem_spec = pl.BlockSpec((None, None, N_MEM, D_MODEL), lambda bi, i: (layer, bi, 0, 0))
    sq = (D_MODEL, D_MODEL)
    return pl.pallas_call(
        _post_xattn_kernel,
        grid=(b, s // tm),
        in_specs=[
            pl.BlockSpec((None, tm, D_MODEL), tile),
            pl.BlockSpec((None, tm, o_mix.shape[-1]), tile),
            _const_spec(w_o_mix.shape),
            _const_spec((1, D_MODEL)),
            _const_spec(sq),
            _const_spec((1, XA_HD)),
            mem_spec,
            mem_spec,
            _const_spec(sq),
        ],
        out_specs=pl.BlockSpec((None, tm, D_MODEL), tile),
        out_shape=jax.ShapeDtypeStruct(x.shape, F32),
        scratch_shapes=[pltpu.VMEM((tm, D_MODEL), BF16)],
        compiler_params=_params(("parallel", "parallel")),
        name="post_xattn",
    )(x, o_mix, w_o_mix, norm_g[None, :], w_q.astype(BF16), q_g[None, :], k_mem, v_mem,
      w_o.astype(BF16))


def _ffn_kernel(x_ref, prev_ref, next_ref, ng_ref, wg_ref, wu_ref, cw_ref, cb_ref, wd_ref, y_ref,
                *, tm, n_chunks):
    i = pl.program_id(1)
    keep_prev = (i > 0).astype(F32)
    keep_next = (i < pl.num_programs(1) - 1).astype(F32)
    x = x_ref[...]
    xe = jnp.concatenate([prev_ref[...] * keep_prev, x, next_ref[...] * keep_next], axis=0)
    he = _rms(xe, ng_ref[...]).astype(BF16)
    hc = he[HALO:HALO + tm]
    cw = cw_ref[...]
    cb = cb_ref[...]
    y = x
    n_tiles = D_FF // MXU_DIM
    bounds = [MXU_DIM * ((n_tiles * c + n_chunks - 1) // n_chunks) for c in range(n_chunks)] + [D_FF]
    for c in range(n_chunks):
        cs = slice(bounds[c], bounds[c + 1])
        ge = _dot(he, wg_ref[:, cs])
        u = _dot(hc, wu_ref[:, cs])
        g = (ge[HALO - 1:HALO - 1 + tm] * cw[0:1, cs] + ge[HALO:HALO + tm] * cw[1:2, cs]
             + ge[HALO + 1:HALO + 1 + tm] * cw[2:3, cs] + cb[:, cs])
        act = (g * jax.nn.sigmoid(g) * u).astype(BF16)
        y = y + _dot(act, wd_ref[cs, :])
    y_ref[...] = y


def _ffn(x, norm_g, w_gu, conv_w, conv_b, w_down, tm, n_chunks=1):
    b, s, _ = x.shape
    nh = tm // HALO
    last = s // HALO - 1
    w_g = w_gu[:, :D_FF].astype(BF16)
    w_u = w_gu[:, D_FF:].astype(BF16)
    weights = (norm_g[None, :], w_g, w_u, conv_w, conv_b[None, :], w_down.astype(BF16))
    kern = functools.partial(_ffn_kernel, tm=tm, n_chunks=n_chunks)
    return pl.pallas_call(
        kern,
        grid=(b, s // tm),
        in_specs=[
            pl.BlockSpec((None, tm, D_MODEL), lambda bi, i: (bi, i, 0)),
            pl.BlockSpec((None, HALO, D_MODEL), lambda bi, i: (bi, jnp.maximum(i * nh - 1, 0), 0)),
            pl.BlockSpec((None, HALO, D_MODEL), lambda bi, i: (bi, jnp.minimum((i + 1) * nh, last), 0)),
        ] + [_const_spec(w.shape) for w in weights],
        out_specs=pl.BlockSpec((None, tm, D_MODEL), lambda bi, i: (bi, i, 0)),
        out_shape=jax.ShapeDtypeStruct(x.shape, F32),
        compiler_params=_params(("parallel", "parallel")),
        name="ffn",
    )(x, x, x, *weights)


def _tiles(s):
    return dict(tm=min(512, s), tq_mla=min(2048, s), tq_diff=min(1024, s))


def _mixer_weights(mla_p, diff_p):
    layers = []
    for i in range(DEPTH):
        j = i // 2
        if i % 2 == 0:
            weights, w_o_mix, bound = _mla_weights(*[p[j] for p in mla_p])
            layers.append(dict(weights=weights, w_o_mix=w_o_mix, bound=bound))
        else:
            norm_g, w_qkv, q_g, k_g, lam_p, sub_g, w_o = [p[j] for p in diff_p]
            bound = _score_bound(DIFF_HD, q_g, k_g)
            k_tab, q_tab = _diff_bias_tables(bound)
            layers.append(dict(pre=(norm_g, w_qkv, q_g, k_g, k_tab), lam_p=lam_p, sub_g=sub_g,
                               q_tab=q_tab, w_o_mix=w_o.astype(BF16), bound=bound))
    return layers


def _trunk(x, mem, mixers, xa_p, ffn_p):
    s = x.shape[1]
    assert s % KV_CHUNK == 0
    t = _tiles(s)
    xa_norm, xa_mem_norm, xa_w_q, xa_w_kv, xa_q_norm, xa_k_norm, xa_w_o = xa_p
    k_mem, v_mem = _mem_kv(mem, xa_mem_norm, xa_w_kv, xa_k_norm)
    cos, sin = _rope_tables(s)
    for i in range(DEPTH):
        mx = mixers[i]
        static_ok = mx["bound"] <= MAX_STATIC_BOUND
        if i % 2 == 0:
            q, k, vt = _mla_pre(x, mx["weights"], cos, sin)
            o_mix = lax.cond(static_ok,
                             lambda q, k, vt: _flash_mla_static(q, k, vt, t["tq_mla"]),
                             lambda q, k, vt: _flash_mla(q, k, vt, t["tq_mla"]), q, k, vt)
        else:
            lambda_init = 0.8 - 0.6 * math.exp(-0.3 * i)
            q, k, vt = _diff_pre(x, *mx["pre"])
            flash = lambda static: functools.partial(
                _flash_diff, lambda_init=lambda_init, tq=t["tq_diff"], static_shift=static)
            o_mix = lax.cond(static_ok, flash(True), flash(False),
                             q, k, vt, mx["lam_p"], mx["sub_g"], mx["q_tab"])
        x = _post_xattn(x, o_mix, mx["w_o_mix"], xa_norm[i], xa_w_q[i], xa_q_norm[i], k_mem, v_mem, i,
                        xa_w_o[i], t["tm"])
        x = _ffn(x, *[p[i] for p in ffn_p], t["tm"])
    return x


def kernel(x_prompt, x_sample, mem_prompt, mem_sample, mla_norm, mla_w_down, mla_q_lat_norm, mla_kv_lat_norm, mla_w_uq, mla_w_ukv, mla_q_norm, mla_k_norm, mla_w_o, diff_norm, diff_w_qkv, diff_q_norm, diff_k_norm, diff_lambda, diff_sub_norm, diff_w_o, xa_norm, xa_mem_norm, xa_w_q, xa_w_kv, xa_q_norm, xa_k_norm, xa_w_o, ffn_norm, ffn_w_gu, ffn_conv_w, ffn_conv_b, ffn_w_down):
    mla_p = (mla_norm, mla_w_down, mla_q_lat_norm, mla_kv_lat_norm, mla_w_uq, mla_w_ukv,
             mla_q_norm, mla_k_norm, mla_w_o)
    diff_p = (diff_norm, diff_w_qkv, diff_q_norm, diff_k_norm, diff_lambda, diff_sub_norm, diff_w_o)
    xa_p = (xa_norm, xa_mem_norm, xa_w_q, xa_w_kv, xa_q_norm, xa_k_norm, xa_w_o)
    ffn_p = (ffn_norm, ffn_w_gu, ffn_conv_w, ffn_conv_b, ffn_w_down)
    mixers = _mixer_weights(mla_p, diff_p)
    y_prompt = _trunk(x_prompt, mem_prompt, mixers, xa_p, ffn_p)
    y_sample = _trunk(x_sample, mem_sample, mixers, xa_p, ffn_p)
    return (y_prompt, y_sample)
```

```python
import functools
import math

import jax
import jax.numpy as jnp
from jax import lax
from jax.experimental import pallas as pl
from jax.experimental.pallas import tpu as pltpu

D_MODEL = 1024
DEPTH = 4
N_MEM = 256
EPS = 1e-6
MLA_HEADS = 8
MLA_Q_LORA = 384
MLA_KV_LORA = 256
MLA_NOPE = 64
MLA_ROPE = 32
MLA_QK = MLA_NOPE + MLA_ROPE
MLA_V = 64
ROPE_THETA = 10000.0
DIFF_HEADS = 8
DIFF_HD = 64
XA_HEADS = 4
XA_HD = D_MODEL // XA_HEADS
D_FF = 2816
CONV_W = 3

LANE = 128
HALO = 8
BF16_ROWS = 16
MXU_DIM = 256
KV_CHUNK = 512
DIFF_VT_ROWS = LANE + BF16_ROWS
MLA_VT_ROWS = MLA_V + BF16_ROWS
SHIFT_LANE = MLA_QK
SCORE_MARGIN = 1.02
MAX_STATIC_BOUND = 60.0
DIFF_AUG_PARTS = 3
POS_SPLIT = 64
LOG2E = math.log2(math.e)
NEG_BIG = -1e30
VMEM_LIMIT = 56 * 1024 * 1024

F32 = jnp.float32
BF16 = jnp.bfloat16


def _const_spec(shape):
    nd = len(shape)
    return pl.BlockSpec(shape, lambda *_: (0,) * nd, pipeline_mode=pl.Buffered(1))


def _params(sem):
    return pltpu.CompilerParams(dimension_semantics=sem, vmem_limit_bytes=VMEM_LIMIT)


def _rms(x, g):
    return x * lax.rsqrt(jnp.mean(x * x, axis=-1, keepdims=True) + EPS) * g


def _dot(a, b):
    return jnp.dot(a, b, preferred_element_type=F32)


def _dot_nt(a, b):
    return lax.dot_general(a, b, (((1,), (1,)), ((), ())), preferred_element_type=F32)


def _mem_kv_kernel(mem_ref, g_ref, w_ref, kg_ref, k_ref, v_ref):
    h = _rms(mem_ref[...], g_ref[...]).astype(BF16)
    kv = _dot(h, w_ref[...])
    kg = kg_ref[...]
    for hd in range(XA_HEADS):
        sl = slice(hd * XA_HD, (hd + 1) * XA_HD)
        k_ref[:, sl] = _rms(kv[:, sl], kg).astype(BF16)
    v_ref[...] = kv[:, D_MODEL:].astype(BF16)


def _mem_kv(mem, mem_g, w_kv, k_g):
    bm = mem.shape[0]
    out = jax.ShapeDtypeStruct((DEPTH, bm, N_MEM, D_MODEL), BF16)
    return pl.pallas_call(
        _mem_kv_kernel,
        grid=(DEPTH, bm),
        in_specs=[
            pl.BlockSpec((None, N_MEM, D_MODEL), lambda l, b: (b, 0, 0)),
            pl.BlockSpec((None, 1, D_MODEL), lambda l, b: (l, 0, 0)),
            pl.BlockSpec((None, D_MODEL, 2 * D_MODEL), lambda l, b: (l, 0, 0)),
            pl.BlockSpec((None, 1, XA_HD), lambda l, b: (l, 0, 0)),
        ],
        out_specs=[
            pl.BlockSpec((None, None, N_MEM, D_MODEL), lambda l, b: (l, b, 0, 0)),
            pl.BlockSpec((None, None, N_MEM, D_MODEL), lambda l, b: (l, b, 0, 0)),
        ],
        out_shape=[out, out],
        compiler_params=_params(("arbitrary", "arbitrary")),
        name="mem_kv",
    )(mem, mem_g[:, None, :], w_kv.astype(BF16), k_g[:, None, :])


def _mla_pre_kernel(x_ref, ng_ref, wd_ref, qlg_ref, kvlg_ref, wuq_ref, wukv_ref,
                    gq_ref, gk_ref, qshift_ref, cos_ref, sin_ref, q_ref, k_ref, vt_ref):
    h = _rms(x_ref[...], ng_ref[...]).astype(BF16)
    down = _dot(h, wd_ref[...])
    c_q = _rms(down[:, :MLA_Q_LORA], qlg_ref[...]).astype(BF16)
    kv0 = MLA_Q_LORA
    c_kv = _rms(down[:, kv0:kv0 + MLA_KV_LORA], kvlg_ref[...]).astype(BF16)
    kr = down[:, kv0 + MLA_KV_LORA:kv0 + MLA_KV_LORA + LANE]
    kr_rot = down[:, kv0 + MLA_KV_LORA + LANE:]
    qq = _dot(c_q, wuq_ref[...])
    kvp = _dot(c_kv, wukv_ref[...])
    cos = cos_ref[...]
    sin = sin_ref[...]
    gq = gq_ref[...]
    gk = gk_ref[...]
    hw = MLA_HEADS * LANE
    lane = lax.broadcasted_iota(jnp.int32, (1, LANE), 1)
    one_col = (lane == MLA_V).astype(F32)
    k_one = (lane == SHIFT_LANE).astype(F32)
    q_shift = qshift_ref[...]
    q_scale = (MLA_QK ** -0.5) * LOG2E
    for hd in range(MLA_HEADS):
        sl = slice(hd * LANE, (hd + 1) * LANE)
        qh = qq[:, sl]
        rq = lax.rsqrt(jnp.sum(qh * qh, axis=-1, keepdims=True) * (1.0 / MLA_QK) + EPS)
        q_rot = qq[:, hw + hd * LANE:hw + (hd + 1) * LANE]
        q_ref[:, sl] = ((rq * q_scale) * (qh * gq * cos + q_rot * sin) + q_shift).astype(BF16)
        kh = kvp[:, sl] + kr
        rk = lax.rsqrt(jnp.sum(kh * kh, axis=-1, keepdims=True) * (1.0 / MLA_QK) + EPS)
        k_ref[:, sl] = (rk * (kh * gk * cos + kr_rot * sin) + k_one).astype(BF16)
        v_t = (kvp[:, hw + hd * LANE:hw + (hd + 1) * LANE] + one_col).T
        vt_ref[hd] = v_t[:MLA_VT_ROWS].astype(BF16)


def _rot_half_cols(w):
    half = MLA_ROPE // 2
    return jnp.concatenate([-w[..., half:], w[..., :half]], axis=-1)


def _mla_weights(norm_g, w_down, q_lat_g, kv_lat_g, w_uq, w_ukv, q_g, k_g, w_o):
    nq, nkv = MLA_Q_LORA, MLA_KV_LORA
    w_dq, w_dkv, w_kr = w_down[:, :nq], w_down[:, nq:nq + nkv], w_down[:, nq + nkv:]
    pad_lo = jnp.zeros((D_MODEL, MLA_NOPE), F32)
    pad_hi = jnp.zeros((D_MODEL, LANE - MLA_QK), F32)
    kr_blk = jnp.concatenate([pad_lo, w_kr, pad_hi], axis=1)
    krot_blk = jnp.concatenate([pad_lo, _rot_half_cols(w_kr * k_g[MLA_NOPE:]), pad_hi], axis=1)
    wd = jnp.concatenate([w_dq, w_dkv, kr_blk, krot_blk], axis=1).astype(BF16)

    wq = w_uq.reshape(nq, MLA_HEADS, MLA_QK)
    zq = jnp.zeros((nq, MLA_HEADS, LANE - MLA_QK), F32)
    wq_main = jnp.concatenate([wq, zq], axis=-1).reshape(nq, MLA_HEADS * LANE)
    wq_rot = jnp.concatenate(
        [jnp.zeros((nq, MLA_HEADS, MLA_NOPE), F32),
         _rot_half_cols(wq[..., MLA_NOPE:] * q_g[MLA_NOPE:]), zq], axis=-1
    ).reshape(nq, MLA_HEADS * LANE)
    wuq = jnp.concatenate([wq_main, wq_rot], axis=1).astype(BF16)

    wkv = w_ukv.reshape(nkv, MLA_HEADS, MLA_NOPE + MLA_V)
    zk = jnp.zeros((nkv, MLA_HEADS, LANE - MLA_NOPE), F32)
    zv = jnp.zeros((nkv, MLA_HEADS, LANE - MLA_V), F32)
    wk = jnp.concatenate([wkv[..., :MLA_NOPE], zk], axis=-1).reshape(nkv, MLA_HEADS * LANE)
    wv = jnp.concatenate([wkv[..., MLA_NOPE:], zv], axis=-1).reshape(nkv, MLA_HEADS * LANE)
    wukv = jnp.concatenate([wk, wv], axis=1).astype(BF16)

    zg = jnp.zeros((LANE - MLA_QK,), F32)
    gq = jnp.concatenate([q_g, zg])[None, :]
    gk = jnp.concatenate([k_g, zg])[None, :]

    wo = w_o.reshape(MLA_HEADS, MLA_V, D_MODEL)
    wo = jnp.concatenate([wo, jnp.zeros((MLA_HEADS, LANE - MLA_V, D_MODEL), F32)], axis=1)
    wo = wo.reshape(MLA_HEADS * LANE, D_MODEL).astype(BF16)
    bound = _score_bound(MLA_QK, q_g, k_g)
    q_shift = -bound * (jnp.arange(LANE) == SHIFT_LANE).astype(F32)[None, :]
    weights = (norm_g[None, :], wd, q_lat_g[None, :], kv_lat_g[None, :], wuq, wukv, gq, gk, q_shift)
    return weights, wo, bound


def _score_bound(dim, q_g, k_g):
    return SCORE_MARGIN * math.sqrt(dim) * LOG2E * jnp.max(jnp.abs(q_g)) * jnp.max(jnp.abs(k_g))


def _rope_tables(seq):
    inv = ROPE_THETA ** (-jnp.arange(0, MLA_ROPE, 2, dtype=F32) / MLA_ROPE)
    ang = jnp.arange(seq, dtype=F32)[:, None] * inv[None, :]
    ang = jnp.concatenate([ang, ang], axis=-1)
    ones = jnp.ones((seq, MLA_NOPE), F32)
    zlo = jnp.zeros((seq, MLA_NOPE), F32)
    zhi = jnp.zeros((seq, LANE - MLA_QK), F32)
    cos = jnp.concatenate([ones, jnp.cos(ang), zhi], axis=1)
    sin = jnp.concatenate([zlo, jnp.sin(ang), zhi], axis=1)
    return cos, sin


def _vt_shape_and_spec(b, heads, rows, s):
    shape = jax.ShapeDtypeStruct((b, heads, s // KV_CHUNK, rows, KV_CHUNK), BF16)
    spec = pl.BlockSpec((None, heads, None, rows, KV_CHUNK), lambda bi, i: (bi, 0, i, 0, 0))
    return shape, spec


def _mla_pre(x, weights, cos, sin):
    b, s, _ = x.shape
    tm = KV_CHUNK
    hw = MLA_HEADS * LANE
    out = jax.ShapeDtypeStruct((b, s, hw), BF16)
    vt_shape, vt_spec = _vt_shape_and_spec(b, MLA_HEADS, MLA_VT_ROWS, s)
    tile = lambda bi, i: (bi, i, 0)
    w_specs = [_const_spec(w.shape) for w in weights]
    return pl.pallas_call(
        _mla_pre_kernel,
        grid=(b, s // tm),
        in_specs=[pl.BlockSpec((None, tm, D_MODEL), tile)] + w_specs + [
            pl.BlockSpec((tm, LANE), lambda bi, i: (i, 0)),
            pl.BlockSpec((tm, LANE), lambda bi, i: (i, 0)),
        ],
        out_specs=[pl.BlockSpec((None, tm, hw), tile)] * 2 + [vt_spec],
        out_shape=[out, out, vt_shape],
        compiler_params=_params(("parallel", "parallel")),
        name="mla_pre",
    )(x, *weights, cos, sin)


def _flash_core(scores, vt_ref, s_buf, mx_buf, p_buf, al_buf, m_sc, acc_sc, n_kv):
    assert n_kv >= 2 and n_kv % 2 == 0

    def issue_scores(j, slot):
        s = scores(j)
        s_buf[slot] = s
        mx_buf[slot] = jnp.max(s, axis=0, keepdims=True)

    def softmax(slot):
        m_old = m_sc[...]
        m_new = jnp.maximum(m_old, mx_buf[slot])
        p_buf[slot] = jnp.exp2(s_buf[slot] - m_new).astype(BF16)
        al_buf[slot] = jnp.exp2(m_old - m_new)
        m_sc[...] = m_new

    def values(j, slot):
        acc_sc[...] = al_buf[slot] * acc_sc[...] + _dot(vt_ref[j], p_buf[slot])

    m_sc[...] = jnp.full(m_sc.shape, NEG_BIG, F32)
    acc_sc[...] = jnp.zeros(acc_sc.shape, F32)
    issue_scores(0, 0)
    issue_scores(1, 1)
    softmax(0)

    def pair(i, carry):
        j = 2 * i + 1
        issue_scores(j + 1, 0)
        softmax(1)
        values(j - 1, 0)
        issue_scores(j + 2, 1)
        softmax(0)
        values(j, 1)
        return carry

    lax.fori_loop(0, (n_kv - 2) // 2, pair, 0)
    softmax(1)
    values(n_kv - 2, 0)
    values(n_kv - 1, 1)


def _flash_scratch(rows, nq):
    return [
        pltpu.VMEM((2, KV_CHUNK, nq), F32),
        pltpu.VMEM((2, 1, nq), F32),
        pltpu.VMEM((2, KV_CHUNK, nq), BF16),
        pltpu.VMEM((2, 1, nq), F32),
        pltpu.VMEM((1, nq), F32),
        pltpu.VMEM((rows, nq), F32),
    ]


def _key_chunk(k_ref, j):
    return k_ref[pl.ds(pl.multiple_of(j * KV_CHUNK, KV_CHUNK), KV_CHUNK), :]


def _mla_store(o_t, o_ref):
    o_ref[...] = jnp.concatenate([o_t, jnp.zeros((LANE - MLA_V, o_t.shape[1]), F32)], axis=0).T.astype(BF16)


def _mla_finalize(acc_sc, o_ref):
    acc = acc_sc[...]
    _mla_store(acc[:MLA_V] / acc[MLA_V:MLA_V + 1], o_ref)


def _flash_mla_kernel(q_ref, k_ref, vt_ref, o_ref, *scratch, n_kv):
    acc_sc = scratch[-1]
    _flash_core(lambda j: _dot_nt(_key_chunk(k_ref, j), q_ref[...]), vt_ref, *scratch, n_kv)
    _mla_finalize(acc_sc, o_ref)


def _flash_mla(q, k, vt, tq):
    b, s, hw = q.shape
    n_kv = s // KV_CHUNK
    kern = functools.partial(_flash_mla_kernel, n_kv=n_kv)
    return pl.pallas_call(
        kern,
        grid=(b, MLA_HEADS, s // tq),
        in_specs=[
            pl.BlockSpec((None, tq, LANE), lambda bi, h, i: (bi, i, h)),
            pl.BlockSpec((None, s, LANE), lambda bi, h, i: (bi, 0, h)),
            pl.BlockSpec((None, None, n_kv, MLA_VT_ROWS, KV_CHUNK), lambda bi, h, i: (bi, h, 0, 0, 0)),
        ],
        out_specs=pl.BlockSpec((None, tq, LANE), lambda bi, h, i: (bi, i, h)),
        out_shape=jax.ShapeDtypeStruct((b, s, hw), BF16),
        scratch_shapes=_flash_scratch(MLA_VT_ROWS, tq),
        compiler_params=_params(("parallel", "parallel", "arbitrary")),
        name="flash_mla",
    )(q, k, vt)


def _static_core(first_probs, probs, chunk_of, vt_ref, p_buf, acc_sc, n_kv):
    assert n_kv >= 2 and n_kv % 2 == 0

    def values(t, slot, first=False):
        pv = _dot(vt_ref[chunk_of(t)], p_buf[slot])
        acc_sc[...] = pv if first else acc_sc[...] + pv

    p_buf[0] = first_probs()
    p_buf[1] = probs(1)
    values(0, 0, first=True)

    for t in range(1, n_kv - 1, 2):
        p_buf[0] = probs(t + 1)
        values(t, 1)
        p_buf[1] = probs(t + 2)
        values(t + 1, 0)
    values(n_kv - 1, 1)


def _static_scratch(rows, nq):
    return [pltpu.VMEM((2, KV_CHUNK, nq), BF16), pltpu.VMEM((rows, nq), F32)]


def _query_tile_loop(n_tiles, tile_fn):
    def body(i, carry):
        tile_fn(i)
        return carry
    lax.fori_loop(0, n_tiles, body, 0)


def _flash_mla_static_kernel(q_ref, k_ref, vt_ref, o_ref, p_buf, acc_sc, *, tq, n_kv):
    def tile(i):
        rows = pl.ds(pl.multiple_of(i * tq, tq), tq)
        probs = lambda t: jnp.exp2(_dot_nt(_key_chunk(k_ref, t), q_ref[rows, :])).astype(BF16)
        _static_core(lambda: probs(0), probs, lambda t: t, vt_ref, p_buf, acc_sc, n_kv)
        _mla_finalize(acc_sc, o_ref.at[rows, :])

    _query_tile_loop(q_ref.shape[0] // tq, tile)


def _flash_mla_static(q, k, vt, tq):
    b, s, hw = q.shape
    n_kv = s // KV_CHUNK
    kern = functools.partial(_flash_mla_static_kernel, tq=tq, n_kv=n_kv)
    head_slab = pl.BlockSpec((None, s, LANE), lambda bi, h: (bi, 0, h))
    return pl.pallas_call(
        kern,
        grid=(b, MLA_HEADS),
        in_specs=[
            head_slab,
            head_slab,
            pl.BlockSpec((None, None, n_kv, MLA_VT_ROWS, KV_CHUNK), lambda bi, h: (bi, h, 0, 0, 0)),
        ],
        out_specs=head_slab,
        out_shape=jax.ShapeDtypeStruct((b, s, hw), BF16),
        scratch_shapes=_static_scratch(MLA_VT_ROWS, tq),
        compiler_params=_params(("parallel", "parallel")),
        name="flash_mla_static",
    )(q, k, vt)


def _pos_lanes(pos0, rows, first_lane):
    pos = pos0 + lax.broadcasted_iota(jnp.int32, (rows, LANE), 0)
    lane = lax.broadcasted_iota(jnp.int32, (rows, LANE), 1) - first_lane
    hi = lax.shift_right_logical(pos, int(math.log2(POS_SPLIT))).astype(F32)
    lo = (pos & (POS_SPLIT - 1)).astype(F32)
    p = DIFF_AUG_PARTS
    in_hi = (lane >= 0) & (lane < p)
    in_lo = (lane >= p) & (lane < 2 * p)
    return jnp.where(in_hi, hi, jnp.where(in_lo, lo, 0.0))


def _diff_pre_kernel(x_ref, ng_ref, w_ref, gq_ref, gk_ref, kaug_ref, q_ref, k_ref, vt_ref):
    h = _rms(x_ref[...], ng_ref[...]).astype(BF16)
    qkv = _dot(h, w_ref[...])
    hw = DIFF_HEADS * LANE
    lo = lax.broadcasted_iota(jnp.int32, (1, LANE), 1) < DIFF_HD
    q_scale = (DIFF_HD ** -0.5) * LOG2E
    tm = qkv.shape[0]
    k_pos = _pos_lanes(pl.program_id(1) * tm, tm, 0)
    ones_rows = (lax.broadcasted_iota(jnp.int32, (DIFF_VT_ROWS - LANE, tm), 0) == 0).astype(F32)

    def half_norm(t, g):
        t2 = t * t
        ss_lo = jnp.sum(jnp.where(lo, t2, 0.0), axis=-1, keepdims=True)
        ss_hi = jnp.sum(jnp.where(lo, 0.0, t2), axis=-1, keepdims=True)
        r = lax.rsqrt(jnp.where(lo, ss_lo, ss_hi) * (1.0 / DIFF_HD) + EPS)
        return t * r * g

    for hd in range(DIFF_HEADS):
        sl = slice(hd * LANE, (hd + 1) * LANE)
        q_ref[:, sl] = (half_norm(qkv[:, sl], gq_ref[...]) * q_scale).astype(BF16)
        ks = slice(2 * hd * LANE, (2 * hd + 1) * LANE)
        k_ref[:, ks] = half_norm(qkv[:, hw + hd * LANE:hw + (hd + 1) * LANE], gk_ref[...]).astype(BF16)
        k_ref[:, (2 * hd + 1) * LANE:(2 * hd + 2) * LANE] = (k_pos + kaug_ref[hd]).astype(BF16)
        v_t = qkv[:, 2 * hw + hd * LANE:2 * hw + (hd + 1) * LANE].T
        vt_ref[hd] = jnp.concatenate([v_t, ones_rows], axis=0).astype(BF16)


def _slope_pieces():
    slopes = 2.0 ** (-8.0 * jnp.arange(1, DIFF_HEADS + 1, dtype=F32) / DIFF_HEADS) * LOG2E
    pieces, rest = [], slopes
    for _ in range(DIFF_AUG_PARTS):
        piece = rest.astype(BF16).astype(F32)
        pieces.append(piece)
        rest = rest - piece
    return jnp.stack(pieces, axis=1)


def _diff_bias_tables(bound):
    p = DIFF_AUG_PARTS
    pieces = _slope_pieces()
    zeros = jnp.zeros((DIFF_HEADS, LANE - 4 * p - 1), F32)
    one = jnp.ones((DIFF_HEADS, 1), F32)
    zp = jnp.zeros((DIFF_HEADS, p), F32)
    k_tab = jnp.concatenate([zp, zp, POS_SPLIT * pieces, pieces, one, zeros], axis=1)[:, None, :]
    shift = -bound * one
    after = jnp.concatenate([-POS_SPLIT * pieces, -pieces, zp, zp, shift, zeros], axis=1)
    before = jnp.concatenate([POS_SPLIT * pieces, pieces, zp, zp, shift, zeros], axis=1)
    diag = jnp.concatenate([zp, zp, zp, zp, shift, zeros], axis=1)
    return k_tab, jnp.stack([after, before, diag], axis=1)


def _diff_pre(x, norm_g, w_qkv, q_g, k_g, k_tab):
    b, s, _ = x.shape
    tm = KV_CHUNK
    hw = DIFF_HEADS * LANE
    out = jax.ShapeDtypeStruct((b, s, hw), BF16)
    out_k = jax.ShapeDtypeStruct((b, s, 2 * hw), BF16)
    vt_shape, vt_spec = _vt_shape_and_spec(b, DIFF_HEADS, DIFF_VT_ROWS, s)
    tile = lambda bi, i: (bi, i, 0)
    weights = (norm_g[None, :], w_qkv.astype(BF16), q_g.reshape(1, LANE), k_g.reshape(1, LANE), k_tab)
    return pl.pallas_call(
        _diff_pre_kernel,
        grid=(b, s // tm),
        in_specs=[pl.BlockSpec((None, tm, D_MODEL), tile)] + [_const_spec(w.shape) for w in weights],
        out_specs=[pl.BlockSpec((None, tm, hw), tile), pl.BlockSpec((None, tm, 2 * hw), tile), vt_spec],
        out_shape=[out, out_k, vt_shape],
        compiler_params=_params(("parallel", "parallel")),
        name="diff_pre",
    )(x, *weights)


def _flash_diff_kernel(slope_ref, q_ref, k_ref, vt_ref, dmat_ref, lam_ref, subg_ref, o_ref,
                       qcat_sc, *scratch, tq, n_kv, lambda_init):
    acc_sc = scratch[-1]
    lo = lax.broadcasted_iota(jnp.int32, (1, LANE), 1) < DIFF_HD
    q = q_ref[...]
    zero = jnp.zeros_like(q)
    qcat_sc[:tq] = jnp.where(lo, q, zero)
    qcat_sc[tq:] = jnp.where(lo, zero, q)
    neg_slope = slope_ref[pl.program_id(1)]
    q0 = pl.program_id(2) * tq

    def scores(j):
        delta = (j * KV_CHUNK - q0).astype(F32)
        bias = jnp.abs(dmat_ref[...] + delta) * neg_slope
        return _dot_nt(_key_chunk(k_ref, j), qcat_sc[...]) + jnp.concatenate([bias, bias], axis=1)

    _flash_core(scores, vt_ref, *scratch, n_kv)
    _diff_finalize(acc_sc, lam_ref, subg_ref, o_ref, tq, lambda_init)


def _diff_finalize(acc_sc, lam_ref, subg_ref, o_ref, tq, lambda_init):
    acc = acc_sc[...]
    _diff_store(acc[:LANE] / acc[LANE:LANE + 1], lam_ref, subg_ref, o_ref, tq, lambda_init)


def _diff_store(o, lam_ref, subg_ref, o_ref, tq, lambda_init):
    lp = lam_ref[...]
    lam = (jnp.exp(jnp.sum(lp[0:1] * lp[1:2], axis=-1, keepdims=True))
           - jnp.exp(jnp.sum(lp[2:3] * lp[3:4], axis=-1, keepdims=True)) + lambda_init)
    o = (o[:, :tq] - lam * o[:, tq:]).T
    o = _rms(o, subg_ref[...]) * (1.0 - lambda_init)
    o_ref[...] = o.astype(BF16)


def _flash_diff_static_kernel(slope_ref, q_ref, k_ref, vt_ref, dmat_ref, lam_ref, subg_ref, qtab_ref,
                              o_ref, qz_sc, aug_sc, p_buf, acc_sc, *, tq, n_kv, lambda_init):
    tile = functools.partial(_flash_diff_static_tile, slope_ref, q_ref, k_ref, vt_ref, dmat_ref, lam_ref,
                             subg_ref, qtab_ref, o_ref, qz_sc, aug_sc, p_buf, acc_sc, tq, n_kv,
                             lambda_init)
    _query_tile_loop(q_ref.shape[0] // tq, tile)


def _flash_diff_static_tile(slope_ref, q_ref, k_ref, vt_ref, dmat_ref, lam_ref, subg_ref, qtab_ref,
                            o_ref, qz_sc, aug_sc, p_buf, acc_sc, tq, n_kv, lambda_init, i):
    lo = lax.broadcasted_iota(jnp.int32, (1, LANE), 1) < DIFF_HD
    q0 = pl.multiple_of(i * tq, tq)
    q = q_ref[pl.ds(q0, tq), :]
    zero = jnp.zeros_like(q)
    qz_sc[:tq] = jnp.where(lo, q, zero)
    qz_sc[tq:] = jnp.where(lo, zero, q)
    n_d = tq // KV_CHUNK
    q_pos = _pos_lanes(q0, tq, 2 * DIFF_AUG_PARTS)
    tab = qtab_ref[...]
    for kind, sign in enumerate((1.0, -1.0, 0.0)):
        aug_sc[kind] = (tab[kind:kind + 1] + sign * q_pos).astype(BF16)
    jd = lax.shift_right_logical(q0, int(math.log2(KV_CHUNK)))
    neg_slope = slope_ref[pl.program_id(1)]

    def with_bias_columns(aug):
        return jnp.concatenate([qz_sc[...], jnp.concatenate([aug, aug], axis=0)], axis=1)

    def diag_probs(d):
        kinds = [2 if e == d else (1 if e > d else 0) for e in range(n_d)]
        aug = jnp.concatenate([aug_sc[kind, e * KV_CHUNK:(e + 1) * KV_CHUNK, :]
                               for e, kind in enumerate(kinds)], axis=0)
        s = _dot_nt(_key_chunk(k_ref, jd + d), with_bias_columns(aug))
        bias = jnp.abs(dmat_ref[...]) * neg_slope
        cols = []
        for c in range(2):
            for e in range(n_d):
                blk = s[:, c * tq + e * KV_CHUNK:c * tq + (e + 1) * KV_CHUNK]
                cols.append(blk + bias if e == d else blk)
        return jnp.exp2(jnp.concatenate(cols, axis=1)).astype(BF16)

    def chunk_of(t):
        r = t - n_d
        return jnp.where(t < n_d, jd + t, r + n_d * (r >= jd).astype(jnp.int32))

    def probs(t):
        if isinstance(t, int) and t < n_d:
            return diag_probs(t)
        j = chunk_of(t)
        before = (j < jd).astype(jnp.int32)
        return jnp.exp2(_dot_nt(_key_chunk(k_ref, j), with_bias_columns(aug_sc[before]))).astype(BF16)

    _static_core(lambda: diag_probs(0), probs, chunk_of, vt_ref, p_buf, acc_sc, n_kv)
    _diff_finalize(acc_sc, lam_ref, subg_ref, o_ref.at[pl.ds(q0, tq), :], tq, lambda_init)


def _flash_diff(q, k, vt, lam_p, sub_g, q_tab, lambda_init, tq, static_shift):
    b, s, hw = q.shape
    n_kv = s // KV_CHUNK
    slopes = 2.0 ** (-8.0 * jnp.arange(1, DIFF_HEADS + 1, dtype=F32) / DIFF_HEADS)
    neg_slopes = -slopes * LOG2E
    d_cols = KV_CHUNK if static_shift else tq
    dmat = (jnp.arange(KV_CHUNK, dtype=F32)[:, None] - jnp.arange(d_cols, dtype=F32)[None, :])
    const = dict(pipeline_mode=pl.Buffered(1))
    origin = lambda *_: (0, 0)
    vt_spec = pl.BlockSpec((None, None, n_kv, DIFF_VT_ROWS, KV_CHUNK), lambda bi, h, *_: (bi, h, 0, 0, 0))
    small_specs = [
        pl.BlockSpec((KV_CHUNK, d_cols), origin, **const),
        pl.BlockSpec((4, DIFF_HD), origin, **const),
        pl.BlockSpec((1, LANE), origin, **const),
    ]
    args = [neg_slopes, q, k, vt, dmat, lam_p, sub_g[None, :]]
    if static_shift:
        assert tq % KV_CHUNK == 0
        kern = functools.partial(_flash_diff_static_kernel, tq=tq, n_kv=n_kv, lambda_init=lambda_init)
        grid = (b, DIFF_HEADS)
        q_spec = pl.BlockSpec((None, s, LANE), lambda bi, h, sl: (bi, 0, h))
        k_spec = pl.BlockSpec((None, s, 2 * LANE), lambda bi, h, sl: (bi, 0, h))
        in_specs = [q_spec, k_spec, vt_spec] + small_specs + [
            pl.BlockSpec((None, 3, LANE), lambda bi, h, sl: (h, 0, 0))]
        args.append(q_tab)
        scratch = ([pltpu.VMEM((2 * tq, LANE), BF16), pltpu.VMEM((3, tq, LANE), BF16)]
                   + _static_scratch(DIFF_VT_ROWS, 2 * tq))
        name = "flash_diff_static"
    else:
        kern = functools.partial(_flash_diff_kernel, tq=tq, n_kv=n_kv, lambda_init=lambda_init)
        grid = (b, DIFF_HEADS, s // tq)
        q_spec = pl.BlockSpec((None, tq, LANE), lambda bi, h, i, sl: (bi, i, h))
        k_spec = pl.BlockSpec((None, s, LANE), lambda bi, h, i, sl: (bi, 0, 2 * h))
        in_specs = [q_spec, k_spec, vt_spec] + small_specs
        scratch = [pltpu.VMEM((2 * tq, LANE), BF16)] + _flash_scratch(DIFF_VT_ROWS, 2 * tq)
        name = "flash_diff"
    grid_spec = pltpu.PrefetchScalarGridSpec(
        num_scalar_prefetch=1, grid=grid, in_specs=in_specs, out_specs=q_spec, scratch_shapes=scratch)
    return pl.pallas_call(
        kern,
        grid_spec=grid_spec,
        out_shape=jax.ShapeDtypeStruct((b, s, hw), BF16),
        compiler_params=_params(("parallel", "parallel") + (("arbitrary",) if len(grid) == 3 else ())),
        name=name,
    )(*args)


def _post_xattn_kernel(x_ref, om_ref, wom_ref, ng_ref, wq_ref, qg_ref, k_ref, v_ref, wo_ref,
                       y_ref, o_sc):
    x1 = x_ref[...] + _dot(om_ref[...], wom_ref[...])
    h = _rms(x1, ng_ref[...]).astype(BF16)
    q = _dot(h, wq_ref[...])
    qg = qg_ref[...] * ((XA_HD ** -0.5) * LOG2E)
    for hd in range(XA_HEADS):
        sl = slice(hd * XA_HD, (hd + 1) * XA_HD)
        qh = _rms(q[:, sl], qg).astype(BF16)
        s = _dot_nt(qh, k_ref[:, sl])
        p = jnp.exp2(s - jnp.max(s, axis=-1, keepdims=True))
        l = jnp.sum(p, axis=-1, keepdims=True)
        o_sc[:, sl] = (_dot(p.astype(BF16), v_ref[:, sl]) / l).astype(BF16)
    y_ref[...] = x1 + _dot(o_sc[...], wo_ref[...])


def _post_xattn(x, o_mix, w_o_mix, norm_g, w_q, q_g, k_mem, v_mem, layer, w_o, tm):
    b, s, _ = x.shape
    tile = lambda bi, i: (bi, i, 0)
    mem_spec = pl.BlockSpec((None, None, N_MEM, D_MODEL), lambda bi, i: (layer, bi, 0, 0))
    sq = (D_MODEL, D_MODEL)
    return pl.pallas_call(
        _post_xattn_kernel,
        grid=(b, s // tm),
        in_specs=[
            pl.BlockSpec((None, tm, D_MODEL), tile),
            pl.BlockSpec((None, tm, o_mix.shape[-1]), tile),
            _const_spec(w_o_mix.shape),
            _const_spec((1, D_MODEL)),
            _const_spec(sq),
            _const_spec((1, XA_HD)),
            mem_spec,
            mem_spec,
            _const_spec(sq),
        ],
        out_specs=pl.BlockSpec((None, tm, D_MODEL), tile),
        out_shape=jax.ShapeDtypeStruct(x.shape, F32),
        scratch_shapes=[pltpu.VMEM((tm, D_MODEL), BF16)],
        compiler_params=_params(("parallel", "parallel")),
        name="post_xattn",
    )(x, o_mix, w_o_mix, norm_g[None, :], w_q.astype(BF16), q_g[None, :], k_mem, v_mem,
      w_o.astype(BF16))


def _ffn_kernel(x_ref, prev_ref, next_ref, ng_ref, wg_ref, wu_ref, cw_ref, cb_ref, wd_ref, y_ref,
                *, tm, n_chunks):
    i = pl.program_id(1)
    keep_prev = (i > 0).astype(F32)
    keep_next = (i < pl.num_programs(1) - 1).astype(F32)
    x = x_ref[...]
    xe = jnp.concatenate([prev_ref[...] * keep_prev, x, next_ref[...] * keep_next], axis=0)
    he = _rms(xe, ng_ref[...]).astype(BF16)
    hc = he[HALO:HALO + tm]
    cw = cw_ref[...]
    cb = cb_ref[...]
    y = x
    n_tiles = D_FF // MXU_DIM
    bounds = [MXU_DIM * ((n_tiles * c + n_chunks - 1) // n_chunks) for c in range(n_chunks)] + [D_FF]
    for c in range(n_chunks):
        cs = slice(bounds[c], bounds[c + 1])
        ge = _dot(he, wg_ref[:, cs])
        u = _dot(hc, wu_ref[:, cs])
        g = (ge[HALO - 1:HALO - 1 + tm] * cw[0:1, cs] + ge[HALO:HALO + tm] * cw[1:2, cs]
             + ge[HALO + 1:HALO + 1 + tm] * cw[2:3, cs] + cb[:, cs])
        act = (g * jax.nn.sigmoid(g) * u).astype(BF16)
        y = y + _dot(act, wd_ref[cs, :])
    y_ref[...] = y


def _ffn(x, norm_g, w_gu, conv_w, conv_b, w_down, tm, n_chunks=1):
    b, s, _ = x.shape
    nh = tm // HALO
    last = s // HALO - 1
    w_g = w_gu[:, :D_FF].astype(BF16)
    w_u = w_gu[:, D_FF:].astype(BF16)
    weights = (norm_g[None, :], w_g, w_u, conv_w, conv_b[None, :], w_down.astype(BF16))
    kern = functools.partial(_ffn_kernel, tm=tm, n_chunks=n_chunks)
    return pl.pallas_call(
        kern,
        grid=(b, s // tm),
        in_specs=[
            pl.BlockSpec((None, tm, D_MODEL), lambda bi, i: (bi, i, 0)),
            pl.BlockSpec((None, HALO, D_MODEL), lambda bi, i: (bi, jnp.maximum(i * nh - 1, 0), 0)),
            pl.BlockSpec((None, HALO, D_MODEL), lambda bi, i: (bi, jnp.minimum((i + 1) * nh, last), 0)),
        ] + [_const_spec(w.shape) for w in weights],
        out_specs=pl.BlockSpec((None, tm, D_MODEL), lambda bi, i: (bi, i, 0)),
        out_shape=jax.ShapeDtypeStruct(x.shape, F32),
        compiler_params=_params(("parallel", "parallel")),
        name="ffn",
    )(x, x, x, *weights)


def _tiles(s):
    return dict(tm=min(512, s), tq_mla=min(4096, s), tq_diff=min(2048, s),
                tq_mla_online=min(1024, s), tq_diff_online=min(512, s))


def _mixer_weights(mla_p, diff_p):
    layers = []
    for i in range(DEPTH):
        j = i // 2
        if i % 2 == 0:
            weights, w_o_mix, bound = _mla_weights(*[p[j] for p in mla_p])
            layers.append(dict(weights=weights, w_o_mix=w_o_mix, bound=bound))
        else:
            norm_g, w_qkv, q_g, k_g, lam_p, sub_g, w_o = [p[j] for p in diff_p]
            bound = _score_bound(DIFF_HD, q_g, k_g)
            k_tab, q_tab = _diff_bias_tables(bound)
            layers.append(dict(pre=(norm_g, w_qkv, q_g, k_g, k_tab), lam_p=lam_p, sub_g=sub_g,
                               q_tab=q_tab, w_o_mix=w_o.astype(BF16), bound=bound))
    return layers


def _trunk(x, mem, mixers, xa_p, ffn_p):
    s = x.shape[1]
    assert s % KV_CHUNK == 0
    t = _tiles(s)
    xa_norm, xa_mem_norm, xa_w_q, xa_w_kv, xa_q_norm, xa_k_norm, xa_w_o = xa_p
    k_mem, v_mem = _mem_kv(mem, xa_mem_norm, xa_w_kv, xa_k_norm)
    cos, sin = _rope_tables(s)
    for i in range(DEPTH):
        mx = mixers[i]
        static_ok = mx["bound"] <= MAX_STATIC_BOUND
        if i % 2 == 0:
            q, k, vt = _mla_pre(x, mx["weights"], cos, sin)
            o_mix = lax.cond(static_ok,
                             lambda q, k, vt: _flash_mla_static(q, k, vt, t["tq_mla"]),
                             lambda q, k, vt: _flash_mla(q, k, vt, t["tq_mla_online"]), q, k, vt)
        else:
            lambda_init = 0.8 - 0.6 * math.exp(-0.3 * i)
            q, k, vt = _diff_pre(x, *mx["pre"])
            flash = lambda static: functools.partial(
                _flash_diff, lambda_init=lambda_init, static_shift=static,
                tq=t["tq_diff"] if static else t["tq_diff_online"])
            o_mix = lax.cond(static_ok, flash(True), flash(False),
                             q, k, vt, mx["lam_p"], mx["sub_g"], mx["q_tab"])
        x = _post_xattn(x, o_mix, mx["w_o_mix"], xa_norm[i], xa_w_q[i], xa_q_norm[i], k_mem, v_mem, i,
                        xa_w_o[i], t["tm"])
        x = _ffn(x, *[p[i] for p in ffn_p], t["tm"])
    return x


def kernel(x_prompt, x_sample, mem_prompt, mem_sample, mla_norm, mla_w_down, mla_q_lat_norm, mla_kv_lat_norm, mla_w_uq, mla_w_ukv, mla_q_norm, mla_k_norm, mla_w_o, diff_norm, diff_w_qkv, diff_q_norm, diff_k_norm, diff_lambda, diff_sub_norm, diff_w_o, xa_norm, xa_mem_norm, xa_w_q, xa_w_kv, xa_q_norm, xa_k_norm, xa_w_o, ffn_norm, ffn_w_gu, ffn_conv_w, ffn_conv_b, ffn_w_down):
    mla_p = (mla_norm, mla_w_down, mla_q_lat_norm, mla_kv_lat_norm, mla_w_uq, mla_w_ukv,
             mla_q_norm, mla_k_norm, mla_w_o)
    diff_p = (diff_norm, diff_w_qkv, diff_q_norm, diff_k_norm, diff_lambda, diff_sub_norm, diff_w_o)
    xa_p = (xa_norm, xa_mem_norm, xa_w_q, xa_w_kv, xa_q_norm, xa_k_norm, xa_w_o)
    ffn_p = (ffn_norm, ffn_w_gu, ffn_conv_w, ffn_conv_b, ffn_w_down)
    mixers = _mixer_weights(mla_p, diff_p)
    y_prompt = _trunk(x_prompt, mem_prompt, mixers, xa_p, ffn_p)
    y_sample = _trunk(x_sample, mem_sample, mixers, xa_p, ffn_p)
    return (y_prompt, y_sample)
```

```python
import functools
import math

import jax
import jax.numpy as jnp
from jax import lax
from jax.experimental import pallas as pl
from jax.experimental.pallas import tpu as pltpu

D_MODEL = 1024
DEPTH = 4
N_MEM = 256
EPS = 1e-6
MLA_HEADS = 8
MLA_Q_LORA = 384
MLA_KV_LORA = 256
MLA_NOPE = 64
MLA_ROPE = 32
MLA_QK = MLA_NOPE + MLA_ROPE
MLA_V = 64
ROPE_THETA = 10000.0
DIFF_HEADS = 8
DIFF_HD = 64
XA_HEADS = 4
XA_HD = D_MODEL // XA_HEADS
D_FF = 2816
CONV_W = 3

LANE = 128
HALO = 8
BF16_ROWS = 16
MXU_DIM = 256
KV_CHUNK = 512
DIFF_VT_ROWS = LANE + BF16_ROWS
MLA_VT_ROWS = MLA_V + BF16_ROWS
SHIFT_LANE = MLA_QK
SCORE_MARGIN = 1.02
MAX_STATIC_BOUND = 40.0
DIFF_AUG_PARTS = 3
POS_SPLIT = 64
LOG2E = math.log2(math.e)
NEG_BIG = -1e30
VMEM_LIMIT = 56 * 1024 * 1024

F32 = jnp.float32
BF16 = jnp.bfloat16


def _const_spec(shape):
    nd = len(shape)
    return pl.BlockSpec(shape, lambda *_: (0,) * nd, pipeline_mode=pl.Buffered(1))


def _params(sem):
    return pltpu.CompilerParams(dimension_semantics=sem, vmem_limit_bytes=VMEM_LIMIT)


def _rms(x, g):
    return x * lax.rsqrt(jnp.mean(x * x, axis=-1, keepdims=True) + EPS) * g


def _dot(a, b):
    return jnp.dot(a, b, preferred_element_type=F32)


def _dot_nt(a, b):
    return lax.dot_general(a, b, (((1,), (1,)), ((), ())), preferred_element_type=F32)


def _mem_kv_kernel(mem_ref, g_ref, w_ref, kg_ref, k_ref, v_ref):
    h = _rms(mem_ref[...], g_ref[...]).astype(BF16)
    kv = _dot(h, w_ref[...])
    kg = kg_ref[...]
    for hd in range(XA_HEADS):
        sl = slice(hd * XA_HD, (hd + 1) * XA_HD)
        k_ref[:, sl] = _rms(kv[:, sl], kg).astype(BF16)
    v_ref[...] = kv[:, D_MODEL:].astype(BF16)


def _mem_kv(mem, mem_g, w_kv, k_g):
    bm = mem.shape[0]
    out = jax.ShapeDtypeStruct((DEPTH, bm, N_MEM, D_MODEL), BF16)
    return pl.pallas_call(
        _mem_kv_kernel,
        grid=(DEPTH, bm),
        in_specs=[
            pl.BlockSpec((None, N_MEM, D_MODEL), lambda l, b: (b, 0, 0)),
            pl.BlockSpec((None, 1, D_MODEL), lambda l, b: (l, 0, 0)),
            pl.BlockSpec((None, D_MODEL, 2 * D_MODEL), lambda l, b: (l, 0, 0)),
            pl.BlockSpec((None, 1, XA_HD), lambda l, b: (l, 0, 0)),
        ],
        out_specs=[
            pl.BlockSpec((None, None, N_MEM, D_MODEL), lambda l, b: (l, b, 0, 0)),
            pl.BlockSpec((None, None, N_MEM, D_MODEL), lambda l, b: (l, b, 0, 0)),
        ],
        out_shape=[out, out],
        compiler_params=_params(("arbitrary", "arbitrary")),
        name="mem_kv",
    )(mem, mem_g[:, None, :], w_kv.astype(BF16), k_g[:, None, :])


def _mla_pre_kernel(x_ref, ng_ref, wd_ref, qlg_ref, kvlg_ref, wuq_ref, wukv_ref,
                    gq_ref, gk_ref, qshift_ref, cos_ref, sin_ref, q_ref, k_ref, vt_ref):
    h = _rms(x_ref[...], ng_ref[...]).astype(BF16)
    down = _dot(h, wd_ref[...])
    c_q = _rms(down[:, :MLA_Q_LORA], qlg_ref[...]).astype(BF16)
    kv0 = MLA_Q_LORA
    c_kv = _rms(down[:, kv0:kv0 + MLA_KV_LORA], kvlg_ref[...]).astype(BF16)
    kr = down[:, kv0 + MLA_KV_LORA:kv0 + MLA_KV_LORA + LANE]
    kr_rot = down[:, kv0 + MLA_KV_LORA + LANE:]
    qq = _dot(c_q, wuq_ref[...])
    kvp = _dot(c_kv, wukv_ref[...])
    cos = cos_ref[...]
    sin = sin_ref[...]
    gq = gq_ref[...]
    gk = gk_ref[...]
    hw = MLA_HEADS * LANE
    lane = lax.broadcasted_iota(jnp.int32, (1, LANE), 1)
    one_col = (lane == MLA_V).astype(F32)
    k_one = (lane == SHIFT_LANE).astype(F32)
    q_shift = qshift_ref[...]
    q_scale = (MLA_QK ** -0.5) * LOG2E
    for hd in range(MLA_HEADS):
        sl = slice(hd * LANE, (hd + 1) * LANE)
        qh = qq[:, sl]
        rq = lax.rsqrt(jnp.sum(qh * qh, axis=-1, keepdims=True) * (1.0 / MLA_QK) + EPS)
        q_rot = qq[:, hw + hd * LANE:hw + (hd + 1) * LANE]
        q_ref[:, sl] = ((rq * q_scale) * (qh * gq * cos + q_rot * sin) + q_shift).astype(BF16)
        kh = kvp[:, sl] + kr
        rk = lax.rsqrt(jnp.sum(kh * kh, axis=-1, keepdims=True) * (1.0 / MLA_QK) + EPS)
        k_ref[:, sl] = (rk * (kh * gk * cos + kr_rot * sin) + k_one).astype(BF16)
        v_t = (kvp[:, hw + hd * LANE:hw + (hd + 1) * LANE] + one_col).T
        vt_ref[hd] = v_t[:MLA_VT_ROWS].astype(BF16)


def _rot_half_cols(w):
    half = MLA_ROPE // 2
    return jnp.concatenate([-w[..., half:], w[..., :half]], axis=-1)


def _mla_weights(norm_g, w_down, q_lat_g, kv_lat_g, w_uq, w_ukv, q_g, k_g, w_o):
    nq, nkv = MLA_Q_LORA, MLA_KV_LORA
    w_dq, w_dkv, w_kr = w_down[:, :nq], w_down[:, nq:nq + nkv], w_down[:, nq + nkv:]
    pad_lo = jnp.zeros((D_MODEL, MLA_NOPE), F32)
    pad_hi = jnp.zeros((D_MODEL, LANE - MLA_QK), F32)
    kr_blk = jnp.concatenate([pad_lo, w_kr, pad_hi], axis=1)
    krot_blk = jnp.concatenate([pad_lo, _rot_half_cols(w_kr * k_g[MLA_NOPE:]), pad_hi], axis=1)
    wd = jnp.concatenate([w_dq, w_dkv, kr_blk, krot_blk], axis=1).astype(BF16)

    wq = w_uq.reshape(nq, MLA_HEADS, MLA_QK)
    zq = jnp.zeros((nq, MLA_HEADS, LANE - MLA_QK), F32)
    wq_main = jnp.concatenate([wq, zq], axis=-1).reshape(nq, MLA_HEADS * LANE)
    wq_rot = jnp.concatenate(
        [jnp.zeros((nq, MLA_HEADS, MLA_NOPE), F32),
         _rot_half_cols(wq[..., MLA_NOPE:] * q_g[MLA_NOPE:]), zq], axis=-1
    ).reshape(nq, MLA_HEADS * LANE)
    wuq = jnp.concatenate([wq_main, wq_rot], axis=1).astype(BF16)

    wkv = w_ukv.reshape(nkv, MLA_HEADS, MLA_NOPE + MLA_V)
    zk = jnp.zeros((nkv, MLA_HEADS, LANE - MLA_NOPE), F32)
    zv = jnp.zeros((nkv, MLA_HEADS, LANE - MLA_V), F32)
    wk = jnp.concatenate([wkv[..., :MLA_NOPE], zk], axis=-1).reshape(nkv, MLA_HEADS * LANE)
    wv = jnp.concatenate([wkv[..., MLA_NOPE:], zv], axis=-1).reshape(nkv, MLA_HEADS * LANE)
    wukv = jnp.concatenate([wk, wv], axis=1).astype(BF16)

    zg = jnp.zeros((LANE - MLA_QK,), F32)
    gq = jnp.concatenate([q_g, zg])[None, :]
    gk = jnp.concatenate([k_g, zg])[None, :]

    wo = w_o.reshape(MLA_HEADS, MLA_V, D_MODEL)
    wo = jnp.concatenate([wo, jnp.zeros((MLA_HEADS, LANE - MLA_V, D_MODEL), F32)], axis=1)
    wo = wo.reshape(MLA_HEADS * LANE, D_MODEL).astype(BF16)
    bound = _score_bound(MLA_QK, q_g, k_g)
    q_shift = -bound * (jnp.arange(LANE) == SHIFT_LANE).astype(F32)[None, :]
    weights = (norm_g[None, :], wd, q_lat_g[None, :], kv_lat_g[None, :], wuq, wukv, gq, gk, q_shift)
    return weights, wo, bound


def _score_bound(dim, q_g, k_g):
    return SCORE_MARGIN * math.sqrt(dim) * LOG2E * jnp.max(jnp.abs(q_g)) * jnp.max(jnp.abs(k_g))


def _rope_tables(seq):
    inv = ROPE_THETA ** (-jnp.arange(0, MLA_ROPE, 2, dtype=F32) / MLA_ROPE)
    ang = jnp.arange(seq, dtype=F32)[:, None] * inv[None, :]
    ang = jnp.concatenate([ang, ang], axis=-1)
    ones = jnp.ones((seq, MLA_NOPE), F32)
    zlo = jnp.zeros((seq, MLA_NOPE), F32)
    zhi = jnp.zeros((seq, LANE - MLA_QK), F32)
    cos = jnp.concatenate([ones, jnp.cos(ang), zhi], axis=1)
    sin = jnp.concatenate([zlo, jnp.sin(ang), zhi], axis=1)
    return cos, sin


def _vt_shape_and_spec(b, heads, rows, s):
    shape = jax.ShapeDtypeStruct((b, heads, s // KV_CHUNK, rows, KV_CHUNK), BF16)
    spec = pl.BlockSpec((None, heads, None, rows, KV_CHUNK), lambda bi, i: (bi, 0, i, 0, 0))
    return shape, spec


def _mla_pre(x, weights, cos, sin):
    b, s, _ = x.shape
    tm = KV_CHUNK
    hw = MLA_HEADS * LANE
    out = jax.ShapeDtypeStruct((b, s, hw), BF16)
    vt_shape, vt_spec = _vt_shape_and_spec(b, MLA_HEADS, MLA_VT_ROWS, s)
    tile = lambda bi, i: (bi, i, 0)
    w_specs = [_const_spec(w.shape) for w in weights]
    return pl.pallas_call(
        _mla_pre_kernel,
        grid=(b, s // tm),
        in_specs=[pl.BlockSpec((None, tm, D_MODEL), tile)] + w_specs + [
            pl.BlockSpec((tm, LANE), lambda bi, i: (i, 0)),
            pl.BlockSpec((tm, LANE), lambda bi, i: (i, 0)),
        ],
        out_specs=[pl.BlockSpec((None, tm, hw), tile)] * 2 + [vt_spec],
        out_shape=[out, out, vt_shape],
        compiler_params=_params(("parallel", "parallel")),
        name="mla_pre",
    )(x, *weights, cos, sin)


def _flash_core(scores, vt_ref, s_buf, mx_buf, p_buf, al_buf, m_sc, acc_sc, n_kv):
    assert n_kv >= 2 and n_kv % 2 == 0

    def issue_scores(j, slot):
        s = scores(j)
        s_buf[slot] = s
        mx_buf[slot] = jnp.max(s, axis=0, keepdims=True)

    def softmax(slot):
        m_old = m_sc[...]
        m_new = jnp.maximum(m_old, mx_buf[slot])
        p_buf[slot] = jnp.exp2(s_buf[slot] - m_new).astype(BF16)
        al_buf[slot] = jnp.exp2(m_old - m_new)
        m_sc[...] = m_new

    def values(j, slot):
        acc_sc[...] = al_buf[slot] * acc_sc[...] + _dot(vt_ref[j], p_buf[slot])

    m_sc[...] = jnp.full(m_sc.shape, NEG_BIG, F32)
    acc_sc[...] = jnp.zeros(acc_sc.shape, F32)
    issue_scores(0, 0)
    issue_scores(1, 1)
    softmax(0)

    def pair(i, carry):
        j = 2 * i + 1
        issue_scores(j + 1, 0)
        softmax(1)
        values(j - 1, 0)
        issue_scores(j + 2, 1)
        softmax(0)
        values(j, 1)
        return carry

    lax.fori_loop(0, (n_kv - 2) // 2, pair, 0)
    softmax(1)
    values(n_kv - 2, 0)
    values(n_kv - 1, 1)


def _flash_scratch(rows, nq):
    return [
        pltpu.VMEM((2, KV_CHUNK, nq), F32),
        pltpu.VMEM((2, 1, nq), F32),
        pltpu.VMEM((2, KV_CHUNK, nq), BF16),
        pltpu.VMEM((2, 1, nq), F32),
        pltpu.VMEM((1, nq), F32),
        pltpu.VMEM((rows, nq), F32),
    ]


def _key_chunk(k_ref, j):
    return k_ref[pl.ds(pl.multiple_of(j * KV_CHUNK, KV_CHUNK), KV_CHUNK), :]


def _mla_store(o_t, o_ref):
    o_ref[...] = jnp.concatenate([o_t, jnp.zeros((LANE - MLA_V, o_t.shape[1]), F32)], axis=0).T.astype(BF16)


def _mla_finalize(acc_sc, o_ref):
    acc = acc_sc[...]
    _mla_store(acc[:MLA_V] / acc[MLA_V:MLA_V + 1], o_ref)


def _flash_mla_kernel(q_ref, k_ref, vt_ref, o_ref, *scratch, n_kv):
    acc_sc = scratch[-1]
    _flash_core(lambda j: _dot_nt(_key_chunk(k_ref, j), q_ref[...]), vt_ref, *scratch, n_kv)
    _mla_finalize(acc_sc, o_ref)


def _flash_mla(q, k, vt, tq):
    b, s, hw = q.shape
    n_kv = s // KV_CHUNK
    kern = functools.partial(_flash_mla_kernel, n_kv=n_kv)
    return pl.pallas_call(
        kern,
        grid=(b, MLA_HEADS, s // tq),
        in_specs=[
            pl.BlockSpec((None, tq, LANE), lambda bi, h, i: (bi, i, h)),
            pl.BlockSpec((None, s, LANE), lambda bi, h, i: (bi, 0, h)),
            pl.BlockSpec((None, None, n_kv, MLA_VT_ROWS, KV_CHUNK), lambda bi, h, i: (bi, h, 0, 0, 0)),
        ],
        out_specs=pl.BlockSpec((None, tq, LANE), lambda bi, h, i: (bi, i, h)),
        out_shape=jax.ShapeDtypeStruct((b, s, hw), BF16),
        scratch_shapes=_flash_scratch(MLA_VT_ROWS, tq),
        compiler_params=_params(("parallel", "parallel", "arbitrary")),
        name="flash_mla",
    )(q, k, vt)


def _static_core(first_probs, probs, chunk_of, vt_ref, p_buf, acc_sc, n_kv):
    assert n_kv >= 2 and n_kv % 2 == 0

    def values(t, slot, first=False):
        pv = _dot(vt_ref[chunk_of(t)], p_buf[slot])
        acc_sc[...] = pv if first else acc_sc[...] + pv

    p_buf[0] = first_probs()
    p_buf[1] = probs(1)
    values(0, 0, first=True)

    for t in range(1, n_kv - 1, 2):
        p_buf[0] = probs(t + 1)
        values(t, 1)
        p_buf[1] = probs(t + 2)
        values(t + 1, 0)
    values(n_kv - 1, 1)


def _static_scratch(rows, nq):
    return [pltpu.VMEM((2, KV_CHUNK, nq), BF16), pltpu.VMEM((rows, nq), F32)]


def _query_tile_loop(n_tiles, tile_fn):
    def body(i, carry):
        tile_fn(i)
        return carry
    lax.fori_loop(0, n_tiles, body, 0)


def _flash_mla_static_kernel(q_ref, k_ref, vt_ref, o_ref, p_buf, acc_sc, *, tq, n_kv):
    def tile(i):
        rows = pl.ds(pl.multiple_of(i * tq, tq), tq)
        probs = lambda t: jnp.exp2(_dot_nt(_key_chunk(k_ref, t), q_ref[rows, :])).astype(BF16)
        _static_core(lambda: probs(0), probs, lambda t: t, vt_ref, p_buf, acc_sc, n_kv)
        _mla_finalize(acc_sc, o_ref.at[rows, :])

    _query_tile_loop(q_ref.shape[0] // tq, tile)


def _flash_mla_static(q, k, vt, tq):
    b, s, hw = q.shape
    n_kv = s // KV_CHUNK
    kern = functools.partial(_flash_mla_static_kernel, tq=tq, n_kv=n_kv)
    head_slab = pl.BlockSpec((None, s, LANE), lambda bi, h: (bi, 0, h))
    return pl.pallas_call(
        kern,
        grid=(b, MLA_HEADS),
        in_specs=[
            head_slab,
            head_slab,
            pl.BlockSpec((None, None, n_kv, MLA_VT_ROWS, KV_CHUNK), lambda bi, h: (bi, h, 0, 0, 0)),
        ],
        out_specs=head_slab,
        out_shape=jax.ShapeDtypeStruct((b, s, hw), BF16),
        scratch_shapes=_static_scratch(MLA_VT_ROWS, tq),
        compiler_params=_params(("parallel", "parallel")),
        name="flash_mla_static",
    )(q, k, vt)


def _pos_lanes(pos0, rows, first_lane):
    pos = pos0 + lax.broadcasted_iota(jnp.int32, (rows, LANE), 0)
    lane = lax.broadcasted_iota(jnp.int32, (rows, LANE), 1) - first_lane
    hi = lax.shift_right_logical(pos, int(math.log2(POS_SPLIT))).astype(F32)
    lo = (pos & (POS_SPLIT - 1)).astype(F32)
    p = DIFF_AUG_PARTS
    in_hi = (lane >= 0) & (lane < p)
    in_lo = (lane >= p) & (lane < 2 * p)
    return jnp.where(in_hi, hi, jnp.where(in_lo, lo, 0.0))


def _diff_pre_kernel(x_ref, ng_ref, w_ref, gq_ref, gk_ref, kaug_ref, q_ref, k_ref, vt_ref):
    h = _rms(x_ref[...], ng_ref[...]).astype(BF16)
    qkv = _dot(h, w_ref[...])
    hw = DIFF_HEADS * LANE
    lo = lax.broadcasted_iota(jnp.int32, (1, LANE), 1) < DIFF_HD
    q_scale = (DIFF_HD ** -0.5) * LOG2E
    tm = qkv.shape[0]
    k_pos = _pos_lanes(pl.program_id(1) * tm, tm, 0)
    ones_rows = (lax.broadcasted_iota(jnp.int32, (DIFF_VT_ROWS - LANE, tm), 0) == 0).astype(F32)

    def half_norm(t, g):
        t2 = t * t
        ss_lo = jnp.sum(jnp.where(lo, t2, 0.0), axis=-1, keepdims=True)
        ss_hi = jnp.sum(jnp.where(lo, 0.0, t2), axis=-1, keepdims=True)
        r = lax.rsqrt(jnp.where(lo, ss_lo, ss_hi) * (1.0 / DIFF_HD) + EPS)
        return t * r * g

    for hd in range(DIFF_HEADS):
        sl = slice(hd * LANE, (hd + 1) * LANE)
        q_ref[:, sl] = (half_norm(qkv[:, sl], gq_ref[...]) * q_scale).astype(BF16)
        ks = slice(2 * hd * LANE, (2 * hd + 1) * LANE)
        k_ref[:, ks] = half_norm(qkv[:, hw + hd * LANE:hw + (hd + 1) * LANE], gk_ref[...]).astype(BF16)
        k_ref[:, (2 * hd + 1) * LANE:(2 * hd + 2) * LANE] = (k_pos + kaug_ref[hd]).astype(BF16)
        v_t = qkv[:, 2 * hw + hd * LANE:2 * hw + (hd + 1) * LANE].T
        vt_ref[hd] = jnp.concatenate([v_t, ones_rows], axis=0).astype(BF16)


def _slope_pieces():
    slopes = 2.0 ** (-8.0 * jnp.arange(1, DIFF_HEADS + 1, dtype=F32) / DIFF_HEADS) * LOG2E
    pieces, rest = [], slopes
    for _ in range(DIFF_AUG_PARTS):
        piece = rest.astype(BF16).astype(F32)
        pieces.append(piece)
        rest = rest - piece
    return jnp.stack(pieces, axis=1)


def _diff_bias_tables(bound):
    p = DIFF_AUG_PARTS
    pieces = _slope_pieces()
    zeros = jnp.zeros((DIFF_HEADS, LANE - 4 * p - 1), F32)
    one = jnp.ones((DIFF_HEADS, 1), F32)
    zp = jnp.zeros((DIFF_HEADS, p), F32)
    k_tab = jnp.concatenate([zp, zp, POS_SPLIT * pieces, pieces, one, zeros], axis=1)[:, None, :]
    shift = -bound * one
    after = jnp.concatenate([-POS_SPLIT * pieces, -pieces, zp, zp, shift, zeros], axis=1)
    before = jnp.concatenate([POS_SPLIT * pieces, pieces, zp, zp, shift, zeros], axis=1)
    diag = jnp.concatenate([zp, zp, zp, zp, shift, zeros], axis=1)
    return k_tab, jnp.stack([after, before, diag], axis=1)


def _diff_pre(x, norm_g, w_qkv, q_g, k_g, k_tab):
    b, s, _ = x.shape
    tm = KV_CHUNK
    hw = DIFF_HEADS * LANE
    out = jax.ShapeDtypeStruct((b, s, hw), BF16)
    out_k = jax.ShapeDtypeStruct((b, s, 2 * hw), BF16)
    vt_shape, vt_spec = _vt_shape_and_spec(b, DIFF_HEADS, DIFF_VT_ROWS, s)
    tile = lambda bi, i: (bi, i, 0)
    weights = (norm_g[None, :], w_qkv.astype(BF16), q_g.reshape(1, LANE), k_g.reshape(1, LANE), k_tab)
    return pl.pallas_call(
        _diff_pre_kernel,
        grid=(b, s // tm),
        in_specs=[pl.BlockSpec((None, tm, D_MODEL), tile)] + [_const_spec(w.shape) for w in weights],
        out_specs=[pl.BlockSpec((None, tm, hw), tile), pl.BlockSpec((None, tm, 2 * hw), tile), vt_spec],
        out_shape=[out, out_k, vt_shape],
        compiler_params=_params(("parallel", "parallel")),
        name="diff_pre",
    )(x, *weights)


def _flash_diff_kernel(slope_ref, q_ref, k_ref, vt_ref, dmat_ref, lam_ref, subg_ref, o_ref,
                       qcat_sc, *scratch, tq, n_kv, lambda_init):
    acc_sc = scratch[-1]
    lo = lax.broadcasted_iota(jnp.int32, (1, LANE), 1) < DIFF_HD
    q = q_ref[...]
    zero = jnp.zeros_like(q)
    qcat_sc[:tq] = jnp.where(lo, q, zero)
    qcat_sc[tq:] = jnp.where(lo, zero, q)
    neg_slope = slope_ref[pl.program_id(1)]
    q0 = pl.program_id(2) * tq

    def scores(j):
        delta = (j * KV_CHUNK - q0).astype(F32)
        bias = jnp.abs(dmat_ref[...] + delta) * neg_slope
        return _dot_nt(_key_chunk(k_ref, j), qcat_sc[...]) + jnp.concatenate([bias, bias], axis=1)

    _flash_core(scores, vt_ref, *scratch, n_kv)
    _diff_finalize(acc_sc, lam_ref, subg_ref, o_ref, tq, lambda_init)


def _diff_finalize(acc_sc, lam_ref, subg_ref, o_ref, tq, lambda_init):
    acc = acc_sc[...]
    _diff_store(acc[:LANE] / acc[LANE:LANE + 1], lam_ref, subg_ref, o_ref, tq, lambda_init)


def _diff_store(o, lam_ref, subg_ref, o_ref, tq, lambda_init):
    lp = lam_ref[...]
    lam = (jnp.exp(jnp.sum(lp[0:1] * lp[1:2], axis=-1, keepdims=True))
           - jnp.exp(jnp.sum(lp[2:3] * lp[3:4], axis=-1, keepdims=True)) + lambda_init)
    o = (o[:, :tq] - lam * o[:, tq:]).T
    o = _rms(o, subg_ref[...]) * (1.0 - lambda_init)
    o_ref[...] = o.astype(BF16)


def _flash_diff_static_kernel(slope_ref, q_ref, k_ref, vt_ref, dmat_ref, lam_ref, subg_ref, qtab_ref,
                              o_ref, qz_sc, aug_sc, p_buf, acc_sc, *, tq, n_kv, lambda_init):
    tile = functools.partial(_flash_diff_static_tile, slope_ref, q_ref, k_ref, vt_ref, dmat_ref, lam_ref,
                             subg_ref, qtab_ref, o_ref, qz_sc, aug_sc, p_buf, acc_sc, tq, n_kv,
                             lambda_init)
    _query_tile_loop(q_ref.shape[0] // tq, tile)


def _flash_diff_static_tile(slope_ref, q_ref, k_ref, vt_ref, dmat_ref, lam_ref, subg_ref, qtab_ref,
                            o_ref, qz_sc, aug_sc, p_buf, acc_sc, tq, n_kv, lambda_init, i):
    lo = lax.broadcasted_iota(jnp.int32, (1, LANE), 1) < DIFF_HD
    q0 = pl.multiple_of(i * tq, tq)
    q = q_ref[pl.ds(q0, tq), :]
    zero = jnp.zeros_like(q)
    qz_sc[:tq] = jnp.where(lo, q, zero)
    qz_sc[tq:] = jnp.where(lo, zero, q)
    n_d = tq // KV_CHUNK
    q_pos = _pos_lanes(q0, tq, 2 * DIFF_AUG_PARTS)
    tab = qtab_ref[...]
    for kind, sign in enumerate((1.0, -1.0, 0.0)):
        aug_sc[kind] = (tab[kind:kind + 1] + sign * q_pos).astype(BF16)
    jd = lax.shift_right_logical(q0, int(math.log2(KV_CHUNK)))
    neg_slope = slope_ref[pl.program_id(1)]

    def with_bias_columns(aug):
        return jnp.concatenate([qz_sc[...], jnp.concatenate([aug, aug], axis=0)], axis=1)

    def diag_probs(d):
        kinds = [2 if e == d else (1 if e > d else 0) for e in range(n_d)]
        aug = jnp.concatenate([aug_sc[kind, e * KV_CHUNK:(e + 1) * KV_CHUNK, :]
                               for e, kind in enumerate(kinds)], axis=0)
        s = _dot_nt(_key_chunk(k_ref, jd + d), with_bias_columns(aug))
        bias = jnp.abs(dmat_ref[...]) * neg_slope
        cols = []
        for c in range(2):
            for e in range(n_d):
                blk = s[:, c * tq + e * KV_CHUNK:c * tq + (e + 1) * KV_CHUNK]
                cols.append(blk + bias if e == d else blk)
        return jnp.exp2(jnp.concatenate(cols, axis=1)).astype(BF16)

    def chunk_of(t):
        r = t - n_d
        return jnp.where(t < n_d, jd + t, r + n_d * (r >= jd).astype(jnp.int32))

    def probs(t):
        if isinstance(t, int) and t < n_d:
            return diag_probs(t)
        j = chunk_of(t)
        before = (j < jd).astype(jnp.int32)
        return jnp.exp2(_dot_nt(_key_chunk(k_ref, j), with_bias_columns(aug_sc[before]))).astype(BF16)

    _static_core(lambda: diag_probs(0), probs, chunk_of, vt_ref, p_buf, acc_sc, n_kv)
    _diff_finalize(acc_sc, lam_ref, subg_ref, o_ref.at[pl.ds(q0, tq), :], tq, lambda_init)


def _flash_diff(q, k, vt, lam_p, sub_g, q_tab, lambda_init, tq, static_shift):
    b, s, hw = q.shape
    n_kv = s // KV_CHUNK
    slopes = 2.0 ** (-8.0 * jnp.arange(1, DIFF_HEADS + 1, dtype=F32) / DIFF_HEADS)
    neg_slopes = -slopes * LOG2E
    d_cols = KV_CHUNK if static_shift else tq
    dmat = (jnp.arange(KV_CHUNK, dtype=F32)[:, None] - jnp.arange(d_cols, dtype=F32)[None, :])
    const = dict(pipeline_mode=pl.Buffered(1))
    origin = lambda *_: (0, 0)
    vt_spec = pl.BlockSpec((None, None, n_kv, DIFF_VT_ROWS, KV_CHUNK), lambda bi, h, *_: (bi, h, 0, 0, 0))
    small_specs = [
        pl.BlockSpec((KV_CHUNK, d_cols), origin, **const),
        pl.BlockSpec((4, DIFF_HD), origin, **const),
        pl.BlockSpec((1, LANE), origin, **const),
    ]
    args = [neg_slopes, q, k, vt, dmat, lam_p, sub_g[None, :]]
    if static_shift:
        assert tq % KV_CHUNK == 0
        kern = functools.partial(_flash_diff_static_kernel, tq=tq, n_kv=n_kv, lambda_init=lambda_init)
        grid = (b, DIFF_HEADS)
        q_spec = pl.BlockSpec((None, s, LANE), lambda bi, h, sl: (bi, 0, h))
        k_spec = pl.BlockSpec((None, s, 2 * LANE), lambda bi, h, sl: (bi, 0, h))
        in_specs = [q_spec, k_spec, vt_spec] + small_specs + [
            pl.BlockSpec((None, 3, LANE), lambda bi, h, sl: (h, 0, 0))]
        args.append(q_tab)
        scratch = ([pltpu.VMEM((2 * tq, LANE), BF16), pltpu.VMEM((3, tq, LANE), BF16)]
                   + _static_scratch(DIFF_VT_ROWS, 2 * tq))
        name = "flash_diff_static"
    else:
        kern = functools.partial(_flash_diff_kernel, tq=tq, n_kv=n_kv, lambda_init=lambda_init)
        grid = (b, DIFF_HEADS, s // tq)
        q_spec = pl.BlockSpec((None, tq, LANE), lambda bi, h, i, sl: (bi, i, h))
        k_spec = pl.BlockSpec((None, s, LANE), lambda bi, h, i, sl: (bi, 0, 2 * h))
        in_specs = [q_spec, k_spec, vt_spec] + small_specs
        scratch = [pltpu.VMEM((2 * tq, LANE), BF16)] + _flash_scratch(DIFF_VT_ROWS, 2 * tq)
        name = "flash_diff"
    grid_spec = pltpu.PrefetchScalarGridSpec(
        num_scalar_prefetch=1, grid=grid, in_specs=in_specs, out_specs=q_spec, scratch_shapes=scratch)
    return pl.pallas_call(
        kern,
        grid_spec=grid_spec,
        out_shape=jax.ShapeDtypeStruct((b, s, hw), BF16),
        compiler_params=_params(("parallel", "parallel") + (("arbitrary",) if len(grid) == 3 else ())),
        name=name,
    )(*args)


def _post_xattn_kernel(x_ref, om_ref, wom_ref, ng_ref, wq_ref, qg_ref, k_ref, v_ref, wo_ref,
                       y_ref, o_sc):
    x1 = x_ref[...] + _dot(om_ref[...], wom_ref[...])
    h = _rms(x1, ng_ref[...]).astype(BF16)
    q = _dot(h, wq_ref[...])
    qg = qg_ref[...] * ((XA_HD ** -0.5) * LOG2E)
    for hd in range(XA_HEADS):
        sl = slice(hd * XA_HD, (hd + 1) * XA_HD)
        qh = _rms(q[:, sl], qg).astype(BF16)
        s = _dot_nt(qh, k_ref[:, sl])
        p = jnp.exp2(s - jnp.max(s, axis=-1, keepdims=True))
        l = jnp.sum(p, axis=-1, keepdims=True)
        o_sc[:, sl] = (_dot(p.astype(BF16), v_ref[:, sl]) / l).astype(BF16)
    y_ref[...] = x1 + _dot(o_sc[...], wo_ref[...])


def _post_xattn(x, o_mix, w_o_mix, norm_g, w_q, q_g, k_mem, v_mem, layer, w_o, tm):
    b, s, _ = x.shape
    tile = lambda bi, i: (bi, i, 0)
    mem_spec = pl.BlockSpec((None, None, N_MEM, D_MODEL), lambda bi, i: (layer, bi, 0, 0))
    sq = (D_MODEL, D_MODEL)
    return pl.pallas_call(
        _post_xattn_kernel,
        grid=(b, s // tm),
        in_specs=[
            pl.BlockSpec((None, tm, D_MODEL), tile),
            pl.BlockSpec((None, tm, o_mix.shape[-1]), tile),
            _const_spec(w_o_mix.shape),
            _const_spec((1, D_MODEL)),
            _const_spec(sq),
            _const_spec((1, XA_HD)),
            mem_spec,
            mem_spec,
            _const_spec(sq),
        ],
        out_specs=pl.BlockSpec((None, tm, D_MODEL), tile),
        out_shape=jax.ShapeDtypeStruct(x.shape, F32),
        scratch_shapes=[pltpu.VMEM((tm, D_MODEL), BF16)],
        compiler_params=_params(("parallel", "parallel")),
        name="post_xattn",
    )(x, o_mix, w_o_mix, norm_g[None, :], w_q.astype(BF16), q_g[None, :], k_mem, v_mem,
      w_o.astype(BF16))


def _ffn_kernel(x_ref, prev_ref, next_ref, ng_ref, wg_ref, wu_ref, cw_ref, cb_ref, wd_ref, y_ref,
                *, tm, n_chunks):
    i = pl.program_id(1)
    keep_prev = (i > 0).astype(F32)
    keep_next = (i < pl.num_programs(1) - 1).astype(F32)
    x = x_ref[...]
    xe = jnp.concatenate([prev_ref[...] * keep_prev, x, next_ref[...] * keep_next], axis=0)
    he = _rms(xe, ng_ref[...]).astype(BF16)
    hc = he[HALO:HALO + tm]
    cw = cw_ref[...]
    cb = cb_ref[...]
    y = x
    n_tiles = D_FF // MXU_DIM
    bounds = [MXU_DIM * ((n_tiles * c + n_chunks - 1) // n_chunks) for c in range(n_chunks)] + [D_FF]
    for c in range(n_chunks):
        cs = slice(bounds[c], bounds[c + 1])
        ge = _dot(he, wg_ref[:, cs])
        u = _dot(hc, wu_ref[:, cs])
        g = (ge[HALO - 1:HALO - 1 + tm] * cw[0:1, cs] + ge[HALO:HALO + tm] * cw[1:2, cs]
             + ge[HALO + 1:HALO + 1 + tm] * cw[2:3, cs] + cb[:, cs])
        act = (g * jax.nn.sigmoid(g) * u).astype(BF16)
        y = y + _dot(act, wd_ref[cs, :])
    y_ref[...] = y


def _ffn(x, norm_g, w_gu, conv_w, conv_b, w_down, tm, n_chunks=1):
    b, s, _ = x.shape
    nh = tm // HALO
    last = s // HALO - 1
    w_g = w_gu[:, :D_FF].astype(BF16)
    w_u = w_gu[:, D_FF:].astype(BF16)
    weights = (norm_g[None, :], w_g, w_u, conv_w, conv_b[None, :], w_down.astype(BF16))
    kern = functools.partial(_ffn_kernel, tm=tm, n_chunks=n_chunks)
    return pl.pallas_call(
        kern,
        grid=(b, s // tm),
        in_specs=[
            pl.BlockSpec((None, tm, D_MODEL), lambda bi, i: (bi, i, 0)),
            pl.BlockSpec((None, HALO, D_MODEL), lambda bi, i: (bi, jnp.maximum(i * nh - 1, 0), 0)),
            pl.BlockSpec((None, HALO, D_MODEL), lambda bi, i: (bi, jnp.minimum((i + 1) * nh, last), 0)),
        ] + [_const_spec(w.shape) for w in weights],
        out_specs=pl.BlockSpec((None, tm, D_MODEL), lambda bi, i: (bi, i, 0)),
        out_shape=jax.ShapeDtypeStruct(x.shape, F32),
        compiler_params=_params(("parallel", "parallel")),
        name="ffn",
    )(x, x, x, *weights)


def _tiles(s):
    return dict(tm=min(512, s), tq_mla=min(4096, s), tq_diff=min(2048, s),
                tq_mla_online=min(1024, s), tq_diff_online=min(512, s))


def _mixer_weights(mla_p, diff_p):
    layers = []
    for i in range(DEPTH):
        j = i // 2
        if i % 2 == 0:
            weights, w_o_mix, bound = _mla_weights(*[p[j] for p in mla_p])
            layers.append(dict(weights=weights, w_o_mix=w_o_mix, bound=bound))
        else:
            norm_g, w_qkv, q_g, k_g, lam_p, sub_g, w_o = [p[j] for p in diff_p]
            bound = _score_bound(DIFF_HD, q_g, k_g)
            k_tab, q_tab = _diff_bias_tables(bound)
            layers.append(dict(pre=(norm_g, w_qkv, q_g, k_g, k_tab), lam_p=lam_p, sub_g=sub_g,
                               q_tab=q_tab, w_o_mix=w_o.astype(BF16), bound=bound))
    return layers


def _trunk(x, mem, mixers, xa_p, ffn_p):
    s = x.shape[1]
    assert s % KV_CHUNK == 0
    t = _tiles(s)
    xa_norm, xa_mem_norm, xa_w_q, xa_w_kv, xa_q_norm, xa_k_norm, xa_w_o = xa_p
    k_mem, v_mem = _mem_kv(mem, xa_mem_norm, xa_w_kv, xa_k_norm)
    cos, sin = _rope_tables(s)
    for i in range(DEPTH):
        mx = mixers[i]
        static_ok = mx["bound"] <= MAX_STATIC_BOUND
        if i % 2 == 0:
            q, k, vt = _mla_pre(x, mx["weights"], cos, sin)
            o_mix = lax.cond(static_ok,
                             lambda q, k, vt: _flash_mla_static(q, k, vt, t["tq_mla"]),
                             lambda q, k, vt: _flash_mla(q, k, vt, t["tq_mla_online"]), q, k, vt)
        else:
            lambda_init = 0.8 - 0.6 * math.exp(-0.3 * i)
            q, k, vt = _diff_pre(x, *mx["pre"])
            flash = lambda static: functools.partial(
                _flash_diff, lambda_init=lambda_init, static_shift=static,
                tq=t["tq_diff"] if static else t["tq_diff_online"])
            o_mix = lax.cond(static_ok, flash(True), flash(False),
                             q, k, vt, mx["lam_p"], mx["sub_g"], mx["q_tab"])
        x = _post_xattn(x, o_mix, mx["w_o_mix"], xa_norm[i], xa_w_q[i], xa_q_norm[i], k_mem, v_mem, i,
                        xa_w_o[i], t["tm"])
        x = _ffn(x, *[p[i] for p in ffn_p], t["tm"])
    return x


def kernel(x_prompt, x_sample, mem_prompt, mem_sample, mla_norm, mla_w_down, mla_q_lat_norm, mla_kv_lat_norm, mla_w_uq, mla_w_ukv, mla_q_norm, mla_k_norm, mla_w_o, diff_norm, diff_w_qkv, diff_q_norm, diff_k_norm, diff_lambda, diff_sub_norm, diff_w_o, xa_norm, xa_mem_norm, xa_w_q, xa_w_kv, xa_q_norm, xa_k_norm, xa_w_o, ffn_norm, ffn_w_gu, ffn_conv_w, ffn_conv_b, ffn_w_down):
    mla_p = (mla_norm, mla_w_down, mla_q_lat_norm, mla_kv_lat_norm, mla_w_uq, mla_w_ukv,
             mla_q_norm, mla_k_norm, mla_w_o)
    diff_p = (diff_norm, diff_w_qkv, diff_q_norm, diff_k_norm, diff_lambda, diff_sub_norm, diff_w_o)
    xa_p = (xa_norm, xa_mem_norm, xa_w_q, xa_w_kv, xa_q_norm, xa_k_norm, xa_w_o)
    ffn_p = (ffn_norm, ffn_w_gu, ffn_conv_w, ffn_conv_b, ffn_w_down)
    mixers = _mixer_weights(mla_p, diff_p)
    y_prompt = _trunk(x_prompt, mem_prompt, mixers, xa_p, ffn_p)
    y_sample = _trunk(x_sample, mem_sample, mixers, xa_p, ffn_p)
    return (y_prompt, y_sample)
```

```python
import functools
import math

import jax
import jax.numpy as jnp
from jax import lax
from jax.experimental import pallas as pl
from jax.experimental.pallas import tpu as pltpu

D_MODEL = 1024
DEPTH = 4
N_MEM = 256
EPS = 1e-6
MLA_HEADS = 8
MLA_Q_LORA = 384
MLA_KV_LORA = 256
MLA_NOPE = 64
MLA_ROPE = 32
MLA_QK = MLA_NOPE + MLA_ROPE
MLA_V = 64
ROPE_THETA = 10000.0
DIFF_HEADS = 8
DIFF_HD = 64
XA_HEADS = 4
XA_HD = D_MODEL // XA_HEADS
D_FF = 2816
CONV_W = 3

LANE = 128
HALO = 8
BF16_ROWS = 16
MXU_DIM = 256
KV_CHUNK = 512
DIFF_VT_ROWS = LANE + BF16_ROWS
MLA_VT_ROWS = MLA_V + BF16_ROWS
SHIFT_LANE = MLA_QK
SCORE_MARGIN = 1.02
MAX_STATIC_BOUND = 40.0
DIFF_AUG_PARTS = 3
POS_SPLIT = 64
LOG2E = math.log2(math.e)
NEG_BIG = -1e30
VMEM_LIMIT = 56 * 1024 * 1024

F32 = jnp.float32
BF16 = jnp.bfloat16


def _const_spec(shape):
    nd = len(shape)
    return pl.BlockSpec(shape, lambda *_: (0,) * nd, pipeline_mode=pl.Buffered(1))


def _params(sem):
    return pltpu.CompilerParams(dimension_semantics=sem, vmem_limit_bytes=VMEM_LIMIT)


def _rms(x, g):
    return x * lax.rsqrt(jnp.mean(x * x, axis=-1, keepdims=True) + EPS) * g


def _dot(a, b):
    return jnp.dot(a, b, preferred_element_type=F32)


def _dot_nt(a, b):
    return lax.dot_general(a, b, (((1,), (1,)), ((), ())), preferred_element_type=F32)


def _mem_kv_kernel(mem_ref, g_ref, w_ref, kg_ref, k_ref, v_ref):
    h = _rms(mem_ref[...], g_ref[...]).astype(BF16)
    kv = _dot(h, w_ref[...])
    kg = kg_ref[...]
    for hd in range(XA_HEADS):
        sl = slice(hd * XA_HD, (hd + 1) * XA_HD)
        k_ref[:, sl] = _rms(kv[:, sl], kg).astype(BF16)
    v_ref[...] = kv[:, D_MODEL:].astype(BF16)


def _mem_kv(mem, mem_g, w_kv, k_g):
    bm = mem.shape[0]
    out = jax.ShapeDtypeStruct((DEPTH, bm, N_MEM, D_MODEL), BF16)
    return pl.pallas_call(
        _mem_kv_kernel,
        grid=(DEPTH, bm),
        in_specs=[
            pl.BlockSpec((None, N_MEM, D_MODEL), lambda l, b: (b, 0, 0)),
            pl.BlockSpec((None, 1, D_MODEL), lambda l, b: (l, 0, 0)),
            pl.BlockSpec((None, D_MODEL, 2 * D_MODEL), lambda l, b: (l, 0, 0)),
            pl.BlockSpec((None, 1, XA_HD), lambda l, b: (l, 0, 0)),
        ],
        out_specs=[
            pl.BlockSpec((None, None, N_MEM, D_MODEL), lambda l, b: (l, b, 0, 0)),
            pl.BlockSpec((None, None, N_MEM, D_MODEL), lambda l, b: (l, b, 0, 0)),
        ],
        out_shape=[out, out],
        compiler_params=_params(("arbitrary", "arbitrary")),
        name="mem_kv",
    )(mem, mem_g[:, None, :], w_kv.astype(BF16), k_g[:, None, :])


def _mla_pre_kernel(x_ref, ng_ref, wd_ref, qlg_ref, kvlg_ref, wuq_ref, wuk_ref, wvt_ref,
                    gq_ref, gk_ref, qshift_ref, cos_ref, sin_ref, q_ref, k_ref, vt_ref):
    h = _rms(x_ref[...], ng_ref[...]).astype(BF16)
    down = _dot(h, wd_ref[...])
    c_q = _rms(down[:, :MLA_Q_LORA], qlg_ref[...]).astype(BF16)
    kv0 = MLA_Q_LORA
    c_kv = _rms(down[:, kv0:kv0 + MLA_KV_LORA], kvlg_ref[...]).astype(BF16)
    kr = down[:, kv0 + MLA_KV_LORA:kv0 + MLA_KV_LORA + LANE]
    kr_rot = down[:, kv0 + MLA_KV_LORA + LANE:]
    qq = _dot(c_q, wuq_ref[...])
    kn = _dot(c_kv, wuk_ref[...])
    v_t = _dot_nt(wvt_ref[...], c_kv)
    ones_row = (lax.broadcasted_iota(jnp.int32, (MLA_VT_ROWS, v_t.shape[1]), 0) == MLA_V).astype(F32)
    cos = cos_ref[...]
    sin = sin_ref[...]
    gq = gq_ref[...]
    gk = gk_ref[...]
    hw = MLA_HEADS * LANE
    lane = lax.broadcasted_iota(jnp.int32, (1, LANE), 1)
    k_one = (lane == SHIFT_LANE).astype(F32)
    q_shift = qshift_ref[...]
    q_scale = (MLA_QK ** -0.5) * LOG2E
    for hd in range(MLA_HEADS):
        sl = slice(hd * LANE, (hd + 1) * LANE)
        qh = qq[:, sl]
        rq = lax.rsqrt(jnp.sum(qh * qh, axis=-1, keepdims=True) * (1.0 / MLA_QK) + EPS)
        q_rot = qq[:, hw + hd * LANE:hw + (hd + 1) * LANE]
        q_ref[:, sl] = ((rq * q_scale) * (qh * gq * cos + q_rot * sin) + q_shift).astype(BF16)
        kh = kn[:, sl] + kr
        rk = lax.rsqrt(jnp.sum(kh * kh, axis=-1, keepdims=True) * (1.0 / MLA_QK) + EPS)
        k_ref[:, sl] = (rk * (kh * gk * cos + kr_rot * sin) + k_one).astype(BF16)
        vt_ref[hd] = (v_t[hd * MLA_VT_ROWS:(hd + 1) * MLA_VT_ROWS] + ones_row).astype(BF16)


def _rot_half_cols(w):
    half = MLA_ROPE // 2
    return jnp.concatenate([-w[..., half:], w[..., :half]], axis=-1)


def _mla_weights(norm_g, w_down, q_lat_g, kv_lat_g, w_uq, w_ukv, q_g, k_g, w_o):
    nq, nkv = MLA_Q_LORA, MLA_KV_LORA
    w_dq, w_dkv, w_kr = w_down[:, :nq], w_down[:, nq:nq + nkv], w_down[:, nq + nkv:]
    pad_lo = jnp.zeros((D_MODEL, MLA_NOPE), F32)
    pad_hi = jnp.zeros((D_MODEL, LANE - MLA_QK), F32)
    kr_blk = jnp.concatenate([pad_lo, w_kr, pad_hi], axis=1)
    krot_blk = jnp.concatenate([pad_lo, _rot_half_cols(w_kr * k_g[MLA_NOPE:]), pad_hi], axis=1)
    wd = jnp.concatenate([w_dq, w_dkv, kr_blk, krot_blk], axis=1).astype(BF16)

    wq = w_uq.reshape(nq, MLA_HEADS, MLA_QK)
    zq = jnp.zeros((nq, MLA_HEADS, LANE - MLA_QK), F32)
    wq_main = jnp.concatenate([wq, zq], axis=-1).reshape(nq, MLA_HEADS * LANE)
    wq_rot = jnp.concatenate(
        [jnp.zeros((nq, MLA_HEADS, MLA_NOPE), F32),
         _rot_half_cols(wq[..., MLA_NOPE:] * q_g[MLA_NOPE:]), zq], axis=-1
    ).reshape(nq, MLA_HEADS * LANE)
    wuq = jnp.concatenate([wq_main, wq_rot], axis=1).astype(BF16)

    wkv = w_ukv.reshape(nkv, MLA_HEADS, MLA_NOPE + MLA_V)
    zk = jnp.zeros((nkv, MLA_HEADS, LANE - MLA_NOPE), F32)
    wuk = jnp.concatenate([wkv[..., :MLA_NOPE], zk], axis=-1).reshape(nkv, MLA_HEADS * LANE).astype(BF16)
    wvt = jnp.transpose(wkv[..., MLA_NOPE:], (1, 2, 0))
    wvt = jnp.concatenate([wvt, jnp.zeros((MLA_HEADS, MLA_VT_ROWS - MLA_V, nkv), F32)], axis=1)
    wvt = wvt.reshape(MLA_HEADS * MLA_VT_ROWS, nkv).astype(BF16)

    zg = jnp.zeros((LANE - MLA_QK,), F32)
    gq = jnp.concatenate([q_g, zg])[None, :]
    gk = jnp.concatenate([k_g, zg])[None, :]

    wo = w_o.reshape(MLA_HEADS, MLA_V, D_MODEL)
    wo = jnp.concatenate([wo, jnp.zeros((MLA_HEADS, LANE - MLA_V, D_MODEL), F32)], axis=1)
    wo = wo.reshape(MLA_HEADS * LANE, D_MODEL).astype(BF16)
    bound = _score_bound(MLA_QK, q_g, k_g)
    q_shift = -bound * (jnp.arange(LANE) == SHIFT_LANE).astype(F32)[None, :]
    weights = (norm_g[None, :], wd, q_lat_g[None, :], kv_lat_g[None, :], wuq, wuk, wvt, gq, gk, q_shift)
    return weights, wo, bound


def _score_bound(dim, q_g, k_g):
    return SCORE_MARGIN * math.sqrt(dim) * LOG2E * jnp.max(jnp.abs(q_g)) * jnp.max(jnp.abs(k_g))


def _rope_tables(seq):
    inv = ROPE_THETA ** (-jnp.arange(0, MLA_ROPE, 2, dtype=F32) / MLA_ROPE)
    ang = jnp.arange(seq, dtype=F32)[:, None] * inv[None, :]
    ang = jnp.concatenate([ang, ang], axis=-1)
    ones = jnp.ones((seq, MLA_NOPE), F32)
    zlo = jnp.zeros((seq, MLA_NOPE), F32)
    zhi = jnp.zeros((seq, LANE - MLA_QK), F32)
    cos = jnp.concatenate([ones, jnp.cos(ang), zhi], axis=1)
    sin = jnp.concatenate([zlo, jnp.sin(ang), zhi], axis=1)
    return cos, sin


def _vt_shape_and_spec(b, heads, rows, s):
    shape = jax.ShapeDtypeStruct((b, heads, s // KV_CHUNK, rows, KV_CHUNK), BF16)
    spec = pl.BlockSpec((None, heads, None, rows, KV_CHUNK), lambda bi, i: (bi, 0, i, 0, 0))
    return shape, spec


def _mla_pre(x, weights, cos, sin):
    b, s, _ = x.shape
    tm = KV_CHUNK
    hw = MLA_HEADS * LANE
    out = jax.ShapeDtypeStruct((b, s, hw), BF16)
    vt_shape, vt_spec = _vt_shape_and_spec(b, MLA_HEADS, MLA_VT_ROWS, s)
    tile = lambda bi, i: (bi, i, 0)
    w_specs = [_const_spec(w.shape) for w in weights]
    return pl.pallas_call(
        _mla_pre_kernel,
        grid=(b, s // tm),
        in_specs=[pl.BlockSpec((None, tm, D_MODEL), tile)] + w_specs + [
            pl.BlockSpec((tm, LANE), lambda bi, i: (i, 0)),
            pl.BlockSpec((tm, LANE), lambda bi, i: (i, 0)),
        ],
        out_specs=[pl.BlockSpec((None, tm, hw), tile)] * 2 + [vt_spec],
        out_shape=[out, out, vt_shape],
        compiler_params=_params(("parallel", "parallel")),
        name="mla_pre",
    )(x, *weights, cos, sin)


def _flash_core(scores, vt_ref, s_buf, mx_buf, p_buf, al_buf, m_sc, acc_sc, n_kv):
    assert n_kv >= 2 and n_kv % 2 == 0

    def issue_scores(j, slot):
        s = scores(j)
        s_buf[slot] = s
        mx_buf[slot] = jnp.max(s, axis=0, keepdims=True)

    def softmax(slot):
        m_old = m_sc[...]
        m_new = jnp.maximum(m_old, mx_buf[slot])
        p_buf[slot] = jnp.exp2(s_buf[slot] - m_new).astype(BF16)
        al_buf[slot] = jnp.exp2(m_old - m_new)
        m_sc[...] = m_new

    def values(j, slot):
        acc_sc[...] = al_buf[slot] * acc_sc[...] + _dot(vt_ref[j], p_buf[slot])

    m_sc[...] = jnp.full(m_sc.shape, NEG_BIG, F32)
    acc_sc[...] = jnp.zeros(acc_sc.shape, F32)
    issue_scores(0, 0)
    issue_scores(1, 1)
    softmax(0)

    def pair(i, carry):
        j = 2 * i + 1
        issue_scores(j + 1, 0)
        softmax(1)
        values(j - 1, 0)
        issue_scores(j + 2, 1)
        softmax(0)
        values(j, 1)
        return carry

    lax.fori_loop(0, (n_kv - 2) // 2, pair, 0)
    softmax(1)
    values(n_kv - 2, 0)
    values(n_kv - 1, 1)


def _flash_scratch(rows, nq):
    return [
        pltpu.VMEM((2, KV_CHUNK, nq), F32),
        pltpu.VMEM((2, 1, nq), F32),
        pltpu.VMEM((2, KV_CHUNK, nq), BF16),
        pltpu.VMEM((2, 1, nq), F32),
        pltpu.VMEM((1, nq), F32),
        pltpu.VMEM((rows, nq), F32),
    ]


def _key_chunk(k_ref, j):
    return k_ref[pl.ds(pl.multiple_of(j * KV_CHUNK, KV_CHUNK), KV_CHUNK), :]


def _mla_store(o_t, o_ref):
    o_ref[...] = jnp.concatenate([o_t, jnp.zeros((LANE - MLA_V, o_t.shape[1]), F32)], axis=0).T.astype(BF16)


def _mla_finalize(acc_sc, o_ref):
    acc = acc_sc[...]
    _mla_store(acc[:MLA_V] / acc[MLA_V:MLA_V + 1], o_ref)


def _flash_mla_kernel(q_ref, k_ref, vt_ref, o_ref, *scratch, n_kv):
    acc_sc = scratch[-1]
    _flash_core(lambda j: _dot_nt(_key_chunk(k_ref, j), q_ref[...]), vt_ref, *scratch, n_kv)
    _mla_finalize(acc_sc, o_ref)


def _flash_mla(q, k, vt, tq):
    b, s, hw = q.shape
    n_kv = s // KV_CHUNK
    kern = functools.partial(_flash_mla_kernel, n_kv=n_kv)
    return pl.pallas_call(
        kern,
        grid=(b, MLA_HEADS, s // tq),
        in_specs=[
            pl.BlockSpec((None, tq, LANE), lambda bi, h, i: (bi, i, h)),
            pl.BlockSpec((None, s, LANE), lambda bi, h, i: (bi, 0, h)),
            pl.BlockSpec((None, None, n_kv, MLA_VT_ROWS, KV_CHUNK), lambda bi, h, i: (bi, h, 0, 0, 0)),
        ],
        out_specs=pl.BlockSpec((None, tq, LANE), lambda bi, h, i: (bi, i, h)),
        out_shape=jax.ShapeDtypeStruct((b, s, hw), BF16),
        scratch_shapes=_flash_scratch(MLA_VT_ROWS, tq),
        compiler_params=_params(("parallel", "parallel", "arbitrary")),
        name="flash_mla",
    )(q, k, vt)


def _static_core(first_probs, probs, chunk_of, vt_ref, p_buf, acc_sc, n_kv):
    assert n_kv >= 2 and n_kv % 2 == 0

    def values(t, slot, first=False):
        pv = _dot(vt_ref[chunk_of(t)], p_buf[slot])
        acc_sc[...] = pv if first else acc_sc[...] + pv

    p_buf[0] = first_probs()
    p_buf[1] = probs(1)
    values(0, 0, first=True)

    for t in range(1, n_kv - 1, 2):
        p_buf[0] = probs(t + 1)
        values(t, 1)
        p_buf[1] = probs(t + 2)
        values(t + 1, 0)
    values(n_kv - 1, 1)


def _static_scratch(rows, nq):
    return [pltpu.VMEM((2, KV_CHUNK, nq), BF16), pltpu.VMEM((rows, nq), F32)]


def _query_tile_loop(n_tiles, tile_fn):
    def body(i, carry):
        tile_fn(i)
        return carry
    lax.fori_loop(0, n_tiles, body, 0)


def _flash_mla_static_kernel(q_ref, k_ref, vt_ref, o_ref, p_buf, acc_sc, *, tq, n_kv):
    def tile(i):
        rows = pl.ds(pl.multiple_of(i * tq, tq), tq)
        probs = lambda t: jnp.exp2(_dot_nt(_key_chunk(k_ref, t), q_ref[rows, :])).astype(BF16)
        _static_core(lambda: probs(0), probs, lambda t: t, vt_ref, p_buf, acc_sc, n_kv)
        _mla_finalize(acc_sc, o_ref.at[rows, :])

    _query_tile_loop(q_ref.shape[0] // tq, tile)


def _flash_mla_static(q, k, vt, tq):
    b, s, hw = q.shape
    n_kv = s // KV_CHUNK
    kern = functools.partial(_flash_mla_static_kernel, tq=tq, n_kv=n_kv)
    head_slab = pl.BlockSpec((None, s, LANE), lambda bi, h: (bi, 0, h))
    return pl.pallas_call(
        kern,
        grid=(b, MLA_HEADS),
        in_specs=[
            head_slab,
            head_slab,
            pl.BlockSpec((None, None, n_kv, MLA_VT_ROWS, KV_CHUNK), lambda bi, h: (bi, h, 0, 0, 0)),
        ],
        out_specs=head_slab,
        out_shape=jax.ShapeDtypeStruct((b, s, hw), BF16),
        scratch_shapes=_static_scratch(MLA_VT_ROWS, tq),
        compiler_params=_params(("parallel", "parallel")),
        name="flash_mla_static",
    )(q, k, vt)


def _pos_lanes(pos0, rows, first_lane):
    pos = pos0 + lax.broadcasted_iota(jnp.int32, (rows, LANE), 0)
    lane = lax.broadcasted_iota(jnp.int32, (rows, LANE), 1) - first_lane
    hi = lax.shift_right_logical(pos, int(math.log2(POS_SPLIT))).astype(F32)
    lo = (pos & (POS_SPLIT - 1)).astype(F32)
    p = DIFF_AUG_PARTS
    in_hi = (lane >= 0) & (lane < p)
    in_lo = (lane >= p) & (lane < 2 * p)
    return jnp.where(in_hi, hi, jnp.where(in_lo, lo, 0.0))


def _diff_pre_kernel(x_ref, ng_ref, w_ref, wvt_ref, gq_ref, gk_ref, kaug_ref, q_ref, k_ref, vt_ref):
    h = _rms(x_ref[...], ng_ref[...]).astype(BF16)
    qkv = _dot(h, w_ref[...])
    v_t = _dot_nt(wvt_ref[...], h)
    hw = DIFF_HEADS * LANE
    lo = lax.broadcasted_iota(jnp.int32, (1, LANE), 1) < DIFF_HD
    q_scale = (DIFF_HD ** -0.5) * LOG2E
    tm = qkv.shape[0]
    k_pos = _pos_lanes(pl.program_id(1) * tm, tm, 0)
    ones_row = (lax.broadcasted_iota(jnp.int32, (DIFF_VT_ROWS, tm), 0) == LANE).astype(F32)

    def half_norm(t, g):
        t2 = t * t
        ss_lo = jnp.sum(jnp.where(lo, t2, 0.0), axis=-1, keepdims=True)
        ss_hi = jnp.sum(jnp.where(lo, 0.0, t2), axis=-1, keepdims=True)
        r = lax.rsqrt(jnp.where(lo, ss_lo, ss_hi) * (1.0 / DIFF_HD) + EPS)
        return t * r * g

    for hd in range(DIFF_HEADS):
        sl = slice(hd * LANE, (hd + 1) * LANE)
        q_ref[:, sl] = (half_norm(qkv[:, sl], gq_ref[...]) * q_scale).astype(BF16)
        ks = slice(2 * hd * LANE, (2 * hd + 1) * LANE)
        k_ref[:, ks] = half_norm(qkv[:, hw + hd * LANE:hw + (hd + 1) * LANE], gk_ref[...]).astype(BF16)
        k_ref[:, (2 * hd + 1) * LANE:(2 * hd + 2) * LANE] = (k_pos + kaug_ref[hd]).astype(BF16)
        vt_ref[hd] = (v_t[hd * DIFF_VT_ROWS:(hd + 1) * DIFF_VT_ROWS] + ones_row).astype(BF16)


def _slope_pieces():
    slopes = 2.0 ** (-8.0 * jnp.arange(1, DIFF_HEADS + 1, dtype=F32) / DIFF_HEADS) * LOG2E
    pieces, rest = [], slopes
    for _ in range(DIFF_AUG_PARTS):
        piece = rest.astype(BF16).astype(F32)
        pieces.append(piece)
        rest = rest - piece
    return jnp.stack(pieces, axis=1)


def _diff_bias_tables(bound):
    p = DIFF_AUG_PARTS
    pieces = _slope_pieces()
    zeros = jnp.zeros((DIFF_HEADS, LANE - 4 * p - 1), F32)
    one = jnp.ones((DIFF_HEADS, 1), F32)
    zp = jnp.zeros((DIFF_HEADS, p), F32)
    k_tab = jnp.concatenate([zp, zp, POS_SPLIT * pieces, pieces, one, zeros], axis=1)[:, None, :]
    shift = -bound * one
    after = jnp.concatenate([-POS_SPLIT * pieces, -pieces, zp, zp, shift, zeros], axis=1)
    before = jnp.concatenate([POS_SPLIT * pieces, pieces, zp, zp, shift, zeros], axis=1)
    diag = jnp.concatenate([zp, zp, zp, zp, shift, zeros], axis=1)
    return k_tab, jnp.stack([after, before, diag], axis=1)


def _diff_pre(x, norm_g, w_qkv, q_g, k_g, k_tab):
    b, s, _ = x.shape
    tm = KV_CHUNK
    hw = DIFF_HEADS * LANE
    out = jax.ShapeDtypeStruct((b, s, hw), BF16)
    out_k = jax.ShapeDtypeStruct((b, s, 2 * hw), BF16)
    vt_shape, vt_spec = _vt_shape_and_spec(b, DIFF_HEADS, DIFF_VT_ROWS, s)
    tile = lambda bi, i: (bi, i, 0)
    wvt = w_qkv[:, 2 * hw:].T.reshape(DIFF_HEADS, LANE, D_MODEL)
    wvt = jnp.concatenate([wvt, jnp.zeros((DIFF_HEADS, DIFF_VT_ROWS - LANE, D_MODEL), F32)], axis=1)
    wvt = wvt.reshape(DIFF_HEADS * DIFF_VT_ROWS, D_MODEL).astype(BF16)
    weights = (norm_g[None, :], w_qkv[:, :2 * hw].astype(BF16), wvt, q_g.reshape(1, LANE),
               k_g.reshape(1, LANE), k_tab)
    return pl.pallas_call(
        _diff_pre_kernel,
        grid=(b, s // tm),
        in_specs=[pl.BlockSpec((None, tm, D_MODEL), tile)] + [_const_spec(w.shape) for w in weights],
        out_specs=[pl.BlockSpec((None, tm, hw), tile), pl.BlockSpec((None, tm, 2 * hw), tile), vt_spec],
        out_shape=[out, out_k, vt_shape],
        compiler_params=_params(("parallel", "parallel")),
        name="diff_pre",
    )(x, *weights)


def _flash_diff_kernel(slope_ref, q_ref, k_ref, vt_ref, dmat_ref, lam_ref, subg_ref, o_ref,
                       qcat_sc, *scratch, tq, n_kv, lambda_init):
    acc_sc = scratch[-1]
    lo = lax.broadcasted_iota(jnp.int32, (1, LANE), 1) < DIFF_HD
    q = q_ref[...]
    zero = jnp.zeros_like(q)
    qcat_sc[:tq] = jnp.where(lo, q, zero)
    qcat_sc[tq:] = jnp.where(lo, zero, q)
    neg_slope = slope_ref[pl.program_id(1)]
    q0 = pl.program_id(2) * tq

    def scores(j):
        delta = (j * KV_CHUNK - q0).astype(F32)
        bias = jnp.abs(dmat_ref[...] + delta) * neg_slope
        return _dot_nt(_key_chunk(k_ref, j), qcat_sc[...]) + jnp.concatenate([bias, bias], axis=1)

    _flash_core(scores, vt_ref, *scratch, n_kv)
    _diff_finalize(acc_sc, lam_ref, subg_ref, o_ref, tq, lambda_init)


def _diff_finalize(acc_sc, lam_ref, subg_ref, o_ref, tq, lambda_init):
    acc = acc_sc[...]
    _diff_store(acc[:LANE] / acc[LANE:LANE + 1], lam_ref, subg_ref, o_ref, tq, lambda_init)


def _diff_store(o, lam_ref, subg_ref, o_ref, tq, lambda_init):
    lp = lam_ref[...]
    lam = (jnp.exp(jnp.sum(lp[0:1] * lp[1:2], axis=-1, keepdims=True))
           - jnp.exp(jnp.sum(lp[2:3] * lp[3:4], axis=-1, keepdims=True)) + lambda_init)
    o = (o[:, :tq] - lam * o[:, tq:]).T
    o = _rms(o, subg_ref[...]) * (1.0 - lambda_init)
    o_ref[...] = o.astype(BF16)


def _flash_diff_static_kernel(slope_ref, q_ref, k_ref, vt_ref, dmat_ref, lam_ref, subg_ref, qtab_ref,
                              o_ref, qz_sc, aug_sc, p_buf, acc_sc, *, tq, n_kv, lambda_init):
    tile = functools.partial(_flash_diff_static_tile, slope_ref, q_ref, k_ref, vt_ref, dmat_ref, lam_ref,
                             subg_ref, qtab_ref, o_ref, qz_sc, aug_sc, p_buf, acc_sc, tq, n_kv,
                             lambda_init)
    _query_tile_loop(q_ref.shape[0] // tq, tile)


def _flash_diff_static_tile(slope_ref, q_ref, k_ref, vt_ref, dmat_ref, lam_ref, subg_ref, qtab_ref,
                            o_ref, qz_sc, aug_sc, p_buf, acc_sc, tq, n_kv, lambda_init, i):
    lo = lax.broadcasted_iota(jnp.int32, (1, LANE), 1) < DIFF_HD
    q0 = pl.multiple_of(i * tq, tq)
    q = q_ref[pl.ds(q0, tq), :]
    zero = jnp.zeros_like(q)
    qz_sc[:tq] = jnp.where(lo, q, zero)
    qz_sc[tq:] = jnp.where(lo, zero, q)
    n_d = tq // KV_CHUNK
    q_pos = _pos_lanes(q0, tq, 2 * DIFF_AUG_PARTS)
    tab = qtab_ref[...]
    for kind, sign in enumerate((1.0, -1.0, 0.0)):
        aug_sc[kind] = (tab[kind:kind + 1] + sign * q_pos).astype(BF16)
    jd = lax.shift_right_logical(q0, int(math.log2(KV_CHUNK)))
    neg_slope = slope_ref[pl.program_id(1)]

    def with_bias_columns(aug):
        return jnp.concatenate([qz_sc[...], jnp.concatenate([aug, aug], axis=0)], axis=1)

    def diag_probs(d):
        kinds = [2 if e == d else (1 if e > d else 0) for e in range(n_d)]
        aug = jnp.concatenate([aug_sc[kind, e * KV_CHUNK:(e + 1) * KV_CHUNK, :]
                               for e, kind in enumerate(kinds)], axis=0)
        s = _dot_nt(_key_chunk(k_ref, jd + d), with_bias_columns(aug))
        bias = jnp.abs(dmat_ref[...]) * neg_slope
        cols = []
        for c in range(2):
            for e in range(n_d):
                blk = s[:, c * tq + e * KV_CHUNK:c * tq + (e + 1) * KV_CHUNK]
                cols.append(blk + bias if e == d else blk)
        return jnp.exp2(jnp.concatenate(cols, axis=1)).astype(BF16)

    def chunk_of(t):
        r = t - n_d
        return jnp.where(t < n_d, jd + t, r + n_d * (r >= jd).astype(jnp.int32))

    def probs(t):
        if isinstance(t, int) and t < n_d:
            return diag_probs(t)
        j = chunk_of(t)
        before = (j < jd).astype(jnp.int32)
        return jnp.exp2(_dot_nt(_key_chunk(k_ref, j), with_bias_columns(aug_sc[before]))).astype(BF16)

    _static_core(lambda: diag_probs(0), probs, chunk_of, vt_ref, p_buf, acc_sc, n_kv)
    _diff_finalize(acc_sc, lam_ref, subg_ref, o_ref.at[pl.ds(q0, tq), :], tq, lambda_init)


def _flash_diff(q, k, vt, lam_p, sub_g, q_tab, lambda_init, tq, static_shift):
    b, s, hw = q.shape
    n_kv = s // KV_CHUNK
    slopes = 2.0 ** (-8.0 * jnp.arange(1, DIFF_HEADS + 1, dtype=F32) / DIFF_HEADS)
    neg_slopes = -slopes * LOG2E
    d_cols = KV_CHUNK if static_shift else tq
    dmat = (jnp.arange(KV_CHUNK, dtype=F32)[:, None] - jnp.arange(d_cols, dtype=F32)[None, :])
    const = dict(pipeline_mode=pl.Buffered(1))
    origin = lambda *_: (0, 0)
    vt_spec = pl.BlockSpec((None, None, n_kv, DIFF_VT_ROWS, KV_CHUNK), lambda bi, h, *_: (bi, h, 0, 0, 0))
    small_specs = [
        pl.BlockSpec((KV_CHUNK, d_cols), origin, **const),
        pl.BlockSpec((4, DIFF_HD), origin, **const),
        pl.BlockSpec((1, LANE), origin, **const),
    ]
    args = [neg_slopes, q, k, vt, dmat, lam_p, sub_g[None, :]]
    if static_shift:
        assert tq % KV_CHUNK == 0
        kern = functools.partial(_flash_diff_static_kernel, tq=tq, n_kv=n_kv, lambda_init=lambda_init)
        grid = (b, DIFF_HEADS)
        q_spec = pl.BlockSpec((None, s, LANE), lambda bi, h, sl: (bi, 0, h))
        k_spec = pl.BlockSpec((None, s, 2 * LANE), lambda bi, h, sl: (bi, 0, h))
        in_specs = [q_spec, k_spec, vt_spec] + small_specs + [
            pl.BlockSpec((None, 3, LANE), lambda bi, h, sl: (h, 0, 0))]
        args.append(q_tab)
        scratch = ([pltpu.VMEM((2 * tq, LANE), BF16), pltpu.VMEM((3, tq, LANE), BF16)]
                   + _static_scratch(DIFF_VT_ROWS, 2 * tq))
        name = "flash_diff_static"
    else:
        kern = functools.partial(_flash_diff_kernel, tq=tq, n_kv=n_kv, lambda_init=lambda_init)
        grid = (b, DIFF_HEADS, s // tq)
        q_spec = pl.BlockSpec((None, tq, LANE), lambda bi, h, i, sl: (bi, i, h))
        k_spec = pl.BlockSpec((None, s, LANE), lambda bi, h, i, sl: (bi, 0, 2 * h))
        in_specs = [q_spec, k_spec, vt_spec] + small_specs
        scratch = [pltpu.VMEM((2 * tq, LANE), BF16)] + _flash_scratch(DIFF_VT_ROWS, 2 * tq)
        name = "flash_diff"
    grid_spec = pltpu.PrefetchScalarGridSpec(
        num_scalar_prefetch=1, grid=grid, in_specs=in_specs, out_specs=q_spec, scratch_shapes=scratch)
    return pl.pallas_call(
        kern,
        grid_spec=grid_spec,
        out_shape=jax.ShapeDtypeStruct((b, s, hw), BF16),
        compiler_params=_params(("parallel", "parallel") + (("arbitrary",) if len(grid) == 3 else ())),
        name=name,
    )(*args)


def _post_xattn_kernel(x_ref, om_ref, wom_ref, ng_ref, wq_ref, qg_ref, k_ref, v_ref, wo_ref,
                       y_ref, o_sc):
    x1 = x_ref[...] + _dot(om_ref[...], wom_ref[...])
    h = _rms(x1, ng_ref[...]).astype(BF16)
    q = _dot(h, wq_ref[...])
    qg = qg_ref[...] * ((XA_HD ** -0.5) * LOG2E)
    for hd in range(XA_HEADS):
        sl = slice(hd * XA_HD, (hd + 1) * XA_HD)
        qh = _rms(q[:, sl], qg).astype(BF16)
        s = _dot_nt(qh, k_ref[:, sl])
        p = jnp.exp2(s - jnp.max(s, axis=-1, keepdims=True))
        l = jnp.sum(p, axis=-1, keepdims=True)
        o_sc[:, sl] = (_dot(p.astype(BF16), v_ref[:, sl]) / l).astype(BF16)
    y_ref[...] = x1 + _dot(o_sc[...], wo_ref[...])


def _post_xattn(x, o_mix, w_o_mix, norm_g, w_q, q_g, k_mem, v_mem, layer, w_o, tm):
    b, s, _ = x.shape
    tile = lambda bi, i: (bi, i, 0)
    mem_spec = pl.BlockSpec((None, None, N_MEM, D_MODEL), lambda bi, i: (layer, bi, 0, 0))
    sq = (D_MODEL, D_MODEL)
    return pl.pallas_call(
        _post_xattn_kernel,
        grid=(b, s // tm),
        in_specs=[
            pl.BlockSpec((None, tm, D_MODEL), tile),
            pl.BlockSpec((None, tm, o_mix.shape[-1]), tile),
            _const_spec(w_o_mix.shape),
            _const_spec((1, D_MODEL)),
            _const_spec(sq),
            _const_spec((1, XA_HD)),
            mem_spec,
            mem_spec,
            _const_spec(sq),
        ],
        out_specs=pl.BlockSpec((None, tm, D_MODEL), tile),
        out_shape=jax.ShapeDtypeStruct(x.shape, F32),
        scratch_shapes=[pltpu.VMEM((tm, D_MODEL), BF16)],
        compiler_params=_params(("parallel", "parallel")),
        name="post_xattn",
    )(x, o_mix, w_o_mix, norm_g[None, :], w_q.astype(BF16), q_g[None, :], k_mem, v_mem,
      w_o.astype(BF16))


def _ffn_kernel(x_ref, prev_ref, next_ref, ng_ref, wg_ref, wu_ref, cw_ref, cb_ref, wd_ref, y_ref,
                *, tm, n_chunks):
    i = pl.program_id(1)
    keep_prev = (i > 0).astype(F32)
    keep_next = (i < pl.num_programs(1) - 1).astype(F32)
    x = x_ref[...]
    xe = jnp.concatenate([prev_ref[...] * keep_prev, x, next_ref[...] * keep_next], axis=0)
    he = _rms(xe, ng_ref[...]).astype(BF16)
    hc = he[HALO:HALO + tm]
    cw = cw_ref[...]
    cb = cb_ref[...]
    y = x
    n_tiles = D_FF // MXU_DIM
    bounds = [MXU_DIM * ((n_tiles * c + n_chunks - 1) // n_chunks) for c in range(n_chunks)] + [D_FF]
    for c in range(n_chunks):
        cs = slice(bounds[c], bounds[c + 1])
        ge = _dot(he, wg_ref[:, cs])
        u = _dot(hc, wu_ref[:, cs])
        g = (ge[HALO - 1:HALO - 1 + tm] * cw[0:1, cs] + ge[HALO:HALO + tm] * cw[1:2, cs]
             + ge[HALO + 1:HALO + 1 + tm] * cw[2:3, cs] + cb[:, cs])
        act = (g * jax.nn.sigmoid(g) * u).astype(BF16)
        y = y + _dot(act, wd_ref[cs, :])
    y_ref[...] = y


def _ffn(x, norm_g, w_gu, conv_w, conv_b, w_down, tm, n_chunks=1):
    b, s, _ = x.shape
    nh = tm // HALO
    last = s // HALO - 1
    w_g = w_gu[:, :D_FF].astype(BF16)
    w_u = w_gu[:, D_FF:].astype(BF16)
    weights = (norm_g[None, :], w_g, w_u, conv_w, conv_b[None, :], w_down.astype(BF16))
    kern = functools.partial(_ffn_kernel, tm=tm, n_chunks=n_chunks)
    return pl.pallas_call(
        kern,
        grid=(b, s // tm),
        in_specs=[
            pl.BlockSpec((None, tm, D_MODEL), lambda bi, i: (bi, i, 0)),
            pl.BlockSpec((None, HALO, D_MODEL), lambda bi, i: (bi, jnp.maximum(i * nh - 1, 0), 0)),
            pl.BlockSpec((None, HALO, D_MODEL), lambda bi, i: (bi, jnp.minimum((i + 1) * nh, last), 0)),
        ] + [_const_spec(w.shape) for w in weights],
        out_specs=pl.BlockSpec((None, tm, D_MODEL), lambda bi, i: (bi, i, 0)),
        out_shape=jax.ShapeDtypeStruct(x.shape, F32),
        compiler_params=_params(("parallel", "parallel")),
        name="ffn",
    )(x, x, x, *weights)


def _tiles(s):
    return dict(tm=min(512, s), tq_mla=min(4096, s), tq_diff=min(2048, s),
                tq_mla_online=min(1024, s), tq_diff_online=min(512, s))


def _mixer_weights(mla_p, diff_p):
    layers = []
    for i in range(DEPTH):
        j = i // 2
        if i % 2 == 0:
            weights, w_o_mix, bound = _mla_weights(*[p[j] for p in mla_p])
            layers.append(dict(weights=weights, w_o_mix=w_o_mix, bound=bound))
        else:
            norm_g, w_qkv, q_g, k_g, lam_p, sub_g, w_o = [p[j] for p in diff_p]
            bound = _score_bound(DIFF_HD, q_g, k_g)
            k_tab, q_tab = _diff_bias_tables(bound)
            layers.append(dict(pre=(norm_g, w_qkv, q_g, k_g, k_tab), lam_p=lam_p, sub_g=sub_g,
                               q_tab=q_tab, w_o_mix=w_o.astype(BF16), bound=bound))
    return layers


def _trunk(x, mem, mixers, xa_p, ffn_p):
    s = x.shape[1]
    assert s % KV_CHUNK == 0
    t = _tiles(s)
    xa_norm, xa_mem_norm, xa_w_q, xa_w_kv, xa_q_norm, xa_k_norm, xa_w_o = xa_p
    k_mem, v_mem = _mem_kv(mem, xa_mem_norm, xa_w_kv, xa_k_norm)
    cos, sin = _rope_tables(s)
    for i in range(DEPTH):
        mx = mixers[i]
        static_ok = mx["bound"] <= MAX_STATIC_BOUND
        if i % 2 == 0:
            q, k, vt = _mla_pre(x, mx["weights"], cos, sin)
            o_mix = lax.cond(static_ok,
                             lambda q, k, vt: _flash_mla_static(q, k, vt, t["tq_mla"]),
                             lambda q, k, vt: _flash_mla(q, k, vt, t["tq_mla_online"]), q, k, vt)
        else:
            lambda_init = 0.8 - 0.6 * math.exp(-0.3 * i)
            q, k, vt = _diff_pre(x, *mx["pre"])
            flash = lambda static: functools.partial(
                _flash_diff, lambda_init=lambda_init, static_shift=static,
                tq=t["tq_diff"] if static else t["tq_diff_online"])
            o_mix = lax.cond(static_ok, flash(True), flash(False),
                             q, k, vt, mx["lam_p"], mx["sub_g"], mx["q_tab"])
        x = _post_xattn(x, o_mix, mx["w_o_mix"], xa_norm[i], xa_w_q[i], xa_q_norm[i], k_mem, v_mem, i,
                        xa_w_o[i], t["tm"])
        x = _ffn(x, *[p[i] for p in ffn_p], t["tm"])
    return x


def kernel(x_prompt, x_sample, mem_prompt, mem_sample, mla_norm, mla_w_down, mla_q_lat_norm, mla_kv_lat_norm, mla_w_uq, mla_w_ukv, mla_q_norm, mla_k_norm, mla_w_o, diff_norm, diff_w_qkv, diff_q_norm, diff_k_norm, diff_lambda, diff_sub_norm, diff_w_o, xa_norm, xa_mem_norm, xa_w_q, xa_w_kv, xa_q_norm, xa_k_norm, xa_w_o, ffn_norm, ffn_w_gu, ffn_conv_w, ffn_conv_b, ffn_w_down):
    mla_p = (mla_norm, mla_w_down, mla_q_lat_norm, mla_kv_lat_norm, mla_w_uq, mla_w_ukv,
             mla_q_norm, mla_k_norm, mla_w_o)
    diff_p = (diff_norm, diff_w_qkv, diff_q_norm, diff_k_norm, diff_lambda, diff_sub_norm, diff_w_o)
    xa_p = (xa_norm, xa_mem_norm, xa_w_q, xa_w_kv, xa_q_norm, xa_k_norm, xa_w_o)
    ffn_p = (ffn_norm, ffn_w_gu, ffn_conv_w, ffn_conv_b, ffn_w_down)
    mixers = _mixer_weights(mla_p, diff_p)
    y_prompt = _trunk(x_prompt, mem_prompt, mixers, xa_p, ffn_p)
    y_sample = _trunk(x_sample, mem_sample, mixers, xa_p, ffn_p)
    return (y_prompt, y_sample)
```

```python
import functools
import math

import jax
import jax.numpy as jnp
from jax import lax
from jax.experimental import pallas as pl
from jax.experimental.pallas import tpu as pltpu

D_MODEL = 1024
DEPTH = 4
N_MEM = 256
EPS = 1e-6
MLA_HEADS = 8
MLA_Q_LORA = 384
MLA_KV_LORA = 256
MLA_NOPE = 64
MLA_ROPE = 32
MLA_QK = MLA_NOPE + MLA_ROPE
MLA_V = 64
ROPE_THETA = 10000.0
DIFF_HEADS = 8
DIFF_HD = 64
XA_HEADS = 4
XA_HD = D_MODEL // XA_HEADS
D_FF = 2816
CONV_W = 3

LANE = 128
HALO = 8
BF16_ROWS = 16
MXU_DIM = 256
KV_CHUNK = 512
DIFF_VT_ROWS = LANE + BF16_ROWS
MLA_VT_ROWS = MLA_V + BF16_ROWS
SHIFT_LANE = MLA_QK
SCORE_MARGIN = 1.02
MAX_STATIC_BOUND = 40.0
DIFF_AUG_PARTS = 3
POS_SPLIT = 64
LOG2E = math.log2(math.e)
NEG_BIG = -1e30
VMEM_LIMIT = 56 * 1024 * 1024

F32 = jnp.float32
BF16 = jnp.bfloat16


def _const_spec(shape):
    nd = len(shape)
    return pl.BlockSpec(shape, lambda *_: (0,) * nd, pipeline_mode=pl.Buffered(1))


def _params(sem):
    return pltpu.CompilerParams(dimension_semantics=sem, vmem_limit_bytes=VMEM_LIMIT)


def _rms(x, g):
    return x * lax.rsqrt(jnp.mean(x * x, axis=-1, keepdims=True) + EPS) * g


def _dot(a, b):
    return jnp.dot(a, b, preferred_element_type=F32)


def _dot_nt(a, b):
    return lax.dot_general(a, b, (((1,), (1,)), ((), ())), preferred_element_type=F32)


def _mem_kv_kernel(mem_ref, g_ref, w_ref, kg_ref, k_ref, v_ref):
    h = _rms(mem_ref[...], g_ref[...]).astype(BF16)
    kv = _dot(h, w_ref[...])
    kg = kg_ref[...]
    for hd in range(XA_HEADS):
        sl = slice(hd * XA_HD, (hd + 1) * XA_HD)
        k_ref[:, sl] = _rms(kv[:, sl], kg).astype(BF16)
    v_ref[...] = kv[:, D_MODEL:].astype(BF16)


def _mem_kv(mem, mem_g, w_kv, k_g):
    bm = mem.shape[0]
    out = jax.ShapeDtypeStruct((DEPTH, bm, N_MEM, D_MODEL), BF16)
    return pl.pallas_call(
        _mem_kv_kernel,
        grid=(DEPTH, bm),
        in_specs=[
            pl.BlockSpec((None, N_MEM, D_MODEL), lambda l, b: (b, 0, 0)),
            pl.BlockSpec((None, 1, D_MODEL), lambda l, b: (l, 0, 0)),
            pl.BlockSpec((None, D_MODEL, 2 * D_MODEL), lambda l, b: (l, 0, 0)),
            pl.BlockSpec((None, 1, XA_HD), lambda l, b: (l, 0, 0)),
        ],
        out_specs=[
            pl.BlockSpec((None, None, N_MEM, D_MODEL), lambda l, b: (l, b, 0, 0)),
            pl.BlockSpec((None, None, N_MEM, D_MODEL), lambda l, b: (l, b, 0, 0)),
        ],
        out_shape=[out, out],
        compiler_params=_params(("arbitrary", "arbitrary")),
        name="mem_kv",
    )(mem, mem_g[:, None, :], w_kv.astype(BF16), k_g[:, None, :])


def _mla_pre_kernel(x_ref, ng_ref, wd_ref, qlg_ref, kvlg_ref, wuq_ref, wuk_ref, wvt_ref,
                    gq_ref, gk_ref, qshift_ref, cos_ref, sin_ref, q_ref, k_ref, vt_ref):
    h = _rms(x_ref[...], ng_ref[...]).astype(BF16)
    down = _dot(h, wd_ref[...])
    c_q = _rms(down[:, :MLA_Q_LORA], qlg_ref[...]).astype(BF16)
    kv0 = MLA_Q_LORA
    c_kv = _rms(down[:, kv0:kv0 + MLA_KV_LORA], kvlg_ref[...]).astype(BF16)
    kr = down[:, kv0 + MLA_KV_LORA:kv0 + MLA_KV_LORA + LANE]
    kr_rot = down[:, kv0 + MLA_KV_LORA + LANE:]
    qq = _dot(c_q, wuq_ref[...])
    kn = _dot(c_kv, wuk_ref[...])
    v_t = _dot_nt(wvt_ref[...], c_kv)
    ones_row = (lax.broadcasted_iota(jnp.int32, (MLA_VT_ROWS, v_t.shape[1]), 0) == MLA_V).astype(F32)
    cos = cos_ref[...]
    sin = sin_ref[...]
    gq = gq_ref[...]
    gk = gk_ref[...]
    hw = MLA_HEADS * LANE
    lane = lax.broadcasted_iota(jnp.int32, (1, LANE), 1)
    k_one = (lane == SHIFT_LANE).astype(F32)
    q_shift = qshift_ref[...]
    q_scale = (MLA_QK ** -0.5) * LOG2E
    for hd in range(MLA_HEADS):
        sl = slice(hd * LANE, (hd + 1) * LANE)
        qh = qq[:, sl]
        rq = lax.rsqrt(jnp.sum(qh * qh, axis=-1, keepdims=True) * (1.0 / MLA_QK) + EPS)
        q_rot = qq[:, hw + hd * LANE:hw + (hd + 1) * LANE]
        q_ref[:, sl] = ((rq * q_scale) * (qh * gq * cos + q_rot * sin) + q_shift).astype(BF16)
        kh = kn[:, sl] + kr
        rk = lax.rsqrt(jnp.sum(kh * kh, axis=-1, keepdims=True) * (1.0 / MLA_QK) + EPS)
        k_ref[:, sl] = (rk * (kh * gk * cos + kr_rot * sin) + k_one).astype(BF16)
        vt_ref[hd] = (v_t[hd * MLA_VT_ROWS:(hd + 1) * MLA_VT_ROWS] + ones_row).astype(BF16)


def _rot_half_cols(w):
    half = MLA_ROPE // 2
    return jnp.concatenate([-w[..., half:], w[..., :half]], axis=-1)


def _mla_weights(norm_g, w_down, q_lat_g, kv_lat_g, w_uq, w_ukv, q_g, k_g, w_o):
    nq, nkv = MLA_Q_LORA, MLA_KV_LORA
    w_dq, w_dkv, w_kr = w_down[:, :nq], w_down[:, nq:nq + nkv], w_down[:, nq + nkv:]
    pad_lo = jnp.zeros((D_MODEL, MLA_NOPE), F32)
    pad_hi = jnp.zeros((D_MODEL, LANE - MLA_QK), F32)
    kr_blk = jnp.concatenate([pad_lo, w_kr, pad_hi], axis=1)
    krot_blk = jnp.concatenate([pad_lo, _rot_half_cols(w_kr * k_g[MLA_NOPE:]), pad_hi], axis=1)
    wd = jnp.concatenate([w_dq, w_dkv, kr_blk, krot_blk], axis=1).astype(BF16)

    wq = w_uq.reshape(nq, MLA_HEADS, MLA_QK)
    zq = jnp.zeros((nq, MLA_HEADS, LANE - MLA_QK), F32)
    wq_main = jnp.concatenate([wq, zq], axis=-1).reshape(nq, MLA_HEADS * LANE)
    wq_rot = jnp.concatenate(
        [jnp.zeros((nq, MLA_HEADS, MLA_NOPE), F32),
         _rot_half_cols(wq[..., MLA_NOPE:] * q_g[MLA_NOPE:]), zq], axis=-1
    ).reshape(nq, MLA_HEADS * LANE)
    wuq = jnp.concatenate([wq_main, wq_rot], axis=1).astype(BF16)

    wkv = w_ukv.reshape(nkv, MLA_HEADS, MLA_NOPE + MLA_V)
    zk = jnp.zeros((nkv, MLA_HEADS, LANE - MLA_NOPE), F32)
    wuk = jnp.concatenate([wkv[..., :MLA_NOPE], zk], axis=-1).reshape(nkv, MLA_HEADS * LANE).astype(BF16)
    wvt = jnp.transpose(wkv[..., MLA_NOPE:], (1, 2, 0))
    wvt = jnp.concatenate([wvt, jnp.zeros((MLA_HEADS, MLA_VT_ROWS - MLA_V, nkv), F32)], axis=1)
    wvt = wvt.reshape(MLA_HEADS * MLA_VT_ROWS, nkv).astype(BF16)

    zg = jnp.zeros((LANE - MLA_QK,), F32)
    gq = jnp.concatenate([q_g, zg])[None, :]
    gk = jnp.concatenate([k_g, zg])[None, :]

    wo = w_o.reshape(MLA_HEADS, MLA_V, D_MODEL)
    wo = jnp.concatenate([wo, jnp.zeros((MLA_HEADS, LANE - MLA_V, D_MODEL), F32)], axis=1)
    wo = wo.reshape(MLA_HEADS * LANE, D_MODEL).astype(BF16)
    bound = _score_bound(MLA_QK, q_g, k_g)
    q_shift = -bound * (jnp.arange(LANE) == SHIFT_LANE).astype(F32)[None, :]
    weights = (norm_g[None, :], wd, q_lat_g[None, :], kv_lat_g[None, :], wuq, wuk, wvt, gq, gk, q_shift)
    return weights, wo, bound


def _score_bound(dim, q_g, k_g):
    return SCORE_MARGIN * math.sqrt(dim) * LOG2E * jnp.max(jnp.abs(q_g)) * jnp.max(jnp.abs(k_g))


def _rope_tables(seq):
    inv = ROPE_THETA ** (-jnp.arange(0, MLA_ROPE, 2, dtype=F32) / MLA_ROPE)
    ang = jnp.arange(seq, dtype=F32)[:, None] * inv[None, :]
    ang = jnp.concatenate([ang, ang], axis=-1)
    ones = jnp.ones((seq, MLA_NOPE), F32)
    zlo = jnp.zeros((seq, MLA_NOPE), F32)
    zhi = jnp.zeros((seq, LANE - MLA_QK), F32)
    cos = jnp.concatenate([ones, jnp.cos(ang), zhi], axis=1)
    sin = jnp.concatenate([zlo, jnp.sin(ang), zhi], axis=1)
    return cos, sin


def _vt_shape_and_spec(b, heads, rows, s):
    shape = jax.ShapeDtypeStruct((b, heads, s // KV_CHUNK, rows, KV_CHUNK), BF16)
    spec = pl.BlockSpec((None, heads, None, rows, KV_CHUNK), lambda bi, i: (bi, 0, i, 0, 0))
    return shape, spec


def _mla_pre(x, weights, cos, sin):
    b, s, _ = x.shape
    tm = KV_CHUNK
    hw = MLA_HEADS * LANE
    out = jax.ShapeDtypeStruct((b, s, hw), BF16)
    vt_shape, vt_spec = _vt_shape_and_spec(b, MLA_HEADS, MLA_VT_ROWS, s)
    tile = lambda bi, i: (bi, i, 0)
    w_specs = [_const_spec(w.shape) for w in weights]
    return pl.pallas_call(
        _mla_pre_kernel,
        grid=(b, s // tm),
        in_specs=[pl.BlockSpec((None, tm, D_MODEL), tile)] + w_specs + [
            pl.BlockSpec((tm, LANE), lambda bi, i: (i, 0)),
            pl.BlockSpec((tm, LANE), lambda bi, i: (i, 0)),
        ],
        out_specs=[pl.BlockSpec((None, tm, hw), tile)] * 2 + [vt_spec],
        out_shape=[out, out, vt_shape],
        compiler_params=_params(("parallel", "parallel")),
        name="mla_pre",
    )(x, *weights, cos, sin)


def _flash_core(scores, vt_ref, s_buf, mx_buf, p_buf, al_buf, m_sc, acc_sc, n_kv):
    assert n_kv >= 2 and n_kv % 2 == 0

    def issue_scores(j, slot):
        s = scores(j)
        s_buf[slot] = s
        mx_buf[slot] = jnp.max(s, axis=0, keepdims=True)

    def softmax(slot):
        m_old = m_sc[...]
        m_new = jnp.maximum(m_old, mx_buf[slot])
        p_buf[slot] = jnp.exp2(s_buf[slot] - m_new).astype(BF16)
        al_buf[slot] = jnp.exp2(m_old - m_new)
        m_sc[...] = m_new

    def values(j, slot):
        acc_sc[...] = al_buf[slot] * acc_sc[...] + _dot(vt_ref[j], p_buf[slot])

    m_sc[...] = jnp.full(m_sc.shape, NEG_BIG, F32)
    acc_sc[...] = jnp.zeros(acc_sc.shape, F32)
    issue_scores(0, 0)
    issue_scores(1, 1)
    softmax(0)

    def pair(i, carry):
        j = 2 * i + 1
        issue_scores(j + 1, 0)
        softmax(1)
        values(j - 1, 0)
        issue_scores(j + 2, 1)
        softmax(0)
        values(j, 1)
        return carry

    lax.fori_loop(0, (n_kv - 2) // 2, pair, 0)
    softmax(1)
    values(n_kv - 2, 0)
    values(n_kv - 1, 1)


def _flash_scratch(rows, nq):
    return [
        pltpu.VMEM((2, KV_CHUNK, nq), F32),
        pltpu.VMEM((2, 1, nq), F32),
        pltpu.VMEM((2, KV_CHUNK, nq), BF16),
        pltpu.VMEM((2, 1, nq), F32),
        pltpu.VMEM((1, nq), F32),
        pltpu.VMEM((rows, nq), F32),
    ]


def _key_chunk(k_ref, j):
    return k_ref[pl.ds(pl.multiple_of(j * KV_CHUNK, KV_CHUNK), KV_CHUNK), :]


def _mla_store(o_t, o_ref):
    o_ref[...] = jnp.concatenate([o_t, jnp.zeros((LANE - MLA_V, o_t.shape[1]), F32)], axis=0).T.astype(BF16)


def _mla_finalize(acc_sc, o_ref):
    acc = acc_sc[...]
    _mla_store(acc[:MLA_V] / acc[MLA_V:MLA_V + 1], o_ref)


def _flash_mla_kernel(q_ref, k_ref, vt_ref, o_ref, *scratch, n_kv):
    acc_sc = scratch[-1]
    _flash_core(lambda j: _dot_nt(_key_chunk(k_ref, j), q_ref[...]), vt_ref, *scratch, n_kv)
    _mla_finalize(acc_sc, o_ref)


def _flash_mla(q, k, vt, tq):
    b, s, hw = q.shape
    n_kv = s // KV_CHUNK
    kern = functools.partial(_flash_mla_kernel, n_kv=n_kv)
    return pl.pallas_call(
        kern,
        grid=(b, MLA_HEADS, s // tq),
        in_specs=[
            pl.BlockSpec((None, tq, LANE), lambda bi, h, i: (bi, i, h)),
            pl.BlockSpec((None, s, LANE), lambda bi, h, i: (bi, 0, h)),
            pl.BlockSpec((None, None, n_kv, MLA_VT_ROWS, KV_CHUNK), lambda bi, h, i: (bi, h, 0, 0, 0)),
        ],
        out_specs=pl.BlockSpec((None, tq, LANE), lambda bi, h, i: (bi, i, h)),
        out_shape=jax.ShapeDtypeStruct((b, s, hw), BF16),
        scratch_shapes=_flash_scratch(MLA_VT_ROWS, tq),
        compiler_params=_params(("parallel", "parallel", "arbitrary")),
        name="flash_mla",
    )(q, k, vt)


def _static_core(first_probs, probs, chunk_of, vt_ref, p_buf, acc_sc, n_kv):
    assert n_kv >= 2 and n_kv % 2 == 0

    def values(t, slot, first=False):
        pv = _dot(vt_ref[chunk_of(t)], p_buf[slot])
        acc_sc[...] = pv if first else acc_sc[...] + pv

    p_buf[0] = first_probs()
    p_buf[1] = probs(1)
    values(0, 0, first=True)

    for t in range(1, n_kv - 1, 2):
        p_buf[0] = probs(t + 1)
        values(t, 1)
        p_buf[1] = probs(t + 2)
        values(t + 1, 0)
    values(n_kv - 1, 1)


def _static_scratch(rows, nq):
    return [pltpu.VMEM((2, KV_CHUNK, nq), BF16), pltpu.VMEM((rows, nq), F32)]


def _query_tile_loop(n_tiles, tile_fn):
    def body(i, carry):
        tile_fn(i)
        return carry
    lax.fori_loop(0, n_tiles, body, 0)


def _flash_mla_static_kernel(q_ref, k_ref, vt_ref, o_ref, p_buf, acc_sc, *, tq, n_kv):
    def tile(i):
        rows = pl.ds(pl.multiple_of(i * tq, tq), tq)
        probs = lambda t: jnp.exp2(_dot_nt(_key_chunk(k_ref, t), q_ref[rows, :])).astype(BF16)
        _static_core(lambda: probs(0), probs, lambda t: t, vt_ref, p_buf, acc_sc, n_kv)
        _mla_finalize(acc_sc, o_ref.at[rows, :])

    _query_tile_loop(q_ref.shape[0] // tq, tile)


def _flash_mla_static(q, k, vt, tq):
    b, s, hw = q.shape
    n_kv = s // KV_CHUNK
    kern = functools.partial(_flash_mla_static_kernel, tq=tq, n_kv=n_kv)
    head_slab = pl.BlockSpec((None, s, LANE), lambda bi, h: (bi, 0, h))
    return pl.pallas_call(
        kern,
        grid=(b, MLA_HEADS),
        in_specs=[
            head_slab,
            head_slab,
            pl.BlockSpec((None, None, n_kv, MLA_VT_ROWS, KV_CHUNK), lambda bi, h: (bi, h, 0, 0, 0)),
        ],
        out_specs=head_slab,
        out_shape=jax.ShapeDtypeStruct((b, s, hw), BF16),
        scratch_shapes=_static_scratch(MLA_VT_ROWS, tq),
        compiler_params=_params(("parallel", "parallel")),
        name="flash_mla_static",
    )(q, k, vt)


def _pos_lanes(pos0, rows, first_lane):
    pos = pos0 + lax.broadcasted_iota(jnp.int32, (rows, LANE), 0)
    lane = lax.broadcasted_iota(jnp.int32, (rows, LANE), 1) - first_lane
    hi = lax.shift_right_logical(pos, int(math.log2(POS_SPLIT))).astype(F32)
    lo = (pos & (POS_SPLIT - 1)).astype(F32)
    p = DIFF_AUG_PARTS
    in_hi = (lane >= 0) & (lane < p)
    in_lo = (lane >= p) & (lane < 2 * p)
    return jnp.where(in_hi, hi, jnp.where(in_lo, lo, 0.0))


def _diff_pre_kernel(x_ref, ng_ref, w_ref, wvt_ref, gq_ref, gk_ref, kaug_ref, q_ref, k_ref, vt_ref):
    h = _rms(x_ref[...], ng_ref[...]).astype(BF16)
    qkv = _dot(h, w_ref[...])
    v_t = _dot_nt(wvt_ref[...], h)
    hw = DIFF_HEADS * LANE
    lo = lax.broadcasted_iota(jnp.int32, (1, LANE), 1) < DIFF_HD
    q_scale = (DIFF_HD ** -0.5) * LOG2E
    tm = qkv.shape[0]
    k_pos = _pos_lanes(pl.program_id(1) * tm, tm, 0)
    ones_row = (lax.broadcasted_iota(jnp.int32, (DIFF_VT_ROWS, tm), 0) == LANE).astype(F32)

    def half_norm(t, g):
        t2 = t * t
        ss_lo = jnp.sum(jnp.where(lo, t2, 0.0), axis=-1, keepdims=True)
        ss_hi = jnp.sum(jnp.where(lo, 0.0, t2), axis=-1, keepdims=True)
        r = lax.rsqrt(jnp.where(lo, ss_lo, ss_hi) * (1.0 / DIFF_HD) + EPS)
        return t * r * g

    for hd in range(DIFF_HEADS):
        sl = slice(hd * LANE, (hd + 1) * LANE)
        q_ref[:, sl] = (half_norm(qkv[:, sl], gq_ref[...]) * q_scale).astype(BF16)
        ks = slice(2 * hd * LANE, (2 * hd + 1) * LANE)
        k_ref[:, ks] = half_norm(qkv[:, hw + hd * LANE:hw + (hd + 1) * LANE], gk_ref[...]).astype(BF16)
        k_ref[:, (2 * hd + 1) * LANE:(2 * hd + 2) * LANE] = (k_pos + kaug_ref[hd]).astype(BF16)
        vt_ref[hd] = (v_t[hd * DIFF_VT_ROWS:(hd + 1) * DIFF_VT_ROWS] + ones_row).astype(BF16)


def _slope_pieces():
    slopes = 2.0 ** (-8.0 * jnp.arange(1, DIFF_HEADS + 1, dtype=F32) / DIFF_HEADS) * LOG2E
    pieces, rest = [], slopes
    for _ in range(DIFF_AUG_PARTS):
        piece = rest.astype(BF16).astype(F32)
        pieces.append(piece)
        rest = rest - piece
    return jnp.stack(pieces, axis=1)


def _diff_bias_tables(bound):
    p = DIFF_AUG_PARTS
    pieces = _slope_pieces()
    zeros = jnp.zeros((DIFF_HEADS, LANE - 4 * p - 1), F32)
    one = jnp.ones((DIFF_HEADS, 1), F32)
    zp = jnp.zeros((DIFF_HEADS, p), F32)
    k_tab = jnp.concatenate([zp, zp, POS_SPLIT * pieces, pieces, one, zeros], axis=1)[:, None, :]
    shift = -bound * one
    after = jnp.concatenate([-POS_SPLIT * pieces, -pieces, zp, zp, shift, zeros], axis=1)
    before = jnp.concatenate([POS_SPLIT * pieces, pieces, zp, zp, shift, zeros], axis=1)
    diag = jnp.concatenate([zp, zp, zp, zp, shift, zeros], axis=1)
    return k_tab, jnp.stack([after, before, diag], axis=1)


def _diff_pre(x, norm_g, w_qkv, q_g, k_g, k_tab):
    b, s, _ = x.shape
    tm = KV_CHUNK
    hw = DIFF_HEADS * LANE
    out = jax.ShapeDtypeStruct((b, s, hw), BF16)
    out_k = jax.ShapeDtypeStruct((b, s, 2 * hw), BF16)
    vt_shape, vt_spec = _vt_shape_and_spec(b, DIFF_HEADS, DIFF_VT_ROWS, s)
    tile = lambda bi, i: (bi, i, 0)
    wvt = w_qkv[:, 2 * hw:].T.reshape(DIFF_HEADS, LANE, D_MODEL)
    wvt = jnp.concatenate([wvt, jnp.zeros((DIFF_HEADS, DIFF_VT_ROWS - LANE, D_MODEL), F32)], axis=1)
    wvt = wvt.reshape(DIFF_HEADS * DIFF_VT_ROWS, D_MODEL).astype(BF16)
    weights = (norm_g[None, :], w_qkv[:, :2 * hw].astype(BF16), wvt, q_g.reshape(1, LANE),
               k_g.reshape(1, LANE), k_tab)
    return pl.pallas_call(
        _diff_pre_kernel,
        grid=(b, s // tm),
        in_specs=[pl.BlockSpec((None, tm, D_MODEL), tile)] + [_const_spec(w.shape) for w in weights],
        out_specs=[pl.BlockSpec((None, tm, hw), tile), pl.BlockSpec((None, tm, 2 * hw), tile), vt_spec],
        out_shape=[out, out_k, vt_shape],
        compiler_params=_params(("parallel", "parallel")),
        name="diff_pre",
    )(x, *weights)


def _flash_diff_kernel(slope_ref, q_ref, k_ref, vt_ref, dmat_ref, lam_ref, subg_ref, o_ref,
                       qcat_sc, *scratch, tq, n_kv, lambda_init):
    acc_sc = scratch[-1]
    lo = lax.broadcasted_iota(jnp.int32, (1, LANE), 1) < DIFF_HD
    q = q_ref[...]
    zero = jnp.zeros_like(q)
    qcat_sc[:tq] = jnp.where(lo, q, zero)
    qcat_sc[tq:] = jnp.where(lo, zero, q)
    neg_slope = slope_ref[pl.program_id(1)]
    q0 = pl.program_id(2) * tq

    def scores(j):
        delta = (j * KV_CHUNK - q0).astype(F32)
        bias = jnp.abs(dmat_ref[...] + delta) * neg_slope
        return _dot_nt(_key_chunk(k_ref, j), qcat_sc[...]) + jnp.concatenate([bias, bias], axis=1)

    _flash_core(scores, vt_ref, *scratch, n_kv)
    _diff_finalize(acc_sc, lam_ref, subg_ref, o_ref, tq, lambda_init)


def _diff_finalize(acc_sc, lam_ref, subg_ref, o_ref, tq, lambda_init):
    acc = acc_sc[...]
    _diff_store(acc[:LANE] / acc[LANE:LANE + 1], lam_ref, subg_ref, o_ref, tq, lambda_init)


def _diff_store(o, lam_ref, subg_ref, o_ref, tq, lambda_init):
    lp = lam_ref[...]
    lam = (jnp.exp(jnp.sum(lp[0:1] * lp[1:2], axis=-1, keepdims=True))
           - jnp.exp(jnp.sum(lp[2:3] * lp[3:4], axis=-1, keepdims=True)) + lambda_init)
    o = (o[:, :tq] - lam * o[:, tq:]).T
    o = _rms(o, subg_ref[...]) * (1.0 - lambda_init)
    o_ref[...] = o.astype(BF16)


def _flash_diff_static_kernel(slope_ref, q_ref, k_ref, vt_ref, dmat_ref, lam_ref, subg_ref, qtab_ref,
                              o_ref, qz_sc, aug_sc, p_buf, acc_sc, *, tq, n_kv, lambda_init):
    tile = functools.partial(_flash_diff_static_tile, slope_ref, q_ref, k_ref, vt_ref, dmat_ref, lam_ref,
                             subg_ref, qtab_ref, o_ref, qz_sc, aug_sc, p_buf, acc_sc, tq, n_kv,
                             lambda_init)
    _query_tile_loop(q_ref.shape[0] // tq, tile)


def _flash_diff_static_tile(slope_ref, q_ref, k_ref, vt_ref, dmat_ref, lam_ref, subg_ref, qtab_ref,
                            o_ref, qz_sc, aug_sc, p_buf, acc_sc, tq, n_kv, lambda_init, i):
    lo = lax.broadcasted_iota(jnp.int32, (1, LANE), 1) < DIFF_HD
    q0 = pl.multiple_of(i * tq, tq)
    q = q_ref[pl.ds(q0, tq), :]
    zero = jnp.zeros_like(q)
    qz_sc[:tq] = jnp.where(lo, q, zero)
    qz_sc[tq:] = jnp.where(lo, zero, q)
    n_d = tq // KV_CHUNK
    q_pos = _pos_lanes(q0, tq, 2 * DIFF_AUG_PARTS)
    tab = qtab_ref[...]
    for kind, sign in enumerate((1.0, -1.0, 0.0)):
        aug_sc[kind] = (tab[kind:kind + 1] + sign * q_pos).astype(BF16)
    jd = lax.shift_right_logical(q0, int(math.log2(KV_CHUNK)))
    neg_slope = slope_ref[pl.program_id(1)]

    def with_bias_columns(aug):
        return jnp.concatenate([qz_sc[...], jnp.concatenate([aug, aug], axis=0)], axis=1)

    def diag_probs(d):
        kinds = [2 if e == d else (1 if e > d else 0) for e in range(n_d)]
        aug = jnp.concatenate([aug_sc[kind, e * KV_CHUNK:(e + 1) * KV_CHUNK, :]
                               for e, kind in enumerate(kinds)], axis=0)
        s = _dot_nt(_key_chunk(k_ref, jd + d), with_bias_columns(aug))
        bias = jnp.abs(dmat_ref[...]) * neg_slope
        cols = []
        for c in range(2):
            for e in range(n_d):
                blk = s[:, c * tq + e * KV_CHUNK:c * tq + (e + 1) * KV_CHUNK]
                cols.append(blk + bias if e == d else blk)
        return jnp.exp2(jnp.concatenate(cols, axis=1)).astype(BF16)

    def chunk_of(t):
        r = t - n_d
        return jnp.where(t < n_d, jd + t, r + n_d * (r >= jd).astype(jnp.int32))

    def probs(t):
        if isinstance(t, int) and t < n_d:
            return diag_probs(t)
        j = chunk_of(t)
        before = (j < jd).astype(jnp.int32)
        return jnp.exp2(_dot_nt(_key_chunk(k_ref, j), with_bias_columns(aug_sc[before]))).astype(BF16)

    _static_core(lambda: diag_probs(0), probs, chunk_of, vt_ref, p_buf, acc_sc, n_kv)
    _diff_finalize(acc_sc, lam_ref, subg_ref, o_ref.at[pl.ds(q0, tq), :], tq, lambda_init)


def _flash_diff(q, k, vt, lam_p, sub_g, q_tab, lambda_init, tq, static_shift):
    b, s, hw = q.shape
    n_kv = s // KV_CHUNK
    slopes = 2.0 ** (-8.0 * jnp.arange(1, DIFF_HEADS + 1, dtype=F32) / DIFF_HEADS)
    neg_slopes = -slopes * LOG2E
    d_cols = KV_CHUNK if static_shift else tq
    dmat = (jnp.arange(KV_CHUNK, dtype=F32)[:, None] - jnp.arange(d_cols, dtype=F32)[None, :])
    const = dict(pipeline_mode=pl.Buffered(1))
    origin = lambda *_: (0, 0)
    vt_spec = pl.BlockSpec((None, None, n_kv, DIFF_VT_ROWS, KV_CHUNK), lambda bi, h, *_: (bi, h, 0, 0, 0))
    small_specs = [
        pl.BlockSpec((KV_CHUNK, d_cols), origin, **const),
        pl.BlockSpec((4, DIFF_HD), origin, **const),
        pl.BlockSpec((1, LANE), origin, **const),
    ]
    args = [neg_slopes, q, k, vt, dmat, lam_p, sub_g[None, :]]
    if static_shift:
        assert tq % KV_CHUNK == 0
        kern = functools.partial(_flash_diff_static_kernel, tq=tq, n_kv=n_kv, lambda_init=lambda_init)
        grid = (b, DIFF_HEADS)
        q_spec = pl.BlockSpec((None, s, LANE), lambda bi, h, sl: (bi, 0, h))
        k_spec = pl.BlockSpec((None, s, 2 * LANE), lambda bi, h, sl: (bi, 0, h))
        in_specs = [q_spec, k_spec, vt_spec] + small_specs + [
            pl.BlockSpec((None, 3, LANE), lambda bi, h, sl: (h, 0, 0))]
        args.append(q_tab)
        scratch = ([pltpu.VMEM((2 * tq, LANE), BF16), pltpu.VMEM((3, tq, LANE), BF16)]
                   + _static_scratch(DIFF_VT_ROWS, 2 * tq))
        name = "flash_diff_static"
    else:
        kern = functools.partial(_flash_diff_kernel, tq=tq, n_kv=n_kv, lambda_init=lambda_init)
        grid = (b, DIFF_HEADS, s // tq)
        q_spec = pl.BlockSpec((None, tq, LANE), lambda bi, h, i, sl: (bi, i, h))
        k_spec = pl.BlockSpec((None, s, LANE), lambda bi, h, i, sl: (bi, 0, 2 * h))
        in_specs = [q_spec, k_spec, vt_spec] + small_specs
        scratch = [pltpu.VMEM((2 * tq, LANE), BF16)] + _flash_scratch(DIFF_VT_ROWS, 2 * tq)
        name = "flash_diff"
    grid_spec = pltpu.PrefetchScalarGridSpec(
        num_scalar_prefetch=1, grid=grid, in_specs=in_specs, out_specs=q_spec, scratch_shapes=scratch)
    return pl.pallas_call(
        kern,
        grid_spec=grid_spec,
        out_shape=jax.ShapeDtypeStruct((b, s, hw), BF16),
        compiler_params=_params(("parallel", "parallel") + (("arbitrary",) if len(grid) == 3 else ())),
        name=name,
    )(*args)


def _post_xattn_kernel(x_ref, om_ref, wom_ref, ng_ref, wq_ref, qg_ref, k_ref, v_ref, wo_ref,
                       y_ref, o_sc):
    x1 = x_ref[...] + _dot(om_ref[...], wom_ref[...])
    h = _rms(x1, ng_ref[...]).astype(BF16)
    q = _dot(h, wq_ref[...])
    qg = qg_ref[...] * ((XA_HD ** -0.5) * LOG2E)
    for hd in range(XA_HEADS):
        sl = slice(hd * XA_HD, (hd + 1) * XA_HD)
        qh = _rms(q[:, sl], qg).astype(BF16)
        s = _dot_nt(qh, k_ref[:, sl])
        p = jnp.exp2(s - jnp.max(s, axis=-1, keepdims=True))
        l = jnp.sum(p, axis=-1, keepdims=True)
        o_sc[:, sl] = (_dot(p.astype(BF16), v_ref[:, sl]) / l).astype(BF16)
    y_ref[...] = x1 + _dot(o_sc[...], wo_ref[...])


def _post_xattn(x, o_mix, w_o_mix, norm_g, w_q, q_g, k_mem, v_mem, layer, w_o, tm):
    b, s, _ = x.shape
    tile = lambda bi, i: (bi, i, 0)
    mem_spec = pl.BlockSpec((None, None, N_MEM, D_MODEL), lambda bi, i: (layer, bi, 0, 0))
    sq = (D_MODEL, D_MODEL)
    return pl.pallas_call(
        _post_xattn_kernel,
        grid=(b, s // tm),
        in_specs=[
            pl.BlockSpec((None, tm, D_MODEL), tile),
            pl.BlockSpec((None, tm, o_mix.shape[-1]), tile),
            _const_spec(w_o_mix.shape),
            _const_spec((1, D_MODEL)),
            _const_spec(sq),
            _const_spec((1, XA_HD)),
            mem_spec,
            mem_spec,
            _const_spec(sq),
        ],
        out_specs=pl.BlockSpec((None, tm, D_MODEL), tile),
        out_shape=jax.ShapeDtypeStruct(x.shape, F32),
        scratch_shapes=[pltpu.VMEM((tm, D_MODEL), BF16)],
        compiler_params=_params(("parallel", "parallel")),
        name="post_xattn",
    )(x, o_mix, w_o_mix, norm_g[None, :], w_q.astype(BF16), q_g[None, :], k_mem, v_mem,
      w_o.astype(BF16))


def _ffn_kernel(x_ref, prev_ref, next_ref, ng_ref, wg_ref, wu_ref, cw_ref, cb_ref, wd_ref, y_ref,
                *, tm, n_chunks):
    i = pl.program_id(1)
    keep_prev = (i > 0).astype(F32)
    keep_next = (i < pl.num_programs(1) - 1).astype(F32)
    x = x_ref[...]
    xe = jnp.concatenate([prev_ref[...] * keep_prev, x, next_ref[...] * keep_next], axis=0)
    he = _rms(xe, ng_ref[...]).astype(BF16)
    hc = he[HALO:HALO + tm]
    cw = cw_ref[...]
    cb = cb_ref[...]
    y = x
    n_tiles = D_FF // MXU_DIM
    bounds = [MXU_DIM * ((n_tiles * c + n_chunks - 1) // n_chunks) for c in range(n_chunks)] + [D_FF]
    for c in range(n_chunks):
        cs = slice(bounds[c], bounds[c + 1])
        ge = _dot(he, wg_ref[:, cs])
        u = _dot(hc, wu_ref[:, cs])
        g = (ge[HALO - 1:HALO - 1 + tm] * cw[0:1, cs] + ge[HALO:HALO + tm] * cw[1:2, cs]
             + ge[HALO + 1:HALO + 1 + tm] * cw[2:3, cs] + cb[:, cs])
        act = (g * jax.nn.sigmoid(g) * u).astype(BF16)
        y = y + _dot(act, wd_ref[cs, :])
    y_ref[...] = y


def _ffn(x, norm_g, w_gu, conv_w, conv_b, w_down, tm, n_chunks=1):
    b, s, _ = x.shape
    nh = tm // HALO
    last = s // HALO - 1
    w_g = w_gu[:, :D_FF].astype(BF16)
    w_u = w_gu[:, D_FF:].astype(BF16)
    weights = (norm_g[None, :], w_g, w_u, conv_w, conv_b[None, :], w_down.astype(BF16))
    kern = functools.partial(_ffn_kernel, tm=tm, n_chunks=n_chunks)
    return pl.pallas_call(
        kern,
        grid=(b, s // tm),
        in_specs=[
            pl.BlockSpec((None, tm, D_MODEL), lambda bi, i: (bi, i, 0)),
            pl.BlockSpec((None, HALO, D_MODEL), lambda bi, i: (bi, jnp.maximum(i * nh - 1, 0), 0)),
            pl.BlockSpec((None, HALO, D_MODEL), lambda bi, i: (bi, jnp.minimum((i + 1) * nh, last), 0)),
        ] + [_const_spec(w.shape) for w in weights],
        out_specs=pl.BlockSpec((None, tm, D_MODEL), lambda bi, i: (bi, i, 0)),
        out_shape=jax.ShapeDtypeStruct(x.shape, F32),
        compiler_params=_params(("parallel", "parallel")),
        name="ffn",
    )(x, x, x, *weights)


def _tiles(s):
    return dict(tm=min(512, s), tm_post=min(1024, s), tq_mla=min(8192, s), tq_diff=min(2048, s),
                tq_mla_online=min(1024, s), tq_diff_online=min(512, s))


def _mixer_weights(mla_p, diff_p):
    layers = []
    for i in range(DEPTH):
        j = i // 2
        if i % 2 == 0:
            weights, w_o_mix, bound = _mla_weights(*[p[j] for p in mla_p])
            layers.append(dict(weights=weights, w_o_mix=w_o_mix, bound=bound))
        else:
            norm_g, w_qkv, q_g, k_g, lam_p, sub_g, w_o = [p[j] for p in diff_p]
            bound = _score_bound(DIFF_HD, q_g, k_g)
            k_tab, q_tab = _diff_bias_tables(bound)
            layers.append(dict(pre=(norm_g, w_qkv, q_g, k_g, k_tab), lam_p=lam_p, sub_g=sub_g,
                               q_tab=q_tab, w_o_mix=w_o.astype(BF16), bound=bound))
    return layers


def _trunk(x, mem, mixers, xa_p, ffn_p):
    s = x.shape[1]
    assert s % KV_CHUNK == 0
    t = _tiles(s)
    xa_norm, xa_mem_norm, xa_w_q, xa_w_kv, xa_q_norm, xa_k_norm, xa_w_o = xa_p
    k_mem, v_mem = _mem_kv(mem, xa_mem_norm, xa_w_kv, xa_k_norm)
    cos, sin = _rope_tables(s)
    for i in range(DEPTH):
        mx = mixers[i]
        static_ok = mx["bound"] <= MAX_STATIC_BOUND
        if i % 2 == 0:
            q, k, vt = _mla_pre(x, mx["weights"], cos, sin)
            o_mix = lax.cond(static_ok,
                             lambda q, k, vt: _flash_mla_static(q, k, vt, t["tq_mla"]),
                             lambda q, k, vt: _flash_mla(q, k, vt, t["tq_mla_online"]), q, k, vt)
        else:
            lambda_init = 0.8 - 0.6 * math.exp(-0.3 * i)
            q, k, vt = _diff_pre(x, *mx["pre"])
            flash = lambda static: functools.partial(
                _flash_diff, lambda_init=lambda_init, static_shift=static,
                tq=t["tq_diff"] if static else t["tq_diff_online"])
            o_mix = lax.cond(static_ok, flash(True), flash(False),
                             q, k, vt, mx["lam_p"], mx["sub_g"], mx["q_tab"])
        x = _post_xattn(x, o_mix, mx["w_o_mix"], xa_norm[i], xa_w_q[i], xa_q_norm[i], k_mem, v_mem, i,
                        xa_w_o[i], t["tm_post"])
        x = _ffn(x, *[p[i] for p in ffn_p], t["tm"])
    return x


def kernel(x_prompt, x_sample, mem_prompt, mem_sample, mla_norm, mla_w_down, mla_q_lat_norm, mla_kv_lat_norm, mla_w_uq, mla_w_ukv, mla_q_norm, mla_k_norm, mla_w_o, diff_norm, diff_w_qkv, diff_q_norm, diff_k_norm, diff_lambda, diff_sub_norm, diff_w_o, xa_norm, xa_mem_norm, xa_w_q, xa_w_kv, xa_q_norm, xa_k_norm, xa_w_o, ffn_norm, ffn_w_gu, ffn_conv_w, ffn_conv_b, ffn_w_down):
    mla_p = (mla_norm, mla_w_down, mla_q_lat_norm, mla_kv_lat_norm, mla_w_uq, mla_w_ukv,
             mla_q_norm, mla_k_norm, mla_w_o)
    diff_p = (diff_norm, diff_w_qkv, diff_q_norm, diff_k_norm, diff_lambda, diff_sub_norm, diff_w_o)
    xa_p = (xa_norm, xa_mem_norm, xa_w_q, xa_w_kv, xa_q_norm, xa_k_norm, xa_w_o)
    ffn_p = (ffn_norm, ffn_w_gu, ffn_conv_w, ffn_conv_b, ffn_w_down)
    mixers = _mixer_weights(mla_p, diff_p)
    y_prompt = _trunk(x_prompt, mem_prompt, mixers, xa_p, ffn_p)
    y_sample = _trunk(x_sample, mem_sample, mixers, xa_p, ffn_p)
    return (y_prompt, y_sample)
```

```python
import functools
import math

import jax
import jax.numpy as jnp
from jax import lax
from jax.experimental import pallas as pl
from jax.experimental.pallas import tpu as pltpu

D_MODEL = 1024
DEPTH = 4
N_MEM = 256
EPS = 1e-6
MLA_HEADS = 8
MLA_Q_LORA = 384
MLA_KV_LORA = 256
MLA_NOPE = 64
MLA_ROPE = 32
MLA_QK = MLA_NOPE + MLA_ROPE
MLA_V = 64
ROPE_THETA = 10000.0
DIFF_HEADS = 8
DIFF_HD = 64
XA_HEADS = 4
XA_HD = D_MODEL // XA_HEADS
D_FF = 2816
CONV_W = 3

LANE = 128
HALO = 8
BF16_ROWS = 16
MXU_DIM = 256
KV_CHUNK = 512
DIFF_VT_ROWS = LANE + BF16_ROWS
MLA_VT_ROWS = MLA_V + BF16_ROWS
SHIFT_LANE = MLA_QK
SCORE_MARGIN = 1.02
MAX_STATIC_BOUND = 40.0
DIFF_AUG_PARTS = 3
POS_SPLIT = 64
LOG2E = math.log2(math.e)
NEG_BIG = -1e30
VMEM_LIMIT = 56 * 1024 * 1024

F32 = jnp.float32
BF16 = jnp.bfloat16


def _const_spec(shape):
    nd = len(shape)
    return pl.BlockSpec(shape, lambda *_: (0,) * nd, pipeline_mode=pl.Buffered(1))


def _params(sem):
    return pltpu.CompilerParams(dimension_semantics=sem, vmem_limit_bytes=VMEM_LIMIT)


def _rms(x, g):
    return x * lax.rsqrt(jnp.mean(x * x, axis=-1, keepdims=True) + EPS) * g


def _dot(a, b):
    return jnp.dot(a, b, preferred_element_type=F32)


def _dot_nt(a, b):
    return lax.dot_general(a, b, (((1,), (1,)), ((), ())), preferred_element_type=F32)


def _mem_kv_kernel(mem_ref, g_ref, w_ref, kg_ref, k_ref, v_ref):
    h = _rms(mem_ref[...], g_ref[...]).astype(BF16)
    kv = _dot(h, w_ref[...])
    kg = kg_ref[...]
    for hd in range(XA_HEADS):
        sl = slice(hd * XA_HD, (hd + 1) * XA_HD)
        k_ref[:, sl] = _rms(kv[:, sl], kg).astype(BF16)
    v_ref[...] = kv[:, D_MODEL:].astype(BF16)


def _mem_kv(mem, mem_g, w_kv, k_g):
    bm = mem.shape[0]
    out = jax.ShapeDtypeStruct((DEPTH, bm, N_MEM, D_MODEL), BF16)
    return pl.pallas_call(
        _mem_kv_kernel,
        grid=(DEPTH, bm),
        in_specs=[
            pl.BlockSpec((None, N_MEM, D_MODEL), lambda l, b: (b, 0, 0)),
            pl.BlockSpec((None, 1, D_MODEL), lambda l, b: (l, 0, 0)),
            pl.BlockSpec((None, D_MODEL, 2 * D_MODEL), lambda l, b: (l, 0, 0)),
            pl.BlockSpec((None, 1, XA_HD), lambda l, b: (l, 0, 0)),
        ],
        out_specs=[
            pl.BlockSpec((None, None, N_MEM, D_MODEL), lambda l, b: (l, b, 0, 0)),
            pl.BlockSpec((None, None, N_MEM, D_MODEL), lambda l, b: (l, b, 0, 0)),
        ],
        out_shape=[out, out],
        compiler_params=_params(("arbitrary", "arbitrary")),
        name="mem_kv",
    )(mem, mem_g[:, None, :], w_kv.astype(BF16), k_g[:, None, :])


def _mla_pre_kernel(x_ref, ng_ref, wd_ref, qlg_ref, kvlg_ref, wuq_ref, wuk_ref, wvt_ref,
                    gq_ref, gk_ref, qshift_ref, cos_ref, sin_ref, q_ref, k_ref, vt_ref):
    h = _rms(x_ref[...], ng_ref[...]).astype(BF16)
    down = _dot(h, wd_ref[...])
    c_q = _rms(down[:, :MLA_Q_LORA], qlg_ref[...]).astype(BF16)
    kv0 = MLA_Q_LORA
    c_kv = _rms(down[:, kv0:kv0 + MLA_KV_LORA], kvlg_ref[...]).astype(BF16)
    kr = down[:, kv0 + MLA_KV_LORA:kv0 + MLA_KV_LORA + LANE]
    kr_rot = down[:, kv0 + MLA_KV_LORA + LANE:]
    qq = _dot(c_q, wuq_ref[...])
    kn = _dot(c_kv, wuk_ref[...])
    v_t = _dot_nt(wvt_ref[...], c_kv)
    ones_row = (lax.broadcasted_iota(jnp.int32, (MLA_VT_ROWS, v_t.shape[1]), 0) == MLA_V).astype(F32)
    cos = cos_ref[...]
    sin = sin_ref[...]
    gq = gq_ref[...]
    gk = gk_ref[...]
    hw = MLA_HEADS * LANE
    lane = lax.broadcasted_iota(jnp.int32, (1, LANE), 1)
    k_one = (lane == SHIFT_LANE).astype(F32)
    q_shift = qshift_ref[...]
    q_scale = (MLA_QK ** -0.5) * LOG2E
    for hd in range(MLA_HEADS):
        sl = slice(hd * LANE, (hd + 1) * LANE)
        qh = qq[:, sl]
        rq = lax.rsqrt(jnp.sum(qh * qh, axis=-1, keepdims=True) * (1.0 / MLA_QK) + EPS)
        q_rot = qq[:, hw + hd * LANE:hw + (hd + 1) * LANE]
        q_ref[:, sl] = ((rq * q_scale) * (qh * gq * cos + q_rot * sin) + q_shift).astype(BF16)
        kh = kn[:, sl] + kr
        rk = lax.rsqrt(jnp.sum(kh * kh, axis=-1, keepdims=True) * (1.0 / MLA_QK) + EPS)
        k_ref[:, sl] = (rk * (kh * gk * cos + kr_rot * sin) + k_one).astype(BF16)
        vt_ref[hd] = (v_t[hd * MLA_VT_ROWS:(hd + 1) * MLA_VT_ROWS] + ones_row).astype(BF16)


def _rot_half_cols(w):
    half = MLA_ROPE // 2
    return jnp.concatenate([-w[..., half:], w[..., :half]], axis=-1)


def _mla_weights(norm_g, w_down, q_lat_g, kv_lat_g, w_uq, w_ukv, q_g, k_g, w_o):
    nq, nkv = MLA_Q_LORA, MLA_KV_LORA
    w_dq, w_dkv, w_kr = w_down[:, :nq], w_down[:, nq:nq + nkv], w_down[:, nq + nkv:]
    pad_lo = jnp.zeros((D_MODEL, MLA_NOPE), F32)
    pad_hi = jnp.zeros((D_MODEL, LANE - MLA_QK), F32)
    kr_blk = jnp.concatenate([pad_lo, w_kr, pad_hi], axis=1)
    krot_blk = jnp.concatenate([pad_lo, _rot_half_cols(w_kr * k_g[MLA_NOPE:]), pad_hi], axis=1)
    wd = jnp.concatenate([w_dq, w_dkv, kr_blk, krot_blk], axis=1).astype(BF16)

    wq = w_uq.reshape(nq, MLA_HEADS, MLA_QK)
    zq = jnp.zeros((nq, MLA_HEADS, LANE - MLA_QK), F32)
    wq_main = jnp.concatenate([wq, zq], axis=-1).reshape(nq, MLA_HEADS * LANE)
    wq_rot = jnp.concatenate(
        [jnp.zeros((nq, MLA_HEADS, MLA_NOPE), F32),
         _rot_half_cols(wq[..., MLA_NOPE:] * q_g[MLA_NOPE:]), zq], axis=-1
    ).reshape(nq, MLA_HEADS * LANE)
    wuq = jnp.concatenate([wq_main, wq_rot], axis=1).astype(BF16)

    wkv = w_ukv.reshape(nkv, MLA_HEADS, MLA_NOPE + MLA_V)
    zk = jnp.zeros((nkv, MLA_HEADS, LANE - MLA_NOPE), F32)
    wuk = jnp.concatenate([wkv[..., :MLA_NOPE], zk], axis=-1).reshape(nkv, MLA_HEADS * LANE).astype(BF16)
    wvt = jnp.transpose(wkv[..., MLA_NOPE:], (1, 2, 0))
    wvt = jnp.concatenate([wvt, jnp.zeros((MLA_HEADS, MLA_VT_ROWS - MLA_V, nkv), F32)], axis=1)
    wvt = wvt.reshape(MLA_HEADS * MLA_VT_ROWS, nkv).astype(BF16)

    zg = jnp.zeros((LANE - MLA_QK,), F32)
    gq = jnp.concatenate([q_g, zg])[None, :]
    gk = jnp.concatenate([k_g, zg])[None, :]

    wo = w_o.reshape(MLA_HEADS, MLA_V, D_MODEL)
    wo = jnp.concatenate([wo, jnp.zeros((MLA_HEADS, LANE - MLA_V, D_MODEL), F32)], axis=1)
    wo = wo.reshape(MLA_HEADS * LANE, D_MODEL).astype(BF16)
    bound = _score_bound(MLA_QK, q_g, k_g)
    q_shift = -bound * (jnp.arange(LANE) == SHIFT_LANE).astype(F32)[None, :]
    weights = (norm_g[None, :], wd, q_lat_g[None, :], kv_lat_g[None, :], wuq, wuk, wvt, gq, gk, q_shift)
    return weights, wo, bound


def _score_bound(dim, q_g, k_g):
    return SCORE_MARGIN * math.sqrt(dim) * LOG2E * jnp.max(jnp.abs(q_g)) * jnp.max(jnp.abs(k_g))


def _rope_tables(seq):
    inv = ROPE_THETA ** (-jnp.arange(0, MLA_ROPE, 2, dtype=F32) / MLA_ROPE)
    ang = jnp.arange(seq, dtype=F32)[:, None] * inv[None, :]
    ang = jnp.concatenate([ang, ang], axis=-1)
    ones = jnp.ones((seq, MLA_NOPE), F32)
    zlo = jnp.zeros((seq, MLA_NOPE), F32)
    zhi = jnp.zeros((seq, LANE - MLA_QK), F32)
    cos = jnp.concatenate([ones, jnp.cos(ang), zhi], axis=1)
    sin = jnp.concatenate([zlo, jnp.sin(ang), zhi], axis=1)
    return cos, sin


def _vt_shape_and_spec(b, heads, rows, s):
    shape = jax.ShapeDtypeStruct((b, heads, s // KV_CHUNK, rows, KV_CHUNK), BF16)
    spec = pl.BlockSpec((None, heads, None, rows, KV_CHUNK), lambda bi, i: (bi, 0, i, 0, 0))
    return shape, spec


def _mla_pre(x, weights, cos, sin):
    b, s, _ = x.shape
    tm = KV_CHUNK
    hw = MLA_HEADS * LANE
    out = jax.ShapeDtypeStruct((b, s, hw), BF16)
    vt_shape, vt_spec = _vt_shape_and_spec(b, MLA_HEADS, MLA_VT_ROWS, s)
    tile = lambda bi, i: (bi, i, 0)
    w_specs = [_const_spec(w.shape) for w in weights]
    return pl.pallas_call(
        _mla_pre_kernel,
        grid=(b, s // tm),
        in_specs=[pl.BlockSpec((None, tm, D_MODEL), tile)] + w_specs + [
            pl.BlockSpec((tm, LANE), lambda bi, i: (i, 0)),
            pl.BlockSpec((tm, LANE), lambda bi, i: (i, 0)),
        ],
        out_specs=[pl.BlockSpec((None, tm, hw), tile)] * 2 + [vt_spec],
        out_shape=[out, out, vt_shape],
        compiler_params=_params(("parallel", "parallel")),
        name="mla_pre",
    )(x, *weights, cos, sin)


def _flash_core(scores, vt_ref, s_buf, mx_buf, p_buf, al_buf, m_sc, acc_sc, n_kv):
    assert n_kv >= 2 and n_kv % 2 == 0

    def issue_scores(j, slot):
        s = scores(j)
        s_buf[slot] = s
        mx_buf[slot] = jnp.max(s, axis=0, keepdims=True)

    def softmax(slot):
        m_old = m_sc[...]
        m_new = jnp.maximum(m_old, mx_buf[slot])
        p_buf[slot] = jnp.exp2(s_buf[slot] - m_new).astype(BF16)
        al_buf[slot] = jnp.exp2(m_old - m_new)
        m_sc[...] = m_new

    def values(j, slot):
        acc_sc[...] = al_buf[slot] * acc_sc[...] + _dot(vt_ref[j], p_buf[slot])

    m_sc[...] = jnp.full(m_sc.shape, NEG_BIG, F32)
    acc_sc[...] = jnp.zeros(acc_sc.shape, F32)
    issue_scores(0, 0)
    issue_scores(1, 1)
    softmax(0)

    def pair(i, carry):
        j = 2 * i + 1
        issue_scores(j + 1, 0)
        softmax(1)
        values(j - 1, 0)
        issue_scores(j + 2, 1)
        softmax(0)
        values(j, 1)
        return carry

    lax.fori_loop(0, (n_kv - 2) // 2, pair, 0)
    softmax(1)
    values(n_kv - 2, 0)
    values(n_kv - 1, 1)


def _flash_scratch(rows, nq):
    return [
        pltpu.VMEM((2, KV_CHUNK, nq), F32),
        pltpu.VMEM((2, 1, nq), F32),
        pltpu.VMEM((2, KV_CHUNK, nq), BF16),
        pltpu.VMEM((2, 1, nq), F32),
        pltpu.VMEM((1, nq), F32),
        pltpu.VMEM((rows, nq), F32),
    ]


def _key_chunk(k_ref, j):
    return k_ref[pl.ds(pl.multiple_of(j * KV_CHUNK, KV_CHUNK), KV_CHUNK), :]


def _mla_store(o_t, o_ref):
    o_ref[...] = jnp.concatenate([o_t, jnp.zeros((LANE - MLA_V, o_t.shape[1]), F32)], axis=0).T.astype(BF16)


def _mla_finalize(acc_sc, o_ref):
    acc = acc_sc[...]
    _mla_store(acc[:MLA_V] / acc[MLA_V:MLA_V + 1], o_ref)


def _flash_mla_kernel(q_ref, k_ref, vt_ref, o_ref, *scratch, n_kv):
    acc_sc = scratch[-1]
    _flash_core(lambda j: _dot_nt(_key_chunk(k_ref, j), q_ref[...]), vt_ref, *scratch, n_kv)
    _mla_finalize(acc_sc, o_ref)


def _flash_mla(q, k, vt, tq):
    b, s, hw = q.shape
    n_kv = s // KV_CHUNK
    kern = functools.partial(_flash_mla_kernel, n_kv=n_kv)
    return pl.pallas_call(
        kern,
        grid=(b, MLA_HEADS, s // tq),
        in_specs=[
            pl.BlockSpec((None, tq, LANE), lambda bi, h, i: (bi, i, h)),
            pl.BlockSpec((None, s, LANE), lambda bi, h, i: (bi, 0, h)),
            pl.BlockSpec((None, None, n_kv, MLA_VT_ROWS, KV_CHUNK), lambda bi, h, i: (bi, h, 0, 0, 0)),
        ],
        out_specs=pl.BlockSpec((None, tq, LANE), lambda bi, h, i: (bi, i, h)),
        out_shape=jax.ShapeDtypeStruct((b, s, hw), BF16),
        scratch_shapes=_flash_scratch(MLA_VT_ROWS, tq),
        compiler_params=_params(("parallel", "parallel", "arbitrary")),
        name="flash_mla",
    )(q, k, vt)


def _static_core(first_probs, probs, chunk_of, vt_ref, p_buf, acc_sc, n_kv):
    assert n_kv >= 2 and n_kv % 2 == 0

    def values(t, slot, first=False):
        pv = _dot(vt_ref[chunk_of(t)], p_buf[slot])
        acc_sc[...] = pv if first else acc_sc[...] + pv

    p_buf[0] = first_probs()
    p_buf[1] = probs(1)
    values(0, 0, first=True)

    for t in range(1, n_kv - 1, 2):
        p_buf[0] = probs(t + 1)
        values(t, 1)
        p_buf[1] = probs(t + 2)
        values(t + 1, 0)
    values(n_kv - 1, 1)


def _static_scratch(rows, nq):
    return [pltpu.VMEM((2, KV_CHUNK, nq), BF16), pltpu.VMEM((rows, nq), F32)]


def _query_tile_loop(n_tiles, tile_fn):
    def body(i, carry):
        tile_fn(i)
        return carry
    lax.fori_loop(0, n_tiles, body, 0)


def _flash_mla_static_kernel(q_ref, k_ref, vt_ref, o_ref, p_buf, acc_sc, *, tq, n_kv):
    def tile(i):
        rows = pl.ds(pl.multiple_of(i * tq, tq), tq)
        probs = lambda t: jnp.exp2(_dot_nt(_key_chunk(k_ref, t), q_ref[rows, :])).astype(BF16)
        _static_core(lambda: probs(0), probs, lambda t: t, vt_ref, p_buf, acc_sc, n_kv)
        _mla_finalize(acc_sc, o_ref.at[rows, :])

    _query_tile_loop(q_ref.shape[0] // tq, tile)


def _flash_mla_static(q, k, vt, tq):
    b, s, hw = q.shape
    n_kv = s // KV_CHUNK
    kern = functools.partial(_flash_mla_static_kernel, tq=tq, n_kv=n_kv)
    head_slab = pl.BlockSpec((None, s, LANE), lambda bi, h: (bi, 0, h))
    return pl.pallas_call(
        kern,
        grid=(b, MLA_HEADS),
        in_specs=[
            head_slab,
            head_slab,
            pl.BlockSpec((None, None, n_kv, MLA_VT_ROWS, KV_CHUNK), lambda bi, h: (bi, h, 0, 0, 0)),
        ],
        out_specs=head_slab,
        out_shape=jax.ShapeDtypeStruct((b, s, hw), BF16),
        scratch_shapes=_static_scratch(MLA_VT_ROWS, tq),
        compiler_params=_params(("parallel", "parallel")),
        name="flash_mla_static",
    )(q, k, vt)


def _pos_lanes(pos0, rows, first_lane):
    pos = pos0 + lax.broadcasted_iota(jnp.int32, (rows, LANE), 0)
    lane = lax.broadcasted_iota(jnp.int32, (rows, LANE), 1) - first_lane
    hi = lax.shift_right_logical(pos, int(math.log2(POS_SPLIT))).astype(F32)
    lo = (pos & (POS_SPLIT - 1)).astype(F32)
    p = DIFF_AUG_PARTS
    in_hi = (lane >= 0) & (lane < p)
    in_lo = (lane >= p) & (lane < 2 * p)
    return jnp.where(in_hi, hi, jnp.where(in_lo, lo, 0.0))


def _diff_pre_kernel(x_ref, ng_ref, w_ref, wvt_ref, gq_ref, gk_ref, kaug_ref, q_ref, k_ref, vt_ref):
    h = _rms(x_ref[...], ng_ref[...]).astype(BF16)
    qkv = _dot(h, w_ref[...])
    v_t = _dot_nt(wvt_ref[...], h)
    hw = DIFF_HEADS * LANE
    lo = lax.broadcasted_iota(jnp.int32, (1, LANE), 1) < DIFF_HD
    q_scale = (DIFF_HD ** -0.5) * LOG2E
    tm = qkv.shape[0]
    k_pos = _pos_lanes(pl.program_id(1) * tm, tm, 0)
    ones_row = (lax.broadcasted_iota(jnp.int32, (DIFF_VT_ROWS, tm), 0) == LANE).astype(F32)

    def half_norm(t, g):
        t2 = t * t
        ss_lo = jnp.sum(jnp.where(lo, t2, 0.0), axis=-1, keepdims=True)
        ss_hi = jnp.sum(jnp.where(lo, 0.0, t2), axis=-1, keepdims=True)
        r = lax.rsqrt(jnp.where(lo, ss_lo, ss_hi) * (1.0 / DIFF_HD) + EPS)
        return t * r * g

    for hd in range(DIFF_HEADS):
        sl = slice(hd * LANE, (hd + 1) * LANE)
        q_ref[:, sl] = (half_norm(qkv[:, sl], gq_ref[...]) * q_scale).astype(BF16)
        ks = slice(2 * hd * LANE, (2 * hd + 1) * LANE)
        k_ref[:, ks] = half_norm(qkv[:, hw + hd * LANE:hw + (hd + 1) * LANE], gk_ref[...]).astype(BF16)
        k_ref[:, (2 * hd + 1) * LANE:(2 * hd + 2) * LANE] = (k_pos + kaug_ref[hd]).astype(BF16)
        vt_ref[hd] = (v_t[hd * DIFF_VT_ROWS:(hd + 1) * DIFF_VT_ROWS] + ones_row).astype(BF16)


def _slope_pieces():
    slopes = 2.0 ** (-8.0 * jnp.arange(1, DIFF_HEADS + 1, dtype=F32) / DIFF_HEADS) * LOG2E
    pieces, rest = [], slopes
    for _ in range(DIFF_AUG_PARTS):
        piece = rest.astype(BF16).astype(F32)
        pieces.append(piece)
        rest = rest - piece
    return jnp.stack(pieces, axis=1)


def _diff_bias_tables(bound):
    p = DIFF_AUG_PARTS
    pieces = _slope_pieces()
    zeros = jnp.zeros((DIFF_HEADS, LANE - 4 * p - 1), F32)
    one = jnp.ones((DIFF_HEADS, 1), F32)
    zp = jnp.zeros((DIFF_HEADS, p), F32)
    k_tab = jnp.concatenate([zp, zp, POS_SPLIT * pieces, pieces, one, zeros], axis=1)[:, None, :]
    shift = -bound * one
    after = jnp.concatenate([-POS_SPLIT * pieces, -pieces, zp, zp, shift, zeros], axis=1)
    before = jnp.concatenate([POS_SPLIT * pieces, pieces, zp, zp, shift, zeros], axis=1)
    diag = jnp.concatenate([zp, zp, zp, zp, shift, zeros], axis=1)
    return k_tab, jnp.stack([after, before, diag], axis=1)


def _diff_pre(x, norm_g, w_qkv, q_g, k_g, k_tab):
    b, s, _ = x.shape
    tm = KV_CHUNK
    hw = DIFF_HEADS * LANE
    out = jax.ShapeDtypeStruct((b, s, hw), BF16)
    out_k = jax.ShapeDtypeStruct((b, s, 2 * hw), BF16)
    vt_shape, vt_spec = _vt_shape_and_spec(b, DIFF_HEADS, DIFF_VT_ROWS, s)
    tile = lambda bi, i: (bi, i, 0)
    wvt = w_qkv[:, 2 * hw:].T.reshape(DIFF_HEADS, LANE, D_MODEL)
    wvt = jnp.concatenate([wvt, jnp.zeros((DIFF_HEADS, DIFF_VT_ROWS - LANE, D_MODEL), F32)], axis=1)
    wvt = wvt.reshape(DIFF_HEADS * DIFF_VT_ROWS, D_MODEL).astype(BF16)
    weights = (norm_g[None, :], w_qkv[:, :2 * hw].astype(BF16), wvt, q_g.reshape(1, LANE),
               k_g.reshape(1, LANE), k_tab)
    return pl.pallas_call(
        _diff_pre_kernel,
        grid=(b, s // tm),
        in_specs=[pl.BlockSpec((None, tm, D_MODEL), tile)] + [_const_spec(w.shape) for w in weights],
        out_specs=[pl.BlockSpec((None, tm, hw), tile), pl.BlockSpec((None, tm, 2 * hw), tile), vt_spec],
        out_shape=[out, out_k, vt_shape],
        compiler_params=_params(("parallel", "parallel")),
        name="diff_pre",
    )(x, *weights)


def _flash_diff_kernel(slope_ref, q_ref, k_ref, vt_ref, dmat_ref, lam_ref, subg_ref, o_ref,
                       qcat_sc, *scratch, tq, n_kv, lambda_init):
    acc_sc = scratch[-1]
    lo = lax.broadcasted_iota(jnp.int32, (1, LANE), 1) < DIFF_HD
    q = q_ref[...]
    zero = jnp.zeros_like(q)
    qcat_sc[:tq] = jnp.where(lo, q, zero)
    qcat_sc[tq:] = jnp.where(lo, zero, q)
    neg_slope = slope_ref[pl.program_id(1)]
    q0 = pl.program_id(2) * tq

    def scores(j):
        delta = (j * KV_CHUNK - q0).astype(F32)
        bias = jnp.abs(dmat_ref[...] + delta) * neg_slope
        return _dot_nt(_key_chunk(k_ref, j), qcat_sc[...]) + jnp.concatenate([bias, bias], axis=1)

    _flash_core(scores, vt_ref, *scratch, n_kv)
    _diff_finalize(acc_sc, lam_ref, subg_ref, o_ref, tq, lambda_init)


def _diff_finalize(acc_sc, lam_ref, subg_ref, o_ref, tq, lambda_init):
    acc = acc_sc[...]
    _diff_store(acc[:LANE] / acc[LANE:LANE + 1], lam_ref, subg_ref, o_ref, tq, lambda_init)


def _diff_store(o, lam_ref, subg_ref, o_ref, tq, lambda_init):
    lp = lam_ref[...]
    lam = (jnp.exp(jnp.sum(lp[0:1] * lp[1:2], axis=-1, keepdims=True))
           - jnp.exp(jnp.sum(lp[2:3] * lp[3:4], axis=-1, keepdims=True)) + lambda_init)
    o = (o[:, :tq] - lam * o[:, tq:]).T
    o = _rms(o, subg_ref[...]) * (1.0 - lambda_init)
    o_ref[...] = o.astype(BF16)


def _flash_diff_static_kernel(slope_ref, q_ref, k_ref, vt_ref, dmat_ref, lam_ref, subg_ref, qtab_ref,
                              o_ref, qz_sc, aug_sc, p_buf, acc_sc, *, tq, n_kv, lambda_init):
    tile = functools.partial(_flash_diff_static_tile, slope_ref, q_ref, k_ref, vt_ref, dmat_ref, lam_ref,
                             subg_ref, qtab_ref, o_ref, qz_sc, aug_sc, p_buf, acc_sc, tq, n_kv,
                             lambda_init)
    _query_tile_loop(q_ref.shape[0] // tq, tile)


def _flash_diff_static_tile(slope_ref, q_ref, k_ref, vt_ref, dmat_ref, lam_ref, subg_ref, qtab_ref,
                            o_ref, qz_sc, aug_sc, p_buf, acc_sc, tq, n_kv, lambda_init, i):
    lo = lax.broadcasted_iota(jnp.int32, (1, LANE), 1) < DIFF_HD
    q0 = pl.multiple_of(i * tq, tq)
    q = q_ref[pl.ds(q0, tq), :]
    zero = jnp.zeros_like(q)
    qz_sc[:tq] = jnp.where(lo, q, zero)
    qz_sc[tq:] = jnp.where(lo, zero, q)
    n_d = tq // KV_CHUNK
    q_pos = _pos_lanes(q0, tq, 2 * DIFF_AUG_PARTS)
    tab = qtab_ref[...]
    for kind, sign in enumerate((1.0, -1.0, 0.0)):
        aug_sc[kind] = (tab[kind:kind + 1] + sign * q_pos).astype(BF16)
    jd = lax.shift_right_logical(q0, int(math.log2(KV_CHUNK)))
    neg_slope = slope_ref[pl.program_id(1)]

    def with_bias_columns(aug):
        return jnp.concatenate([qz_sc[...], jnp.concatenate([aug, aug], axis=0)], axis=1)

    def diag_probs(d):
        kinds = [2 if e == d else (1 if e > d else 0) for e in range(n_d)]
        aug = jnp.concatenate([aug_sc[kind, e * KV_CHUNK:(e + 1) * KV_CHUNK, :]
                               for e, kind in enumerate(kinds)], axis=0)
        s = _dot_nt(_key_chunk(k_ref, jd + d), with_bias_columns(aug))
        bias = jnp.abs(dmat_ref[...]) * neg_slope
        cols = []
        for c in range(2):
            for e in range(n_d):
                blk = s[:, c * tq + e * KV_CHUNK:c * tq + (e + 1) * KV_CHUNK]
                cols.append(blk + bias if e == d else blk)
        return jnp.exp2(jnp.concatenate(cols, axis=1)).astype(BF16)

    def chunk_of(t):
        r = t - n_d
        return jnp.where(t < n_d, jd + t, r + n_d * (r >= jd).astype(jnp.int32))

    def probs(t):
        if isinstance(t, int) and t < n_d:
            return diag_probs(t)
        j = chunk_of(t)
        before = (j < jd).astype(jnp.int32)
        return jnp.exp2(_dot_nt(_key_chunk(k_ref, j), with_bias_columns(aug_sc[before]))).astype(BF16)

    _static_core(lambda: diag_probs(0), probs, chunk_of, vt_ref, p_buf, acc_sc, n_kv)
    _diff_finalize(acc_sc, lam_ref, subg_ref, o_ref.at[pl.ds(q0, tq), :], tq, lambda_init)


def _flash_diff(q, k, vt, lam_p, sub_g, q_tab, lambda_init, tq, static_shift):
    b, s, hw = q.shape
    n_kv = s // KV_CHUNK
    slopes = 2.0 ** (-8.0 * jnp.arange(1, DIFF_HEADS + 1, dtype=F32) / DIFF_HEADS)
    neg_slopes = -slopes * LOG2E
    d_cols = KV_CHUNK if static_shift else tq
    dmat = (jnp.arange(KV_CHUNK, dtype=F32)[:, None] - jnp.arange(d_cols, dtype=F32)[None, :])
    const = dict(pipeline_mode=pl.Buffered(1))
    origin = lambda *_: (0, 0)
    vt_spec = pl.BlockSpec((None, None, n_kv, DIFF_VT_ROWS, KV_CHUNK), lambda bi, h, *_: (bi, h, 0, 0, 0))
    small_specs = [
        pl.BlockSpec((KV_CHUNK, d_cols), origin, **const),
        pl.BlockSpec((4, DIFF_HD), origin, **const),
        pl.BlockSpec((1, LANE), origin, **const),
    ]
    args = [neg_slopes, q, k, vt, dmat, lam_p, sub_g[None, :]]
    if static_shift:
        assert tq % KV_CHUNK == 0
        kern = functools.partial(_flash_diff_static_kernel, tq=tq, n_kv=n_kv, lambda_init=lambda_init)
        grid = (b, DIFF_HEADS)
        q_spec = pl.BlockSpec((None, s, LANE), lambda bi, h, sl: (bi, 0, h))
        k_spec = pl.BlockSpec((None, s, 2 * LANE), lambda bi, h, sl: (bi, 0, h))
        in_specs = [q_spec, k_spec, vt_spec] + small_specs + [
            pl.BlockSpec((None, 3, LANE), lambda bi, h, sl: (h, 0, 0))]
        args.append(q_tab)
        scratch = ([pltpu.VMEM((2 * tq, LANE), BF16), pltpu.VMEM((3, tq, LANE), BF16)]
                   + _static_scratch(DIFF_VT_ROWS, 2 * tq))
        name = "flash_diff_static"
    else:
        kern = functools.partial(_flash_diff_kernel, tq=tq, n_kv=n_kv, lambda_init=lambda_init)
        grid = (b, DIFF_HEADS, s // tq)
        q_spec = pl.BlockSpec((None, tq, LANE), lambda bi, h, i, sl: (bi, i, h))
        k_spec = pl.BlockSpec((None, s, LANE), lambda bi, h, i, sl: (bi, 0, 2 * h))
        in_specs = [q_spec, k_spec, vt_spec] + small_specs
        scratch = [pltpu.VMEM((2 * tq, LANE), BF16)] + _flash_scratch(DIFF_VT_ROWS, 2 * tq)
        name = "flash_diff"
    grid_spec = pltpu.PrefetchScalarGridSpec(
        num_scalar_prefetch=1, grid=grid, in_specs=in_specs, out_specs=q_spec, scratch_shapes=scratch)
    return pl.pallas_call(
        kern,
        grid_spec=grid_spec,
        out_shape=jax.ShapeDtypeStruct((b, s, hw), BF16),
        compiler_params=_params(("parallel", "parallel") + (("arbitrary",) if len(grid) == 3 else ())),
        name=name,
    )(*args)


def _post_xattn_kernel(x_ref, om_ref, wom_ref, ng_ref, wq_ref, qg_ref, k_ref, v_ref, wo_ref,
                       y_ref, o_sc):
    x1 = x_ref[...] + _dot(om_ref[...], wom_ref[...])
    h = _rms(x1, ng_ref[...]).astype(BF16)
    q = _dot(h, wq_ref[...])
    qg = qg_ref[...] * ((XA_HD ** -0.5) * LOG2E)
    for hd in range(XA_HEADS):
        sl = slice(hd * XA_HD, (hd + 1) * XA_HD)
        qh = _rms(q[:, sl], qg).astype(BF16)
        s = _dot_nt(qh, k_ref[:, sl])
        p = jnp.exp2(s - jnp.max(s, axis=-1, keepdims=True))
        l = jnp.sum(p, axis=-1, keepdims=True)
        o_sc[:, sl] = (_dot(p.astype(BF16), v_ref[:, sl]) / l).astype(BF16)
    y_ref[...] = x1 + _dot(o_sc[...], wo_ref[...])


def _post_xattn(x, o_mix, w_o_mix, norm_g, w_q, q_g, k_mem, v_mem, layer, w_o, tm):
    b, s, _ = x.shape
    tile = lambda bi, i: (bi, i, 0)
    mem_spec = pl.BlockSpec((None, None, N_MEM, D_MODEL), lambda bi, i: (layer, bi, 0, 0))
    sq = (D_MODEL, D_MODEL)
    return pl.pallas_call(
        _post_xattn_kernel,
        grid=(b, s // tm),
        in_specs=[
            pl.BlockSpec((None, tm, D_MODEL), tile),
            pl.BlockSpec((None, tm, o_mix.shape[-1]), tile),
            _const_spec(w_o_mix.shape),
            _const_spec((1, D_MODEL)),
            _const_spec(sq),
            _const_spec((1, XA_HD)),
            mem_spec,
            mem_spec,
            _const_spec(sq),
        ],
        out_specs=pl.BlockSpec((None, tm, D_MODEL), tile),
        out_shape=jax.ShapeDtypeStruct(x.shape, F32),
        scratch_shapes=[pltpu.VMEM((tm, D_MODEL), BF16)],
        compiler_params=_params(("parallel", "parallel")),
        name="post_xattn",
    )(x, o_mix, w_o_mix, norm_g[None, :], w_q.astype(BF16), q_g[None, :], k_mem, v_mem,
      w_o.astype(BF16))


def _ffn_kernel(x_ref, prev_ref, next_ref, ng_ref, wg_ref, wu_ref, cw_ref, cb_ref, wd_ref, y_ref,
                *, tm, n_chunks):
    i = pl.program_id(1)
    keep_prev = (i > 0).astype(F32)
    keep_next = (i < pl.num_programs(1) - 1).astype(F32)
    x = x_ref[...]
    xe = jnp.concatenate([prev_ref[...] * keep_prev, x, next_ref[...] * keep_next], axis=0)
    he = _rms(xe, ng_ref[...]).astype(BF16)
    hc = he[HALO:HALO + tm]
    cw = cw_ref[...]
    cb = cb_ref[...]
    y = x
    n_tiles = D_FF // MXU_DIM
    bounds = [MXU_DIM * ((n_tiles * c + n_chunks - 1) // n_chunks) for c in range(n_chunks)] + [D_FF]
    for c in range(n_chunks):
        cs = slice(bounds[c], bounds[c + 1])
        ge = _dot(he, wg_ref[:, cs])
        u = _dot(hc, wu_ref[:, cs])
        g = (ge[HALO - 1:HALO - 1 + tm] * cw[0:1, cs] + ge[HALO:HALO + tm] * cw[1:2, cs]
             + ge[HALO + 1:HALO + 1 + tm] * cw[2:3, cs] + cb[:, cs])
        act = (g * jax.nn.sigmoid(g) * u).astype(BF16)
        y = y + _dot(act, wd_ref[cs, :])
    y_ref[...] = y


def _ffn(x, norm_g, w_gu, conv_w, conv_b, w_down, tm, n_chunks=1):
    b, s, _ = x.shape
    nh = tm // HALO
    last = s // HALO - 1
    w_g = w_gu[:, :D_FF].astype(BF16)
    w_u = w_gu[:, D_FF:].astype(BF16)
    weights = (norm_g[None, :], w_g, w_u, conv_w, conv_b[None, :], w_down.astype(BF16))
    kern = functools.partial(_ffn_kernel, tm=tm, n_chunks=n_chunks)
    return pl.pallas_call(
        kern,
        grid=(b, s // tm),
        in_specs=[
            pl.BlockSpec((None, tm, D_MODEL), lambda bi, i: (bi, i, 0)),
            pl.BlockSpec((None, HALO, D_MODEL), lambda bi, i: (bi, jnp.maximum(i * nh - 1, 0), 0)),
            pl.BlockSpec((None, HALO, D_MODEL), lambda bi, i: (bi, jnp.minimum((i + 1) * nh, last), 0)),
        ] + [_const_spec(w.shape) for w in weights],
        out_specs=pl.BlockSpec((None, tm, D_MODEL), lambda bi, i: (bi, i, 0)),
        out_shape=jax.ShapeDtypeStruct(x.shape, F32),
        compiler_params=_params(("parallel", "parallel")),
        name="ffn",
    )(x, x, x, *weights)


def _tiles(s):
    return dict(tm=min(512, s), tm_post=min(1024, s), tq_mla=min(4096, s), tq_diff=min(2048, s),
                tq_mla_online=min(1024, s), tq_diff_online=min(512, s))


def _mixer_weights(mla_p, diff_p):
    layers = []
    for i in range(DEPTH):
        j = i // 2
        if i % 2 == 0:
            weights, w_o_mix, bound = _mla_weights(*[p[j] for p in mla_p])
            layers.append(dict(weights=weights, w_o_mix=w_o_mix, bound=bound))
        else:
            norm_g, w_qkv, q_g, k_g, lam_p, sub_g, w_o = [p[j] for p in diff_p]
            bound = _score_bound(DIFF_HD, q_g, k_g)
            k_tab, q_tab = _diff_bias_tables(bound)
            layers.append(dict(pre=(norm_g, w_qkv, q_g, k_g, k_tab), lam_p=lam_p, sub_g=sub_g,
                               q_tab=q_tab, w_o_mix=w_o.astype(BF16), bound=bound))
    return layers


def _trunk(x, mem, mixers, xa_p, ffn_p):
    s = x.shape[1]
    assert s % KV_CHUNK == 0
    t = _tiles(s)
    xa_norm, xa_mem_norm, xa_w_q, xa_w_kv, xa_q_norm, xa_k_norm, xa_w_o = xa_p
    k_mem, v_mem = _mem_kv(mem, xa_mem_norm, xa_w_kv, xa_k_norm)
    cos, sin = _rope_tables(s)
    for i in range(DEPTH):
        mx = mixers[i]
        static_ok = mx["bound"] <= MAX_STATIC_BOUND
        if i % 2 == 0:
            q, k, vt = _mla_pre(x, mx["weights"], cos, sin)
            o_mix = lax.cond(static_ok,
                             lambda q, k, vt: _flash_mla_static(q, k, vt, t["tq_mla"]),
                             lambda q, k, vt: _flash_mla(q, k, vt, t["tq_mla_online"]), q, k, vt)
        else:
            lambda_init = 0.8 - 0.6 * math.exp(-0.3 * i)
            q, k, vt = _diff_pre(x, *mx["pre"])
            flash = lambda static: functools.partial(
                _flash_diff, lambda_init=lambda_init, static_shift=static,
                tq=t["tq_diff"] if static else t["tq_diff_online"])
            o_mix = lax.cond(static_ok, flash(True), flash(False),
                             q, k, vt, mx["lam_p"], mx["sub_g"], mx["q_tab"])
        x = _post_xattn(x, o_mix, mx["w_o_mix"], xa_norm[i], xa_w_q[i], xa_q_norm[i], k_mem, v_mem, i,
                        xa_w_o[i], t["tm_post"])
        x = _ffn(x, *[p[i] for p in ffn_p], t["tm"])
    return x


def kernel(x_prompt, x_sample, mem_prompt, mem_sample, mla_norm, mla_w_down, mla_q_lat_norm, mla_kv_lat_norm, mla_w_uq, mla_w_ukv, mla_q_norm, mla_k_norm, mla_w_o, diff_norm, diff_w_qkv, diff_q_norm, diff_k_norm, diff_lambda, diff_sub_norm, diff_w_o, xa_norm, xa_mem_norm, xa_w_q, xa_w_kv, xa_q_norm, xa_k_norm, xa_w_o, ffn_norm, ffn_w_gu, ffn_conv_w, ffn_conv_b, ffn_w_down):
    mla_p = (mla_norm, mla_w_down, mla_q_lat_norm, mla_kv_lat_norm, mla_w_uq, mla_w_ukv,
             mla_q_norm, mla_k_norm, mla_w_o)
    diff_p = (diff_norm, diff_w_qkv, diff_q_norm, diff_k_norm, diff_lambda, diff_sub_norm, diff_w_o)
    xa_p = (xa_norm, xa_mem_norm, xa_w_q, xa_w_kv, xa_q_norm, xa_k_norm, xa_w_o)
    ffn_p = (ffn_norm, ffn_w_gu, ffn_conv_w, ffn_conv_b, ffn_w_down)
    mixers = _mixer_weights(mla_p, diff_p)
    y_prompt = _trunk(x_prompt, mem_prompt, mixers, xa_p, ffn_p)
    y_sample = _trunk(x_sample, mem_sample, mixers, xa_p, ffn_p)
    return (y_prompt, y_sample)
```

```python
import functools
import math

import jax
import jax.numpy as jnp
from jax import lax
from jax.experimental import pallas as pl
from jax.experimental.pallas import tpu as pltpu

D_MODEL = 1024
DEPTH = 4
N_MEM = 256
EPS = 1e-6
MLA_HEADS = 8
MLA_Q_LORA = 384
MLA_KV_LORA = 256
MLA_NOPE = 64
MLA_ROPE = 32
MLA_QK = MLA_NOPE + MLA_ROPE
MLA_V = 64
ROPE_THETA = 10000.0
DIFF_HEADS = 8
DIFF_HD = 64
XA_HEADS = 4
XA_HD = D_MODEL // XA_HEADS
D_FF = 2816
CONV_W = 3

LANE = 128
HALO = 8
BF16_ROWS = 16
MXU_DIM = 256
KV_CHUNK = 512
DIFF_VT_ROWS = LANE + BF16_ROWS
MLA_VT_ROWS = MLA_V + BF16_ROWS
SHIFT_LANE = MLA_QK
SCORE_MARGIN = 1.02
MAX_STATIC_BOUND = 40.0
DIFF_AUG_PARTS = 3
POS_SPLIT = 64
LOG2E = math.log2(math.e)
NEG_BIG = -1e30
VMEM_LIMIT = 56 * 1024 * 1024

F32 = jnp.float32
BF16 = jnp.bfloat16


def _const_spec(shape):
    nd = len(shape)
    return pl.BlockSpec(shape, lambda *_: (0,) * nd, pipeline_mode=pl.Buffered(1))


def _params(sem):
    return pltpu.CompilerParams(dimension_semantics=sem, vmem_limit_bytes=VMEM_LIMIT)


def _rms(x, g):
    return x * lax.rsqrt(jnp.mean(x * x, axis=-1, keepdims=True) + EPS) * g


def _dot(a, b):
    return jnp.dot(a, b, preferred_element_type=F32)


def _dot_nt(a, b):
    return lax.dot_general(a, b, (((1,), (1,)), ((), ())), preferred_element_type=F32)


def _mem_kv_kernel(mem_ref, g_ref, w_ref, kg_ref, k_ref, v_ref):
    h = _rms(mem_ref[...], g_ref[...]).astype(BF16)
    kv = _dot(h, w_ref[...])
    kg = kg_ref[...]
    for hd in range(XA_HEADS):
        sl = slice(hd * XA_HD, (hd + 1) * XA_HD)
        k_ref[:, sl] = _rms(kv[:, sl], kg).astype(BF16)
    v_ref[...] = kv[:, D_MODEL:].astype(BF16)


def _mem_kv(mem, mem_g, w_kv, k_g):
    bm = mem.shape[0]
    out = jax.ShapeDtypeStruct((DEPTH, bm, N_MEM, D_MODEL), BF16)
    return pl.pallas_call(
        _mem_kv_kernel,
        grid=(DEPTH, bm),
        in_specs=[
            pl.BlockSpec((None, N_MEM, D_MODEL), lambda l, b: (b, 0, 0)),
            pl.BlockSpec((None, 1, D_MODEL), lambda l, b: (l, 0, 0)),
            pl.BlockSpec((None, D_MODEL, 2 * D_MODEL), lambda l, b: (l, 0, 0)),
            pl.BlockSpec((None, 1, XA_HD), lambda l, b: (l, 0, 0)),
        ],
        out_specs=[
            pl.BlockSpec((None, None, N_MEM, D_MODEL), lambda l, b: (l, b, 0, 0)),
            pl.BlockSpec((None, None, N_MEM, D_MODEL), lambda l, b: (l, b, 0, 0)),
        ],
        out_shape=[out, out],
        compiler_params=_params(("arbitrary", "arbitrary")),
        name="mem_kv",
    )(mem, mem_g[:, None, :], w_kv.astype(BF16), k_g[:, None, :])


def _mla_pre_kernel(x_ref, ng_ref, wd_ref, qlg_ref, kvlg_ref, wuq_ref, wuk_ref, wvt_ref,
                    gq_ref, gk_ref, qshift_ref, cos_ref, sin_ref, q_ref, k_ref, vt_ref):
    h = _rms(x_ref[...], ng_ref[...]).astype(BF16)
    down = _dot(h, wd_ref[...])
    c_q = _rms(down[:, :MLA_Q_LORA], qlg_ref[...]).astype(BF16)
    kv0 = MLA_Q_LORA
    c_kv = _rms(down[:, kv0:kv0 + MLA_KV_LORA], kvlg_ref[...]).astype(BF16)
    kr = down[:, kv0 + MLA_KV_LORA:kv0 + MLA_KV_LORA + LANE]
    kr_rot = down[:, kv0 + MLA_KV_LORA + LANE:]
    qq = _dot(c_q, wuq_ref[...])
    kn = _dot(c_kv, wuk_ref[...])
    v_t = _dot_nt(wvt_ref[...], c_kv)
    ones_row = (lax.broadcasted_iota(jnp.int32, (MLA_VT_ROWS, v_t.shape[1]), 0) == MLA_V).astype(F32)
    cos = cos_ref[...]
    sin = sin_ref[...]
    gq = gq_ref[...]
    gk = gk_ref[...]
    hw = MLA_HEADS * LANE
    lane = lax.broadcasted_iota(jnp.int32, (1, LANE), 1)
    k_one = (lane == SHIFT_LANE).astype(F32)
    q_shift = qshift_ref[...]
    q_scale = (MLA_QK ** -0.5) * LOG2E
    for hd in range(MLA_HEADS):
        sl = slice(hd * LANE, (hd + 1) * LANE)
        qh = qq[:, sl]
        rq = lax.rsqrt(jnp.sum(qh * qh, axis=-1, keepdims=True) * (1.0 / MLA_QK) + EPS)
        q_rot = qq[:, hw + hd * LANE:hw + (hd + 1) * LANE]
        q_ref[:, sl] = ((rq * q_scale) * (qh * gq * cos + q_rot * sin) + q_shift).astype(BF16)
        kh = kn[:, sl] + kr
        rk = lax.rsqrt(jnp.sum(kh * kh, axis=-1, keepdims=True) * (1.0 / MLA_QK) + EPS)
        k_ref[:, sl] = (rk * (kh * gk * cos + kr_rot * sin) + k_one).astype(BF16)
        _store_vt(vt_ref, hd, v_t[hd * MLA_VT_ROWS:(hd + 1) * MLA_VT_ROWS] + ones_row)


def _rot_half_cols(w):
    half = MLA_ROPE // 2
    return jnp.concatenate([-w[..., half:], w[..., :half]], axis=-1)


def _mla_weights(norm_g, w_down, q_lat_g, kv_lat_g, w_uq, w_ukv, q_g, k_g, w_o):
    nq, nkv = MLA_Q_LORA, MLA_KV_LORA
    w_dq, w_dkv, w_kr = w_down[:, :nq], w_down[:, nq:nq + nkv], w_down[:, nq + nkv:]
    pad_lo = jnp.zeros((D_MODEL, MLA_NOPE), F32)
    pad_hi = jnp.zeros((D_MODEL, LANE - MLA_QK), F32)
    kr_blk = jnp.concatenate([pad_lo, w_kr, pad_hi], axis=1)
    krot_blk = jnp.concatenate([pad_lo, _rot_half_cols(w_kr * k_g[MLA_NOPE:]), pad_hi], axis=1)
    wd = jnp.concatenate([w_dq, w_dkv, kr_blk, krot_blk], axis=1).astype(BF16)

    wq = w_uq.reshape(nq, MLA_HEADS, MLA_QK)
    zq = jnp.zeros((nq, MLA_HEADS, LANE - MLA_QK), F32)
    wq_main = jnp.concatenate([wq, zq], axis=-1).reshape(nq, MLA_HEADS * LANE)
    wq_rot = jnp.concatenate(
        [jnp.zeros((nq, MLA_HEADS, MLA_NOPE), F32),
         _rot_half_cols(wq[..., MLA_NOPE:] * q_g[MLA_NOPE:]), zq], axis=-1
    ).reshape(nq, MLA_HEADS * LANE)
    wuq = jnp.concatenate([wq_main, wq_rot], axis=1).astype(BF16)

    wkv = w_ukv.reshape(nkv, MLA_HEADS, MLA_NOPE + MLA_V)
    zk = jnp.zeros((nkv, MLA_HEADS, LANE - MLA_NOPE), F32)
    wuk = jnp.concatenate([wkv[..., :MLA_NOPE], zk], axis=-1).reshape(nkv, MLA_HEADS * LANE).astype(BF16)
    wvt = jnp.transpose(wkv[..., MLA_NOPE:], (1, 2, 0))
    wvt = jnp.concatenate([wvt, jnp.zeros((MLA_HEADS, MLA_VT_ROWS - MLA_V, nkv), F32)], axis=1)
    wvt = wvt.reshape(MLA_HEADS * MLA_VT_ROWS, nkv).astype(BF16)

    zg = jnp.zeros((LANE - MLA_QK,), F32)
    gq = jnp.concatenate([q_g, zg])[None, :]
    gk = jnp.concatenate([k_g, zg])[None, :]

    wo = w_o.reshape(MLA_HEADS, MLA_V, D_MODEL)
    wo = jnp.concatenate([wo, jnp.zeros((MLA_HEADS, LANE - MLA_V, D_MODEL), F32)], axis=1)
    wo = wo.reshape(MLA_HEADS * LANE, D_MODEL).astype(BF16)
    bound = _score_bound(MLA_QK, q_g, k_g)
    q_shift = -bound * (jnp.arange(LANE) == SHIFT_LANE).astype(F32)[None, :]
    weights = (norm_g[None, :], wd, q_lat_g[None, :], kv_lat_g[None, :], wuq, wuk, wvt, gq, gk, q_shift)
    return weights, wo, bound


def _score_bound(dim, q_g, k_g):
    return SCORE_MARGIN * math.sqrt(dim) * LOG2E * jnp.max(jnp.abs(q_g)) * jnp.max(jnp.abs(k_g))


def _rope_tables(seq):
    inv = ROPE_THETA ** (-jnp.arange(0, MLA_ROPE, 2, dtype=F32) / MLA_ROPE)
    ang = jnp.arange(seq, dtype=F32)[:, None] * inv[None, :]
    ang = jnp.concatenate([ang, ang], axis=-1)
    ones = jnp.ones((seq, MLA_NOPE), F32)
    zlo = jnp.zeros((seq, MLA_NOPE), F32)
    zhi = jnp.zeros((seq, LANE - MLA_QK), F32)
    cos = jnp.concatenate([ones, jnp.cos(ang), zhi], axis=1)
    sin = jnp.concatenate([zlo, jnp.sin(ang), zhi], axis=1)
    return cos, sin


def _vt_shape_and_spec(b, heads, rows, s, tm):
    shape = jax.ShapeDtypeStruct((b, heads, s // KV_CHUNK, rows, KV_CHUNK), BF16)
    spec = pl.BlockSpec((None, heads, tm // KV_CHUNK, rows, KV_CHUNK), lambda bi, i: (bi, 0, i, 0, 0))
    return shape, spec


def _store_vt(vt_ref, hd, v_t):
    for c in range(v_t.shape[1] // KV_CHUNK):
        vt_ref[hd, c] = v_t[:, c * KV_CHUNK:(c + 1) * KV_CHUNK].astype(BF16)


def _mla_pre(x, weights, cos, sin, tm):
    b, s, _ = x.shape
    hw = MLA_HEADS * LANE
    out = jax.ShapeDtypeStruct((b, s, hw), BF16)
    vt_shape, vt_spec = _vt_shape_and_spec(b, MLA_HEADS, MLA_VT_ROWS, s, tm)
    tile = lambda bi, i: (bi, i, 0)
    w_specs = [_const_spec(w.shape) for w in weights]
    return pl.pallas_call(
        _mla_pre_kernel,
        grid=(b, s // tm),
        in_specs=[pl.BlockSpec((None, tm, D_MODEL), tile)] + w_specs + [
            pl.BlockSpec((tm, LANE), lambda bi, i: (i, 0)),
            pl.BlockSpec((tm, LANE), lambda bi, i: (i, 0)),
        ],
        out_specs=[pl.BlockSpec((None, tm, hw), tile)] * 2 + [vt_spec],
        out_shape=[out, out, vt_shape],
        compiler_params=_params(("parallel", "parallel")),
        name="mla_pre",
    )(x, *weights, cos, sin)


def _flash_core(scores, vt_ref, s_buf, mx_buf, p_buf, al_buf, m_sc, acc_sc, n_kv):
    assert n_kv >= 2 and n_kv % 2 == 0

    def issue_scores(j, slot):
        s = scores(j)
        s_buf[slot] = s
        mx_buf[slot] = jnp.max(s, axis=0, keepdims=True)

    def softmax(slot):
        m_old = m_sc[...]
        m_new = jnp.maximum(m_old, mx_buf[slot])
        p_buf[slot] = jnp.exp2(s_buf[slot] - m_new).astype(BF16)
        al_buf[slot] = jnp.exp2(m_old - m_new)
        m_sc[...] = m_new

    def values(j, slot):
        acc_sc[...] = al_buf[slot] * acc_sc[...] + _dot(vt_ref[j], p_buf[slot])

    m_sc[...] = jnp.full(m_sc.shape, NEG_BIG, F32)
    acc_sc[...] = jnp.zeros(acc_sc.shape, F32)
    issue_scores(0, 0)
    issue_scores(1, 1)
    softmax(0)

    def pair(i, carry):
        j = 2 * i + 1
        issue_scores(j + 1, 0)
        softmax(1)
        values(j - 1, 0)
        issue_scores(j + 2, 1)
        softmax(0)
        values(j, 1)
        return carry

    lax.fori_loop(0, (n_kv - 2) // 2, pair, 0)
    softmax(1)
    values(n_kv - 2, 0)
    values(n_kv - 1, 1)


def _flash_scratch(rows, nq):
    return [
        pltpu.VMEM((2, KV_CHUNK, nq), F32),
        pltpu.VMEM((2, 1, nq), F32),
        pltpu.VMEM((2, KV_CHUNK, nq), BF16),
        pltpu.VMEM((2, 1, nq), F32),
        pltpu.VMEM((1, nq), F32),
        pltpu.VMEM((rows, nq), F32),
    ]


def _key_chunk(k_ref, j):
    return k_ref[pl.ds(pl.multiple_of(j * KV_CHUNK, KV_CHUNK), KV_CHUNK), :]


def _mla_store(o_t, o_ref):
    o_ref[...] = jnp.concatenate([o_t, jnp.zeros((LANE - MLA_V, o_t.shape[1]), F32)], axis=0).T.astype(BF16)


def _mla_finalize(acc_sc, o_ref):
    acc = acc_sc[...]
    _mla_store(acc[:MLA_V] / acc[MLA_V:MLA_V + 1], o_ref)


def _flash_mla_kernel(q_ref, k_ref, vt_ref, o_ref, *scratch, n_kv):
    acc_sc = scratch[-1]
    _flash_core(lambda j: _dot_nt(_key_chunk(k_ref, j), q_ref[...]), vt_ref, *scratch, n_kv)
    _mla_finalize(acc_sc, o_ref)


def _flash_mla(q, k, vt, tq):
    b, s, hw = q.shape
    n_kv = s // KV_CHUNK
    kern = functools.partial(_flash_mla_kernel, n_kv=n_kv)
    return pl.pallas_call(
        kern,
        grid=(b, MLA_HEADS, s // tq),
        in_specs=[
            pl.BlockSpec((None, tq, LANE), lambda bi, h, i: (bi, i, h)),
            pl.BlockSpec((None, s, LANE), lambda bi, h, i: (bi, 0, h)),
            pl.BlockSpec((None, None, n_kv, MLA_VT_ROWS, KV_CHUNK), lambda bi, h, i: (bi, h, 0, 0, 0)),
        ],
        out_specs=pl.BlockSpec((None, tq, LANE), lambda bi, h, i: (bi, i, h)),
        out_shape=jax.ShapeDtypeStruct((b, s, hw), BF16),
        scratch_shapes=_flash_scratch(MLA_VT_ROWS, tq),
        compiler_params=_params(("parallel", "parallel", "arbitrary")),
        name="flash_mla",
    )(q, k, vt)


def _static_core(first_probs, probs, chunk_of, vt_ref, p_buf, acc_sc, n_kv):
    assert n_kv >= 2 and n_kv % 2 == 0

    def values(t, slot, first=False):
        pv = _dot(vt_ref[chunk_of(t)], p_buf[slot])
        acc_sc[...] = pv if first else acc_sc[...] + pv

    p_buf[0] = first_probs()
    p_buf[1] = probs(1)
    values(0, 0, first=True)

    for t in range(1, n_kv - 1, 2):
        p_buf[0] = probs(t + 1)
        values(t, 1)
        p_buf[1] = probs(t + 2)
        values(t + 1, 0)
    values(n_kv - 1, 1)


def _static_scratch(rows, nq):
    return [pltpu.VMEM((2, KV_CHUNK, nq), BF16), pltpu.VMEM((rows, nq), F32)]


def _query_tile_loop(n_tiles, tile_fn):
    def body(i, carry):
        tile_fn(i)
        return carry
    lax.fori_loop(0, n_tiles, body, 0)


def _flash_mla_static_kernel(q_ref, k_ref, vt_ref, o_ref, p_buf, acc_sc, *, tq, n_kv):
    def tile(i):
        rows = pl.ds(pl.multiple_of(i * tq, tq), tq)
        probs = lambda t: jnp.exp2(_dot_nt(_key_chunk(k_ref, t), q_ref[rows, :])).astype(BF16)
        _static_core(lambda: probs(0), probs, lambda t: t, vt_ref, p_buf, acc_sc, n_kv)
        _mla_finalize(acc_sc, o_ref.at[rows, :])

    _query_tile_loop(q_ref.shape[0] // tq, tile)


def _flash_mla_static(q, k, vt, tq):
    b, s, hw = q.shape
    n_kv = s // KV_CHUNK
    kern = functools.partial(_flash_mla_static_kernel, tq=tq, n_kv=n_kv)
    head_slab = pl.BlockSpec((None, s, LANE), lambda bi, h: (bi, 0, h))
    return pl.pallas_call(
        kern,
        grid=(b, MLA_HEADS),
        in_specs=[
            head_slab,
            head_slab,
            pl.BlockSpec((None, None, n_kv, MLA_VT_ROWS, KV_CHUNK), lambda bi, h: (bi, h, 0, 0, 0)),
        ],
        out_specs=head_slab,
        out_shape=jax.ShapeDtypeStruct((b, s, hw), BF16),
        scratch_shapes=_static_scratch(MLA_VT_ROWS, tq),
        compiler_params=_params(("parallel", "parallel")),
        name="flash_mla_static",
    )(q, k, vt)


def _pos_lanes(pos0, rows, first_lane):
    pos = pos0 + lax.broadcasted_iota(jnp.int32, (rows, LANE), 0)
    lane = lax.broadcasted_iota(jnp.int32, (rows, LANE), 1) - first_lane
    hi = lax.shift_right_logical(pos, int(math.log2(POS_SPLIT))).astype(F32)
    lo = (pos & (POS_SPLIT - 1)).astype(F32)
    p = DIFF_AUG_PARTS
    in_hi = (lane >= 0) & (lane < p)
    in_lo = (lane >= p) & (lane < 2 * p)
    return jnp.where(in_hi, hi, jnp.where(in_lo, lo, 0.0))


def _diff_pre_kernel(x_ref, ng_ref, w_ref, wvt_ref, gq_ref, gk_ref, kaug_ref, q_ref, k_ref, vt_ref):
    h = _rms(x_ref[...], ng_ref[...]).astype(BF16)
    qkv = _dot(h, w_ref[...])
    v_t = _dot_nt(wvt_ref[...], h)
    hw = DIFF_HEADS * LANE
    lo = lax.broadcasted_iota(jnp.int32, (1, LANE), 1) < DIFF_HD
    q_scale = (DIFF_HD ** -0.5) * LOG2E
    tm = qkv.shape[0]
    k_pos = _pos_lanes(pl.program_id(1) * tm, tm, 0)
    ones_row = (lax.broadcasted_iota(jnp.int32, (DIFF_VT_ROWS, tm), 0) == LANE).astype(F32)

    def half_norm(t, g):
        t2 = t * t
        ss_lo = jnp.sum(jnp.where(lo, t2, 0.0), axis=-1, keepdims=True)
        ss_hi = jnp.sum(jnp.where(lo, 0.0, t2), axis=-1, keepdims=True)
        r = lax.rsqrt(jnp.where(lo, ss_lo, ss_hi) * (1.0 / DIFF_HD) + EPS)
        return t * r * g

    for hd in range(DIFF_HEADS):
        sl = slice(hd * LANE, (hd + 1) * LANE)
        q_ref[:, sl] = (half_norm(qkv[:, sl], gq_ref[...]) * q_scale).astype(BF16)
        ks = slice(2 * hd * LANE, (2 * hd + 1) * LANE)
        k_ref[:, ks] = half_norm(qkv[:, hw + hd * LANE:hw + (hd + 1) * LANE], gk_ref[...]).astype(BF16)
        k_ref[:, (2 * hd + 1) * LANE:(2 * hd + 2) * LANE] = (k_pos + kaug_ref[hd]).astype(BF16)
        _store_vt(vt_ref, hd, v_t[hd * DIFF_VT_ROWS:(hd + 1) * DIFF_VT_ROWS] + ones_row)


def _slope_pieces():
    slopes = 2.0 ** (-8.0 * jnp.arange(1, DIFF_HEADS + 1, dtype=F32) / DIFF_HEADS) * LOG2E
    pieces, rest = [], slopes
    for _ in range(DIFF_AUG_PARTS):
        piece = rest.astype(BF16).astype(F32)
        pieces.append(piece)
        rest = rest - piece
    return jnp.stack(pieces, axis=1)


def _diff_bias_tables(bound):
    p = DIFF_AUG_PARTS
    pieces = _slope_pieces()
    zeros = jnp.zeros((DIFF_HEADS, LANE - 4 * p - 1), F32)
    one = jnp.ones((DIFF_HEADS, 1), F32)
    zp = jnp.zeros((DIFF_HEADS, p), F32)
    k_tab = jnp.concatenate([zp, zp, POS_SPLIT * pieces, pieces, one, zeros], axis=1)[:, None, :]
    shift = -bound * one
    after = jnp.concatenate([-POS_SPLIT * pieces, -pieces, zp, zp, shift, zeros], axis=1)
    before = jnp.concatenate([POS_SPLIT * pieces, pieces, zp, zp, shift, zeros], axis=1)
    diag = jnp.concatenate([zp, zp, zp, zp, shift, zeros], axis=1)
    return k_tab, jnp.stack([after, before, diag], axis=1)


def _diff_pre(x, norm_g, w_qkv, q_g, k_g, k_tab, tm):
    b, s, _ = x.shape
    hw = DIFF_HEADS * LANE
    out = jax.ShapeDtypeStruct((b, s, hw), BF16)
    out_k = jax.ShapeDtypeStruct((b, s, 2 * hw), BF16)
    vt_shape, vt_spec = _vt_shape_and_spec(b, DIFF_HEADS, DIFF_VT_ROWS, s, tm)
    tile = lambda bi, i: (bi, i, 0)
    wvt = w_qkv[:, 2 * hw:].T.reshape(DIFF_HEADS, LANE, D_MODEL)
    wvt = jnp.concatenate([wvt, jnp.zeros((DIFF_HEADS, DIFF_VT_ROWS - LANE, D_MODEL), F32)], axis=1)
    wvt = wvt.reshape(DIFF_HEADS * DIFF_VT_ROWS, D_MODEL).astype(BF16)
    weights = (norm_g[None, :], w_qkv[:, :2 * hw].astype(BF16), wvt, q_g.reshape(1, LANE),
               k_g.reshape(1, LANE), k_tab)
    return pl.pallas_call(
        _diff_pre_kernel,
        grid=(b, s // tm),
        in_specs=[pl.BlockSpec((None, tm, D_MODEL), tile)] + [_const_spec(w.shape) for w in weights],
        out_specs=[pl.BlockSpec((None, tm, hw), tile), pl.BlockSpec((None, tm, 2 * hw), tile), vt_spec],
        out_shape=[out, out_k, vt_shape],
        compiler_params=_params(("parallel", "parallel")),
        name="diff_pre",
    )(x, *weights)


def _flash_diff_kernel(slope_ref, q_ref, k_ref, vt_ref, dmat_ref, lam_ref, subg_ref, o_ref,
                       qcat_sc, *scratch, tq, n_kv, lambda_init):
    acc_sc = scratch[-1]
    lo = lax.broadcasted_iota(jnp.int32, (1, LANE), 1) < DIFF_HD
    q = q_ref[...]
    zero = jnp.zeros_like(q)
    qcat_sc[:tq] = jnp.where(lo, q, zero)
    qcat_sc[tq:] = jnp.where(lo, zero, q)
    neg_slope = slope_ref[pl.program_id(1)]
    q0 = pl.program_id(2) * tq

    def scores(j):
        delta = (j * KV_CHUNK - q0).astype(F32)
        bias = jnp.abs(dmat_ref[...] + delta) * neg_slope
        return _dot_nt(_key_chunk(k_ref, j), qcat_sc[...]) + jnp.concatenate([bias, bias], axis=1)

    _flash_core(scores, vt_ref, *scratch, n_kv)
    _diff_finalize(acc_sc, lam_ref, subg_ref, o_ref, tq, lambda_init)


def _diff_finalize(acc_sc, lam_ref, subg_ref, o_ref, tq, lambda_init):
    acc = acc_sc[...]
    _diff_store(acc[:LANE] / acc[LANE:LANE + 1], lam_ref, subg_ref, o_ref, tq, lambda_init)


def _diff_store(o, lam_ref, subg_ref, o_ref, tq, lambda_init):
    lp = lam_ref[...]
    lam = (jnp.exp(jnp.sum(lp[0:1] * lp[1:2], axis=-1, keepdims=True))
           - jnp.exp(jnp.sum(lp[2:3] * lp[3:4], axis=-1, keepdims=True)) + lambda_init)
    o = (o[:, :tq] - lam * o[:, tq:]).T
    o = _rms(o, subg_ref[...]) * (1.0 - lambda_init)
    o_ref[...] = o.astype(BF16)


def _flash_diff_static_kernel(slope_ref, q_ref, k_ref, vt_ref, dmat_ref, lam_ref, subg_ref, qtab_ref,
                              o_ref, qz_sc, aug_sc, p_buf, acc_sc, *, tq, n_kv, lambda_init):
    tile = functools.partial(_flash_diff_static_tile, slope_ref, q_ref, k_ref, vt_ref, dmat_ref, lam_ref,
                             subg_ref, qtab_ref, o_ref, qz_sc, aug_sc, p_buf, acc_sc, tq, n_kv,
                             lambda_init)
    _query_tile_loop(q_ref.shape[0] // tq, tile)


def _flash_diff_static_tile(slope_ref, q_ref, k_ref, vt_ref, dmat_ref, lam_ref, subg_ref, qtab_ref,
                            o_ref, qz_sc, aug_sc, p_buf, acc_sc, tq, n_kv, lambda_init, i):
    lo = lax.broadcasted_iota(jnp.int32, (1, LANE), 1) < DIFF_HD
    q0 = pl.multiple_of(i * tq, tq)
    q = q_ref[pl.ds(q0, tq), :]
    zero = jnp.zeros_like(q)
    qz_sc[:tq] = jnp.where(lo, q, zero)
    qz_sc[tq:] = jnp.where(lo, zero, q)
    n_d = tq // KV_CHUNK
    q_pos = _pos_lanes(q0, tq, 2 * DIFF_AUG_PARTS)
    tab = qtab_ref[...]
    for kind, sign in enumerate((1.0, -1.0, 0.0)):
        aug_sc[kind] = (tab[kind:kind + 1] + sign * q_pos).astype(BF16)
    jd = lax.shift_right_logical(q0, int(math.log2(KV_CHUNK)))
    neg_slope = slope_ref[pl.program_id(1)]

    def with_bias_columns(aug):
        return jnp.concatenate([qz_sc[...], jnp.concatenate([aug, aug], axis=0)], axis=1)

    def diag_probs(d):
        kinds = [2 if e == d else (1 if e > d else 0) for e in range(n_d)]
        aug = jnp.concatenate([aug_sc[kind, e * KV_CHUNK:(e + 1) * KV_CHUNK, :]
                               for e, kind in enumerate(kinds)], axis=0)
        s = _dot_nt(_key_chunk(k_ref, jd + d), with_bias_columns(aug))
        bias = jnp.abs(dmat_ref[...]) * neg_slope
        cols = []
        for c in range(2):
            for e in range(n_d):
                blk = s[:, c * tq + e * KV_CHUNK:c * tq + (e + 1) * KV_CHUNK]
                cols.append(blk + bias if e == d else blk)
        return jnp.exp2(jnp.concatenate(cols, axis=1)).astype(BF16)

    def chunk_of(t):
        r = t - n_d
        return jnp.where(t < n_d, jd + t, r + n_d * (r >= jd).astype(jnp.int32))

    def probs(t):
        if isinstance(t, int) and t < n_d:
            return diag_probs(t)
        j = chunk_of(t)
        before = (j < jd).astype(jnp.int32)
        return jnp.exp2(_dot_nt(_key_chunk(k_ref, j), with_bias_columns(aug_sc[before]))).astype(BF16)

    _static_core(lambda: diag_probs(0), probs, chunk_of, vt_ref, p_buf, acc_sc, n_kv)
    _diff_finalize(acc_sc, lam_ref, subg_ref, o_ref.at[pl.ds(q0, tq), :], tq, lambda_init)


def _flash_diff(q, k, vt, lam_p, sub_g, q_tab, lambda_init, tq, static_shift):
    b, s, hw = q.shape
    n_kv = s // KV_CHUNK
    slopes = 2.0 ** (-8.0 * jnp.arange(1, DIFF_HEADS + 1, dtype=F32) / DIFF_HEADS)
    neg_slopes = -slopes * LOG2E
    d_cols = KV_CHUNK if static_shift else tq
    dmat = (jnp.arange(KV_CHUNK, dtype=F32)[:, None] - jnp.arange(d_cols, dtype=F32)[None, :])
    const = dict(pipeline_mode=pl.Buffered(1))
    origin = lambda *_: (0, 0)
    vt_spec = pl.BlockSpec((None, None, n_kv, DIFF_VT_ROWS, KV_CHUNK), lambda bi, h, *_: (bi, h, 0, 0, 0))
    small_specs = [
        pl.BlockSpec((KV_CHUNK, d_cols), origin, **const),
        pl.BlockSpec((4, DIFF_HD), origin, **const),
        pl.BlockSpec((1, LANE), origin, **const),
    ]
    args = [neg_slopes, q, k, vt, dmat, lam_p, sub_g[None, :]]
    if static_shift:
        assert tq % KV_CHUNK == 0
        kern = functools.partial(_flash_diff_static_kernel, tq=tq, n_kv=n_kv, lambda_init=lambda_init)
        grid = (b, DIFF_HEADS)
        q_spec = pl.BlockSpec((None, s, LANE), lambda bi, h, sl: (bi, 0, h))
        k_spec = pl.BlockSpec((None, s, 2 * LANE), lambda bi, h, sl: (bi, 0, h))
        in_specs = [q_spec, k_spec, vt_spec] + small_specs + [
            pl.BlockSpec((None, 3, LANE), lambda bi, h, sl: (h, 0, 0))]
        args.append(q_tab)
        scratch = ([pltpu.VMEM((2 * tq, LANE), BF16), pltpu.VMEM((3, tq, LANE), BF16)]
                   + _static_scratch(DIFF_VT_ROWS, 2 * tq))
        name = "flash_diff_static"
    else:
        kern = functools.partial(_flash_diff_kernel, tq=tq, n_kv=n_kv, lambda_init=lambda_init)
        grid = (b, DIFF_HEADS, s // tq)
        q_spec = pl.BlockSpec((None, tq, LANE), lambda bi, h, i, sl: (bi, i, h))
        k_spec = pl.BlockSpec((None, s, LANE), lambda bi, h, i, sl: (bi, 0, 2 * h))
        in_specs = [q_spec, k_spec, vt_spec] + small_specs
        scratch = [pltpu.VMEM((2 * tq, LANE), BF16)] + _flash_scratch(DIFF_VT_ROWS, 2 * tq)
        name = "flash_diff"
    grid_spec = pltpu.PrefetchScalarGridSpec(
        num_scalar_prefetch=1, grid=grid, in_specs=in_specs, out_specs=q_spec, scratch_shapes=scratch)
    return pl.pallas_call(
        kern,
        grid_spec=grid_spec,
        out_shape=jax.ShapeDtypeStruct((b, s, hw), BF16),
        compiler_params=_params(("parallel", "parallel") + (("arbitrary",) if len(grid) == 3 else ())),
        name=name,
    )(*args)


def _post_xattn_kernel(x_ref, om_ref, wom_ref, ng_ref, wq_ref, qg_ref, k_ref, v_ref, wo_ref,
                       y_ref, o_sc):
    x1 = x_ref[...] + _dot(om_ref[...], wom_ref[...])
    h = _rms(x1, ng_ref[...]).astype(BF16)
    q = _dot(h, wq_ref[...])
    qg = qg_ref[...] * ((XA_HD ** -0.5) * LOG2E)
    for hd in range(XA_HEADS):
        sl = slice(hd * XA_HD, (hd + 1) * XA_HD)
        qh = _rms(q[:, sl], qg).astype(BF16)
        s = _dot_nt(qh, k_ref[:, sl])
        p = jnp.exp2(s - jnp.max(s, axis=-1, keepdims=True))
        l = jnp.sum(p, axis=-1, keepdims=True)
        o_sc[:, sl] = (_dot(p.astype(BF16), v_ref[:, sl]) / l).astype(BF16)
    y_ref[...] = x1 + _dot(o_sc[...], wo_ref[...])


def _post_xattn(x, o_mix, w_o_mix, norm_g, w_q, q_g, k_mem, v_mem, layer, w_o, tm):
    b, s, _ = x.shape
    tile = lambda bi, i: (bi, i, 0)
    mem_spec = pl.BlockSpec((None, None, N_MEM, D_MODEL), lambda bi, i: (layer, bi, 0, 0))
    sq = (D_MODEL, D_MODEL)
    return pl.pallas_call(
        _post_xattn_kernel,
        grid=(b, s // tm),
        in_specs=[
            pl.BlockSpec((None, tm, D_MODEL), tile),
            pl.BlockSpec((None, tm, o_mix.shape[-1]), tile),
            _const_spec(w_o_mix.shape),
            _const_spec((1, D_MODEL)),
            _const_spec(sq),
            _const_spec((1, XA_HD)),
            mem_spec,
            mem_spec,
            _const_spec(sq),
        ],
        out_specs=pl.BlockSpec((None, tm, D_MODEL), tile),
        out_shape=jax.ShapeDtypeStruct(x.shape, F32),
        scratch_shapes=[pltpu.VMEM((tm, D_MODEL), BF16)],
        compiler_params=_params(("parallel", "parallel")),
        name="post_xattn",
    )(x, o_mix, w_o_mix, norm_g[None, :], w_q.astype(BF16), q_g[None, :], k_mem, v_mem,
      w_o.astype(BF16))


def _ffn_kernel(x_ref, prev_ref, next_ref, ng_ref, wg_ref, wu_ref, cw_ref, cb_ref, wd_ref, y_ref,
                *, tm, n_chunks):
    i = pl.program_id(1)
    keep_prev = (i > 0).astype(F32)
    keep_next = (i < pl.num_programs(1) - 1).astype(F32)
    x = x_ref[...]
    xe = jnp.concatenate([prev_ref[...] * keep_prev, x, next_ref[...] * keep_next], axis=0)
    he = _rms(xe, ng_ref[...]).astype(BF16)
    hc = he[HALO:HALO + tm]
    cw = cw_ref[...]
    cb = cb_ref[...]
    y = x
    n_tiles = D_FF // MXU_DIM
    bounds = [MXU_DIM * ((n_tiles * c + n_chunks - 1) // n_chunks) for c in range(n_chunks)] + [D_FF]
    for c in range(n_chunks):
        cs = slice(bounds[c], bounds[c + 1])
        ge = _dot(he, wg_ref[:, cs])
        u = _dot(hc, wu_ref[:, cs])
        g = (ge[HALO - 1:HALO - 1 + tm] * cw[0:1, cs] + ge[HALO:HALO + tm] * cw[1:2, cs]
             + ge[HALO + 1:HALO + 1 + tm] * cw[2:3, cs] + cb[:, cs])
        act = (g * jax.nn.sigmoid(g) * u).astype(BF16)
        y = y + _dot(act, wd_ref[cs, :])
    y_ref[...] = y


def _ffn(x, norm_g, w_gu, conv_w, conv_b, w_down, tm, n_chunks=1):
    b, s, _ = x.shape
    nh = tm // HALO
    last = s // HALO - 1
    w_g = w_gu[:, :D_FF].astype(BF16)
    w_u = w_gu[:, D_FF:].astype(BF16)
    weights = (norm_g[None, :], w_g, w_u, conv_w, conv_b[None, :], w_down.astype(BF16))
    kern = functools.partial(_ffn_kernel, tm=tm, n_chunks=n_chunks)
    return pl.pallas_call(
        kern,
        grid=(b, s // tm),
        in_specs=[
            pl.BlockSpec((None, tm, D_MODEL), lambda bi, i: (bi, i, 0)),
            pl.BlockSpec((None, HALO, D_MODEL), lambda bi, i: (bi, jnp.maximum(i * nh - 1, 0), 0)),
            pl.BlockSpec((None, HALO, D_MODEL), lambda bi, i: (bi, jnp.minimum((i + 1) * nh, last), 0)),
        ] + [_const_spec(w.shape) for w in weights],
        out_specs=pl.BlockSpec((None, tm, D_MODEL), lambda bi, i: (bi, i, 0)),
        out_shape=jax.ShapeDtypeStruct(x.shape, F32),
        compiler_params=_params(("parallel", "parallel")),
        name="ffn",
    )(x, x, x, *weights)


def _tiles(s):
    return dict(tm=min(512, s), tm_pre=min(1024, s), tm_post=min(1024, s),
                tq_mla=min(4096, s), tq_diff=min(2048, s),
                tq_mla_online=min(1024, s), tq_diff_online=min(512, s))


def _mixer_weights(mla_p, diff_p):
    layers = []
    for i in range(DEPTH):
        j = i // 2
        if i % 2 == 0:
            weights, w_o_mix, bound = _mla_weights(*[p[j] for p in mla_p])
            layers.append(dict(weights=weights, w_o_mix=w_o_mix, bound=bound))
        else:
            norm_g, w_qkv, q_g, k_g, lam_p, sub_g, w_o = [p[j] for p in diff_p]
            bound = _score_bound(DIFF_HD, q_g, k_g)
            k_tab, q_tab = _diff_bias_tables(bound)
            layers.append(dict(pre=(norm_g, w_qkv, q_g, k_g, k_tab), lam_p=lam_p, sub_g=sub_g,
                               q_tab=q_tab, w_o_mix=w_o.astype(BF16), bound=bound))
    return layers


def _trunk(x, mem, mixers, xa_p, ffn_p):
    s = x.shape[1]
    assert s % KV_CHUNK == 0
    t = _tiles(s)
    xa_norm, xa_mem_norm, xa_w_q, xa_w_kv, xa_q_norm, xa_k_norm, xa_w_o = xa_p
    k_mem, v_mem = _mem_kv(mem, xa_mem_norm, xa_w_kv, xa_k_norm)
    cos, sin = _rope_tables(s)
    for i in range(DEPTH):
        mx = mixers[i]
        static_ok = mx["bound"] <= MAX_STATIC_BOUND
        if i % 2 == 0:
            q, k, vt = _mla_pre(x, mx["weights"], cos, sin, t["tm_pre"])
            o_mix = lax.cond(static_ok,
                             lambda q, k, vt: _flash_mla_static(q, k, vt, t["tq_mla"]),
                             lambda q, k, vt: _flash_mla(q, k, vt, t["tq_mla_online"]), q, k, vt)
        else:
            lambda_init = 0.8 - 0.6 * math.exp(-0.3 * i)
            q, k, vt = _diff_pre(x, *mx["pre"], t["tm_pre"])
            flash = lambda static: functools.partial(
                _flash_diff, lambda_init=lambda_init, static_shift=static,
                tq=t["tq_diff"] if static else t["tq_diff_online"])
            o_mix = lax.cond(static_ok, flash(True), flash(False),
                             q, k, vt, mx["lam_p"], mx["sub_g"], mx["q_tab"])
        x = _post_xattn(x, o_mix, mx["w_o_mix"], xa_norm[i], xa_w_q[i], xa_q_norm[i], k_mem, v_mem, i,
                        xa_w_o[i], t["tm_post"])
        x = _ffn(x, *[p[i] for p in ffn_p], t["tm"])
    return x


def kernel(x_prompt, x_sample, mem_prompt, mem_sample, mla_norm, mla_w_down, mla_q_lat_norm, mla_kv_lat_norm, mla_w_uq, mla_w_ukv, mla_q_norm, mla_k_norm, mla_w_o, diff_norm, diff_w_qkv, diff_q_norm, diff_k_norm, diff_lambda, diff_sub_norm, diff_w_o, xa_norm, xa_mem_norm, xa_w_q, xa_w_kv, xa_q_norm, xa_k_norm, xa_w_o, ffn_norm, ffn_w_gu, ffn_conv_w, ffn_conv_b, ffn_w_down):
    mla_p = (mla_norm, mla_w_down, mla_q_lat_norm, mla_kv_lat_norm, mla_w_uq, mla_w_ukv,
             mla_q_norm, mla_k_norm, mla_w_o)
    diff_p = (diff_norm, diff_w_qkv, diff_q_norm, diff_k_norm, diff_lambda, diff_sub_norm, diff_w_o)
    xa_p = (xa_norm, xa_mem_norm, xa_w_q, xa_w_kv, xa_q_norm, xa_k_norm, xa_w_o)
    ffn_p = (ffn_norm, ffn_w_gu, ffn_conv_w, ffn_conv_b, ffn_w_down)
    mixers = _mixer_weights(mla_p, diff_p)
    y_prompt = _trunk(x_prompt, mem_prompt, mixers, xa_p, ffn_p)
    y_sample = _trunk(x_sample, mem_sample, mixers, xa_p, ffn_p)
    return (y_prompt, y_sample)
```

```python
import functools
import math

import jax
import jax.numpy as jnp
from jax import lax
from jax.experimental import pallas as pl
from jax.experimental.pallas import tpu as pltpu

D_MODEL = 1024
DEPTH = 4
N_MEM = 256
EPS = 1e-6
MLA_HEADS = 8
MLA_Q_LORA = 384
MLA_KV_LORA = 256
MLA_NOPE = 64
MLA_ROPE = 32
MLA_QK = MLA_NOPE + MLA_ROPE
MLA_V = 64
ROPE_THETA = 10000.0
DIFF_HEADS = 8
DIFF_HD = 64
XA_HEADS = 4
XA_HD = D_MODEL // XA_HEADS
D_FF = 2816
CONV_W = 3

LANE = 128
HALO = 8
BF16_ROWS = 16
MXU_DIM = 256
KV_CHUNK = 512
DIFF_VT_ROWS = LANE + BF16_ROWS
MLA_VT_ROWS = MLA_V + BF16_ROWS
SHIFT_LANE = MLA_QK
SCORE_MARGIN = 1.02
MAX_STATIC_BOUND = 40.0
DIFF_AUG_PARTS = 3
POS_SPLIT = 64
LOG2E = math.log2(math.e)
NEG_BIG = -1e30
VMEM_LIMIT = 56 * 1024 * 1024

F32 = jnp.float32
BF16 = jnp.bfloat16


def _const_spec(shape):
    nd = len(shape)
    return pl.BlockSpec(shape, lambda *_: (0,) * nd, pipeline_mode=pl.Buffered(1))


def _params(sem):
    return pltpu.CompilerParams(dimension_semantics=sem, vmem_limit_bytes=VMEM_LIMIT)


def _rms(x, g):
    return x * lax.rsqrt(jnp.mean(x * x, axis=-1, keepdims=True) + EPS) * g


def _dot(a, b):
    return jnp.dot(a, b, preferred_element_type=F32)


def _dot_nt(a, b):
    return lax.dot_general(a, b, (((1,), (1,)), ((), ())), preferred_element_type=F32)


def _mem_kv_kernel(mem_ref, g_ref, w_ref, kg_ref, k_ref, v_ref):
    h = _rms(mem_ref[...], g_ref[...]).astype(BF16)
    kv = _dot(h, w_ref[...])
    kg = kg_ref[...]
    for hd in range(XA_HEADS):
        sl = slice(hd * XA_HD, (hd + 1) * XA_HD)
        k_ref[:, sl] = _rms(kv[:, sl], kg).astype(BF16)
    v_ref[...] = kv[:, D_MODEL:].astype(BF16)


def _mem_kv(mem, mem_g, w_kv, k_g):
    bm = mem.shape[0]
    out = jax.ShapeDtypeStruct((DEPTH, bm, N_MEM, D_MODEL), BF16)
    return pl.pallas_call(
        _mem_kv_kernel,
        grid=(DEPTH, bm),
        in_specs=[
            pl.BlockSpec((None, N_MEM, D_MODEL), lambda l, b: (b, 0, 0)),
            pl.BlockSpec((None, 1, D_MODEL), lambda l, b: (l, 0, 0)),
            pl.BlockSpec((None, D_MODEL, 2 * D_MODEL), lambda l, b: (l, 0, 0)),
            pl.BlockSpec((None, 1, XA_HD), lambda l, b: (l, 0, 0)),
        ],
        out_specs=[
            pl.BlockSpec((None, None, N_MEM, D_MODEL), lambda l, b: (l, b, 0, 0)),
            pl.BlockSpec((None, None, N_MEM, D_MODEL), lambda l, b: (l, b, 0, 0)),
        ],
        out_shape=[out, out],
        compiler_params=_params(("arbitrary", "arbitrary")),
        name="mem_kv",
    )(mem, mem_g[:, None, :], w_kv.astype(BF16), k_g[:, None, :])


def _mla_pre_kernel(x_ref, ng_ref, wd_ref, qlg_ref, kvlg_ref, wuq_ref, wuk_ref, wvt_ref,
                    gq_ref, gk_ref, qshift_ref, cos_ref, sin_ref, q_ref, k_ref, vt_ref):
    h = _rms(x_ref[...], ng_ref[...]).astype(BF16)
    down = _dot(h, wd_ref[...])
    c_q = _rms(down[:, :MLA_Q_LORA], qlg_ref[...]).astype(BF16)
    kv0 = MLA_Q_LORA
    c_kv = _rms(down[:, kv0:kv0 + MLA_KV_LORA], kvlg_ref[...]).astype(BF16)
    kr = down[:, kv0 + MLA_KV_LORA:kv0 + MLA_KV_LORA + LANE]
    kr_rot = down[:, kv0 + MLA_KV_LORA + LANE:]
    qq = _dot(c_q, wuq_ref[...])
    kn = _dot(c_kv, wuk_ref[...])
    v_t = _dot_nt(wvt_ref[...], c_kv)
    ones_row = (lax.broadcasted_iota(jnp.int32, (MLA_VT_ROWS, v_t.shape[1]), 0) == MLA_V).astype(F32)
    cos = cos_ref[...]
    sin = sin_ref[...]
    gq = gq_ref[...]
    gk = gk_ref[...]
    hw = MLA_HEADS * LANE
    lane = lax.broadcasted_iota(jnp.int32, (1, LANE), 1)
    k_one = (lane == SHIFT_LANE).astype(F32)
    q_shift = qshift_ref[...]
    q_scale = (MLA_QK ** -0.5) * LOG2E
    for hd in range(MLA_HEADS):
        sl = slice(hd * LANE, (hd + 1) * LANE)
        qh = qq[:, sl]
        rq = lax.rsqrt(jnp.sum(qh * qh, axis=-1, keepdims=True) * (1.0 / MLA_QK) + EPS)
        q_rot = qq[:, hw + hd * LANE:hw + (hd + 1) * LANE]
        q_ref[:, sl] = ((rq * q_scale) * (qh * gq * cos + q_rot * sin) + q_shift).astype(BF16)
        kh = kn[:, sl] + kr
        rk = lax.rsqrt(jnp.sum(kh * kh, axis=-1, keepdims=True) * (1.0 / MLA_QK) + EPS)
        k_ref[:, sl] = (rk * (kh * gk * cos + kr_rot * sin) + k_one).astype(BF16)
        _store_vt(vt_ref, hd, v_t[hd * MLA_VT_ROWS:(hd + 1) * MLA_VT_ROWS] + ones_row)


def _rot_half_cols(w):
    half = MLA_ROPE // 2
    return jnp.concatenate([-w[..., half:], w[..., :half]], axis=-1)


def _mla_weights(norm_g, w_down, q_lat_g, kv_lat_g, w_uq, w_ukv, q_g, k_g, w_o):
    nq, nkv = MLA_Q_LORA, MLA_KV_LORA
    w_dq, w_dkv, w_kr = w_down[:, :nq], w_down[:, nq:nq + nkv], w_down[:, nq + nkv:]
    pad_lo = jnp.zeros((D_MODEL, MLA_NOPE), F32)
    pad_hi = jnp.zeros((D_MODEL, LANE - MLA_QK), F32)
    kr_blk = jnp.concatenate([pad_lo, w_kr, pad_hi], axis=1)
    krot_blk = jnp.concatenate([pad_lo, _rot_half_cols(w_kr * k_g[MLA_NOPE:]), pad_hi], axis=1)
    wd = jnp.concatenate([w_dq, w_dkv, kr_blk, krot_blk], axis=1).astype(BF16)

    wq = w_uq.reshape(nq, MLA_HEADS, MLA_QK)
    zq = jnp.zeros((nq, MLA_HEADS, LANE - MLA_QK), F32)
    wq_main = jnp.concatenate([wq, zq], axis=-1).reshape(nq, MLA_HEADS * LANE)
    wq_rot = jnp.concatenate(
        [jnp.zeros((nq, MLA_HEADS, MLA_NOPE), F32),
         _rot_half_cols(wq[..., MLA_NOPE:] * q_g[MLA_NOPE:]), zq], axis=-1
    ).reshape(nq, MLA_HEADS * LANE)
    wuq = jnp.concatenate([wq_main, wq_rot], axis=1).astype(BF16)

    wkv = w_ukv.reshape(nkv, MLA_HEADS, MLA_NOPE + MLA_V)
    zk = jnp.zeros((nkv, MLA_HEADS, LANE - MLA_NOPE), F32)
    wuk = jnp.concatenate([wkv[..., :MLA_NOPE], zk], axis=-1).reshape(nkv, MLA_HEADS * LANE).astype(BF16)
    wvt = jnp.transpose(wkv[..., MLA_NOPE:], (1, 2, 0))
    wvt = jnp.concatenate([wvt, jnp.zeros((MLA_HEADS, MLA_VT_ROWS - MLA_V, nkv), F32)], axis=1)
    wvt = wvt.reshape(MLA_HEADS * MLA_VT_ROWS, nkv).astype(BF16)

    zg = jnp.zeros((LANE - MLA_QK,), F32)
    gq = jnp.concatenate([q_g, zg])[None, :]
    gk = jnp.concatenate([k_g, zg])[None, :]

    wo = w_o.reshape(MLA_HEADS, MLA_V, D_MODEL)
    wo = jnp.concatenate([wo, jnp.zeros((MLA_HEADS, LANE - MLA_V, D_MODEL), F32)], axis=1)
    wo = wo.reshape(MLA_HEADS * LANE, D_MODEL).astype(BF16)
    bound = _score_bound(MLA_QK, q_g, k_g)
    q_shift = -bound * (jnp.arange(LANE) == SHIFT_LANE).astype(F32)[None, :]
    weights = (norm_g[None, :], wd, q_lat_g[None, :], kv_lat_g[None, :], wuq, wuk, wvt, gq, gk, q_shift)
    return weights, wo, bound


def _score_bound(dim, q_g, k_g):
    return SCORE_MARGIN * math.sqrt(dim) * LOG2E * jnp.max(jnp.abs(q_g)) * jnp.max(jnp.abs(k_g))


def _rope_tables(seq):
    inv = ROPE_THETA ** (-jnp.arange(0, MLA_ROPE, 2, dtype=F32) / MLA_ROPE)
    ang = jnp.arange(seq, dtype=F32)[:, None] * inv[None, :]
    ang = jnp.concatenate([ang, ang], axis=-1)
    ones = jnp.ones((seq, MLA_NOPE), F32)
    zlo = jnp.zeros((seq, MLA_NOPE), F32)
    zhi = jnp.zeros((seq, LANE - MLA_QK), F32)
    cos = jnp.concatenate([ones, jnp.cos(ang), zhi], axis=1)
    sin = jnp.concatenate([zlo, jnp.sin(ang), zhi], axis=1)
    return cos, sin


def _vt_shape_and_spec(b, heads, rows, s, tm):
    shape = jax.ShapeDtypeStruct((b, heads, s // KV_CHUNK, rows, KV_CHUNK), BF16)
    spec = pl.BlockSpec((None, heads, tm // KV_CHUNK, rows, KV_CHUNK), lambda bi, i: (bi, 0, i, 0, 0))
    return shape, spec


def _store_vt(vt_ref, hd, v_t):
    for c in range(v_t.shape[1] // KV_CHUNK):
        vt_ref[hd, c] = v_t[:, c * KV_CHUNK:(c + 1) * KV_CHUNK].astype(BF16)


def _mla_pre(x, weights, cos, sin, tm):
    b, s, _ = x.shape
    hw = MLA_HEADS * LANE
    out = jax.ShapeDtypeStruct((b, s, hw), BF16)
    vt_shape, vt_spec = _vt_shape_and_spec(b, MLA_HEADS, MLA_VT_ROWS, s, tm)
    tile = lambda bi, i: (bi, i, 0)
    w_specs = [_const_spec(w.shape) for w in weights]
    return pl.pallas_call(
        _mla_pre_kernel,
        grid=(b, s // tm),
        in_specs=[pl.BlockSpec((None, tm, D_MODEL), tile)] + w_specs + [
            pl.BlockSpec((tm, LANE), lambda bi, i: (i, 0)),
            pl.BlockSpec((tm, LANE), lambda bi, i: (i, 0)),
        ],
        out_specs=[pl.BlockSpec((None, tm, hw), tile)] * 2 + [vt_spec],
        out_shape=[out, out, vt_shape],
        compiler_params=_params(("parallel", "parallel")),
        name="mla_pre",
    )(x, *weights, cos, sin)


def _flash_core(scores, vt_ref, s_buf, mx_buf, p_buf, al_buf, m_sc, acc_sc, n_kv):
    assert n_kv >= 2 and n_kv % 2 == 0

    def issue_scores(j, slot):
        s = scores(j)
        s_buf[slot] = s
        mx_buf[slot] = jnp.max(s, axis=0, keepdims=True)

    def softmax(slot):
        m_old = m_sc[...]
        m_new = jnp.maximum(m_old, mx_buf[slot])
        p_buf[slot] = jnp.exp2(s_buf[slot] - m_new).astype(BF16)
        al_buf[slot] = jnp.exp2(m_old - m_new)
        m_sc[...] = m_new

    def values(j, slot):
        acc_sc[...] = al_buf[slot] * acc_sc[...] + _dot(vt_ref[j], p_buf[slot])

    m_sc[...] = jnp.full(m_sc.shape, NEG_BIG, F32)
    acc_sc[...] = jnp.zeros(acc_sc.shape, F32)
    issue_scores(0, 0)
    issue_scores(1, 1)
    softmax(0)

    def pair(i, carry):
        j = 2 * i + 1
        issue_scores(j + 1, 0)
        softmax(1)
        values(j - 1, 0)
        issue_scores(j + 2, 1)
        softmax(0)
        values(j, 1)
        return carry

    lax.fori_loop(0, (n_kv - 2) // 2, pair, 0)
    softmax(1)
    values(n_kv - 2, 0)
    values(n_kv - 1, 1)


def _flash_scratch(rows, nq):
    return [
        pltpu.VMEM((2, KV_CHUNK, nq), F32),
        pltpu.VMEM((2, 1, nq), F32),
        pltpu.VMEM((2, KV_CHUNK, nq), BF16),
        pltpu.VMEM((2, 1, nq), F32),
        pltpu.VMEM((1, nq), F32),
        pltpu.VMEM((rows, nq), F32),
    ]


def _key_chunk(k_ref, j):
    return k_ref[pl.ds(pl.multiple_of(j * KV_CHUNK, KV_CHUNK), KV_CHUNK), :]


def _mla_store(o_t, o_ref):
    o_ref[...] = jnp.concatenate([o_t, jnp.zeros((LANE - MLA_V, o_t.shape[1]), F32)], axis=0).T.astype(BF16)


def _mla_finalize(acc_sc, o_ref):
    acc = acc_sc[...]
    _mla_store(acc[:MLA_V] / acc[MLA_V:MLA_V + 1], o_ref)


def _flash_mla_kernel(q_ref, k_ref, vt_ref, o_ref, *scratch, n_kv):
    acc_sc = scratch[-1]
    _flash_core(lambda j: _dot_nt(_key_chunk(k_ref, j), q_ref[...]), vt_ref, *scratch, n_kv)
    _mla_finalize(acc_sc, o_ref)


def _flash_mla(q, k, vt, tq):
    b, s, hw = q.shape
    n_kv = s // KV_CHUNK
    kern = functools.partial(_flash_mla_kernel, n_kv=n_kv)
    return pl.pallas_call(
        kern,
        grid=(b, MLA_HEADS, s // tq),
        in_specs=[
            pl.BlockSpec((None, tq, LANE), lambda bi, h, i: (bi, i, h)),
            pl.BlockSpec((None, s, LANE), lambda bi, h, i: (bi, 0, h)),
            pl.BlockSpec((None, None, n_kv, MLA_VT_ROWS, KV_CHUNK), lambda bi, h, i: (bi, h, 0, 0, 0)),
        ],
        out_specs=pl.BlockSpec((None, tq, LANE), lambda bi, h, i: (bi, i, h)),
        out_shape=jax.ShapeDtypeStruct((b, s, hw), BF16),
        scratch_shapes=_flash_scratch(MLA_VT_ROWS, tq),
        compiler_params=_params(("parallel", "parallel", "arbitrary")),
        name="flash_mla",
    )(q, k, vt)


def _static_core(first_probs, probs, chunk_of, vt_ref, p_buf, acc_sc, n_kv):
    assert n_kv >= 2 and n_kv % 2 == 0

    def values(t, slot, first=False):
        pv = _dot(vt_ref[chunk_of(t)], p_buf[slot])
        acc_sc[...] = pv if first else acc_sc[...] + pv

    p_buf[0] = first_probs()
    p_buf[1] = probs(1)
    values(0, 0, first=True)

    for t in range(1, n_kv - 1, 2):
        p_buf[0] = probs(t + 1)
        values(t, 1)
        p_buf[1] = probs(t + 2)
        values(t + 1, 0)
    values(n_kv - 1, 1)


def _static_scratch(rows, nq):
    return [pltpu.VMEM((2, KV_CHUNK, nq), BF16), pltpu.VMEM((rows, nq), F32)]


def _query_tile_loop(n_tiles, tile_fn):
    def body(i, carry):
        tile_fn(i)
        return carry
    lax.fori_loop(0, n_tiles, body, 0)


def _flash_mla_static_kernel(q_ref, k_ref, vt_ref, o_ref, p_buf, acc_sc, *, tq, n_kv):
    def tile(i):
        rows = pl.ds(pl.multiple_of(i * tq, tq), tq)
        probs = lambda t: jnp.exp2(_dot_nt(_key_chunk(k_ref, t), q_ref[rows, :])).astype(BF16)
        _static_core(lambda: probs(0), probs, lambda t: t, vt_ref, p_buf, acc_sc, n_kv)
        _mla_finalize(acc_sc, o_ref.at[rows, :])

    _query_tile_loop(q_ref.shape[0] // tq, tile)


def _flash_mla_static(q, k, vt, tq):
    b, s, hw = q.shape
    n_kv = s // KV_CHUNK
    kern = functools.partial(_flash_mla_static_kernel, tq=tq, n_kv=n_kv)
    head_slab = pl.BlockSpec((None, s, LANE), lambda bi, h: (bi, 0, h))
    return pl.pallas_call(
        kern,
        grid=(b, MLA_HEADS),
        in_specs=[
            head_slab,
            head_slab,
            pl.BlockSpec((None, None, n_kv, MLA_VT_ROWS, KV_CHUNK), lambda bi, h: (bi, h, 0, 0, 0)),
        ],
        out_specs=head_slab,
        out_shape=jax.ShapeDtypeStruct((b, s, hw), BF16),
        scratch_shapes=_static_scratch(MLA_VT_ROWS, tq),
        compiler_params=_params(("parallel", "parallel")),
        name="flash_mla_static",
    )(q, k, vt)


def _pos_lanes(pos0, rows, first_lane):
    pos = pos0 + lax.broadcasted_iota(jnp.int32, (rows, LANE), 0)
    lane = lax.broadcasted_iota(jnp.int32, (rows, LANE), 1) - first_lane
    hi = lax.shift_right_logical(pos, int(math.log2(POS_SPLIT))).astype(F32)
    lo = (pos & (POS_SPLIT - 1)).astype(F32)
    p = DIFF_AUG_PARTS
    in_hi = (lane >= 0) & (lane < p)
    in_lo = (lane >= p) & (lane < 2 * p)
    return jnp.where(in_hi, hi, jnp.where(in_lo, lo, 0.0))


def _diff_pre_kernel(x_ref, ng_ref, w_ref, wvt_ref, gq_ref, gk_ref, kaug_ref, q_ref, k_ref, vt_ref):
    h = _rms(x_ref[...], ng_ref[...]).astype(BF16)
    qkv = _dot(h, w_ref[...])
    v_t = _dot_nt(wvt_ref[...], h)
    hw = DIFF_HEADS * LANE
    lo = lax.broadcasted_iota(jnp.int32, (1, LANE), 1) < DIFF_HD
    q_scale = (DIFF_HD ** -0.5) * LOG2E
    tm = qkv.shape[0]
    k_pos = _pos_lanes(pl.program_id(1) * tm, tm, 0)
    ones_row = (lax.broadcasted_iota(jnp.int32, (DIFF_VT_ROWS, tm), 0) == LANE).astype(F32)

    def half_norm(t, g):
        t2 = t * t
        ss_lo = jnp.sum(jnp.where(lo, t2, 0.0), axis=-1, keepdims=True)
        ss_hi = jnp.sum(jnp.where(lo, 0.0, t2), axis=-1, keepdims=True)
        r = lax.rsqrt(jnp.where(lo, ss_lo, ss_hi) * (1.0 / DIFF_HD) + EPS)
        return t * r * g

    for hd in range(DIFF_HEADS):
        sl = slice(hd * LANE, (hd + 1) * LANE)
        q_ref[:, sl] = (half_norm(qkv[:, sl], gq_ref[...]) * q_scale).astype(BF16)
        ks = slice(2 * hd * LANE, (2 * hd + 1) * LANE)
        k_ref[:, ks] = half_norm(qkv[:, hw + hd * LANE:hw + (hd + 1) * LANE], gk_ref[...]).astype(BF16)
        k_ref[:, (2 * hd + 1) * LANE:(2 * hd + 2) * LANE] = (k_pos + kaug_ref[hd]).astype(BF16)
        _store_vt(vt_ref, hd, v_t[hd * DIFF_VT_ROWS:(hd + 1) * DIFF_VT_ROWS] + ones_row)


def _slope_pieces():
    slopes = 2.0 ** (-8.0 * jnp.arange(1, DIFF_HEADS + 1, dtype=F32) / DIFF_HEADS) * LOG2E
    pieces, rest = [], slopes
    for _ in range(DIFF_AUG_PARTS):
        piece = rest.astype(BF16).astype(F32)
        pieces.append(piece)
        rest = rest - piece
    return jnp.stack(pieces, axis=1)


def _diff_bias_tables(bound):
    p = DIFF_AUG_PARTS
    pieces = _slope_pieces()
    zeros = jnp.zeros((DIFF_HEADS, LANE - 4 * p - 1), F32)
    one = jnp.ones((DIFF_HEADS, 1), F32)
    zp = jnp.zeros((DIFF_HEADS, p), F32)
    k_tab = jnp.concatenate([zp, zp, POS_SPLIT * pieces, pieces, one, zeros], axis=1)[:, None, :]
    shift = -bound * one
    after = jnp.concatenate([-POS_SPLIT * pieces, -pieces, zp, zp, shift, zeros], axis=1)
    before = jnp.concatenate([POS_SPLIT * pieces, pieces, zp, zp, shift, zeros], axis=1)
    diag = jnp.concatenate([zp, zp, zp, zp, shift, zeros], axis=1)
    return k_tab, jnp.stack([after, before, diag], axis=1)


def _diff_pre(x, norm_g, w_qkv, q_g, k_g, k_tab, tm):
    b, s, _ = x.shape
    hw = DIFF_HEADS * LANE
    out = jax.ShapeDtypeStruct((b, s, hw), BF16)
    out_k = jax.ShapeDtypeStruct((b, s, 2 * hw), BF16)
    vt_shape, vt_spec = _vt_shape_and_spec(b, DIFF_HEADS, DIFF_VT_ROWS, s, tm)
    tile = lambda bi, i: (bi, i, 0)
    wvt = w_qkv[:, 2 * hw:].T.reshape(DIFF_HEADS, LANE, D_MODEL)
    wvt = jnp.concatenate([wvt, jnp.zeros((DIFF_HEADS, DIFF_VT_ROWS - LANE, D_MODEL), F32)], axis=1)
    wvt = wvt.reshape(DIFF_HEADS * DIFF_VT_ROWS, D_MODEL).astype(BF16)
    weights = (norm_g[None, :], w_qkv[:, :2 * hw].astype(BF16), wvt, q_g.reshape(1, LANE),
               k_g.reshape(1, LANE), k_tab)
    return pl.pallas_call(
        _diff_pre_kernel,
        grid=(b, s // tm),
        in_specs=[pl.BlockSpec((None, tm, D_MODEL), tile)] + [_const_spec(w.shape) for w in weights],
        out_specs=[pl.BlockSpec((None, tm, hw), tile), pl.BlockSpec((None, tm, 2 * hw), tile), vt_spec],
        out_shape=[out, out_k, vt_shape],
        compiler_params=_params(("parallel", "parallel")),
        name="diff_pre",
    )(x, *weights)


def _flash_diff_kernel(slope_ref, q_ref, k_ref, vt_ref, dmat_ref, lam_ref, subg_ref, o_ref,
                       qcat_sc, *scratch, tq, n_kv, lambda_init):
    acc_sc = scratch[-1]
    lo = lax.broadcasted_iota(jnp.int32, (1, LANE), 1) < DIFF_HD
    q = q_ref[...]
    zero = jnp.zeros_like(q)
    qcat_sc[:tq] = jnp.where(lo, q, zero)
    qcat_sc[tq:] = jnp.where(lo, zero, q)
    neg_slope = slope_ref[pl.program_id(1)]
    q0 = pl.program_id(2) * tq

    def scores(j):
        delta = (j * KV_CHUNK - q0).astype(F32)
        bias = jnp.abs(dmat_ref[...] + delta) * neg_slope
        return _dot_nt(_key_chunk(k_ref, j), qcat_sc[...]) + jnp.concatenate([bias, bias], axis=1)

    _flash_core(scores, vt_ref, *scratch, n_kv)
    _diff_finalize(acc_sc, lam_ref, subg_ref, o_ref, tq, lambda_init)


def _diff_finalize(acc_sc, lam_ref, subg_ref, o_ref, tq, lambda_init):
    acc = acc_sc[...]
    _diff_store(acc[:LANE] / acc[LANE:LANE + 1], lam_ref, subg_ref, o_ref, tq, lambda_init)


def _diff_store(o, lam_ref, subg_ref, o_ref, tq, lambda_init):
    lp = lam_ref[...]
    lam = (jnp.exp(jnp.sum(lp[0:1] * lp[1:2], axis=-1, keepdims=True))
           - jnp.exp(jnp.sum(lp[2:3] * lp[3:4], axis=-1, keepdims=True)) + lambda_init)
    o = (o[:, :tq] - lam * o[:, tq:]).T
    o = _rms(o, subg_ref[...]) * (1.0 - lambda_init)
    o_ref[...] = o.astype(BF16)


def _flash_diff_static_kernel(slope_ref, q_ref, k_ref, vt_ref, dmat_ref, lam_ref, subg_ref, qtab_ref,
                              o_ref, qz_sc, aug_sc, p_buf, acc_sc, *, tq, n_kv, lambda_init):
    tile = functools.partial(_flash_diff_static_tile, slope_ref, q_ref, k_ref, vt_ref, dmat_ref, lam_ref,
                             subg_ref, qtab_ref, o_ref, qz_sc, aug_sc, p_buf, acc_sc, tq, n_kv,
                             lambda_init)
    _query_tile_loop(q_ref.shape[0] // tq, tile)


def _flash_diff_static_tile(slope_ref, q_ref, k_ref, vt_ref, dmat_ref, lam_ref, subg_ref, qtab_ref,
                            o_ref, qz_sc, aug_sc, p_buf, acc_sc, tq, n_kv, lambda_init, i):
    lo = lax.broadcasted_iota(jnp.int32, (1, LANE), 1) < DIFF_HD
    q0 = pl.multiple_of(i * tq, tq)
    q = q_ref[pl.ds(q0, tq), :]
    zero = jnp.zeros_like(q)
    qz_sc[:tq] = jnp.where(lo, q, zero)
    qz_sc[tq:] = jnp.where(lo, zero, q)
    n_d = tq // KV_CHUNK
    q_pos = _pos_lanes(q0, tq, 2 * DIFF_AUG_PARTS)
    tab = qtab_ref[...]
    for kind, sign in enumerate((1.0, -1.0, 0.0)):
        aug_sc[kind] = (tab[kind:kind + 1] + sign * q_pos).astype(BF16)
    jd = lax.shift_right_logical(q0, int(math.log2(KV_CHUNK)))
    neg_slope = slope_ref[pl.program_id(1)]

    def with_bias_columns(aug):
        return jnp.concatenate([qz_sc[...], jnp.concatenate([aug, aug], axis=0)], axis=1)

    def diag_probs(d):
        kinds = [2 if e == d else (1 if e > d else 0) for e in range(n_d)]
        aug = jnp.concatenate([aug_sc[kind, e * KV_CHUNK:(e + 1) * KV_CHUNK, :]
                               for e, kind in enumerate(kinds)], axis=0)
        s = _dot_nt(_key_chunk(k_ref, jd + d), with_bias_columns(aug))
        bias = jnp.abs(dmat_ref[...]) * neg_slope
        cols = []
        for c in range(2):
            for e in range(n_d):
                blk = s[:, c * tq + e * KV_CHUNK:c * tq + (e + 1) * KV_CHUNK]
                cols.append(blk + bias if e == d else blk)
        return jnp.exp2(jnp.concatenate(cols, axis=1)).astype(BF16)

    def chunk_of(t):
        r = t - n_d
        return jnp.where(t < n_d, jd + t, r + n_d * (r >= jd).astype(jnp.int32))

    def probs(t):
        if isinstance(t, int) and t < n_d:
            return diag_probs(t)
        j = chunk_of(t)
        before = (j < jd).astype(jnp.int32)
        return jnp.exp2(_dot_nt(_key_chunk(k_ref, j), with_bias_columns(aug_sc[before]))).astype(BF16)

    _static_core(lambda: diag_probs(0), probs, chunk_of, vt_ref, p_buf, acc_sc, n_kv)
    _diff_finalize(acc_sc, lam_ref, subg_ref, o_ref.at[pl.ds(q0, tq), :], tq, lambda_init)


def _flash_diff(q, k, vt, lam_p, sub_g, q_tab, lambda_init, tq, static_shift):
    b, s, hw = q.shape
    n_kv = s // KV_CHUNK
    slopes = 2.0 ** (-8.0 * jnp.arange(1, DIFF_HEADS + 1, dtype=F32) / DIFF_HEADS)
    neg_slopes = -slopes * LOG2E
    d_cols = KV_CHUNK if static_shift else tq
    dmat = (jnp.arange(KV_CHUNK, dtype=F32)[:, None] - jnp.arange(d_cols, dtype=F32)[None, :])
    const = dict(pipeline_mode=pl.Buffered(1))
    origin = lambda *_: (0, 0)
    vt_spec = pl.BlockSpec((None, None, n_kv, DIFF_VT_ROWS, KV_CHUNK), lambda bi, h, *_: (bi, h, 0, 0, 0))
    small_specs = [
        pl.BlockSpec((KV_CHUNK, d_cols), origin, **const),
        pl.BlockSpec((4, DIFF_HD), origin, **const),
        pl.BlockSpec((1, LANE), origin, **const),
    ]
    args = [neg_slopes, q, k, vt, dmat, lam_p, sub_g[None, :]]
    if static_shift:
        assert tq % KV_CHUNK == 0
        kern = functools.partial(_flash_diff_static_kernel, tq=tq, n_kv=n_kv, lambda_init=lambda_init)
        grid = (b, DIFF_HEADS)
        q_spec = pl.BlockSpec((None, s, LANE), lambda bi, h, sl: (bi, 0, h))
        k_spec = pl.BlockSpec((None, s, 2 * LANE), lambda bi, h, sl: (bi, 0, h))
        in_specs = [q_spec, k_spec, vt_spec] + small_specs + [
            pl.BlockSpec((None, 3, LANE), lambda bi, h, sl: (h, 0, 0))]
        args.append(q_tab)
        scratch = ([pltpu.VMEM((2 * tq, LANE), BF16), pltpu.VMEM((3, tq, LANE), BF16)]
                   + _static_scratch(DIFF_VT_ROWS, 2 * tq))
        name = "flash_diff_static"
    else:
        kern = functools.partial(_flash_diff_kernel, tq=tq, n_kv=n_kv, lambda_init=lambda_init)
        grid = (b, DIFF_HEADS, s // tq)
        q_spec = pl.BlockSpec((None, tq, LANE), lambda bi, h, i, sl: (bi, i, h))
        k_spec = pl.BlockSpec((None, s, LANE), lambda bi, h, i, sl: (bi, 0, 2 * h))
        in_specs = [q_spec, k_spec, vt_spec] + small_specs
        scratch = [pltpu.VMEM((2 * tq, LANE), BF16)] + _flash_scratch(DIFF_VT_ROWS, 2 * tq)
        name = "flash_diff"
    grid_spec = pltpu.PrefetchScalarGridSpec(
        num_scalar_prefetch=1, grid=grid, in_specs=in_specs, out_specs=q_spec, scratch_shapes=scratch)
    return pl.pallas_call(
        kern,
        grid_spec=grid_spec,
        out_shape=jax.ShapeDtypeStruct((b, s, hw), BF16),
        compiler_params=_params(("parallel", "parallel") + (("arbitrary",) if len(grid) == 3 else ())),
        name=name,
    )(*args)


def _post_xattn_kernel(x_ref, om_ref, wom_ref, ng_ref, wq_ref, qg_ref, k_ref, v_ref, wo_ref, b_ref,
                       y_ref, o_sc, *, static_shift):
    x1 = x_ref[...] + _dot(om_ref[...], wom_ref[...])
    h = _rms(x1, ng_ref[...]).astype(BF16)
    q = _dot(h, wq_ref[...])
    qg = qg_ref[...] * ((XA_HD ** -0.5) * LOG2E)
    for hd in range(XA_HEADS):
        sl = slice(hd * XA_HD, (hd + 1) * XA_HD)
        qh = _rms(q[:, sl], qg).astype(BF16)
        s = _dot_nt(qh, k_ref[:, sl])
        shift = b_ref[...] if static_shift else jnp.max(s, axis=-1, keepdims=True)
        p = jnp.exp2(s - shift)
        l = jnp.sum(p, axis=-1, keepdims=True)
        o_sc[:, sl] = (_dot(p.astype(BF16), v_ref[:, sl]) / l).astype(BF16)
    y_ref[...] = x1 + _dot(o_sc[...], wo_ref[...])


def _post_xattn(x, o_mix, w_o_mix, norm_g, w_q, q_g, k_mem, v_mem, w_o, bound, *, layer, tm, static_shift):
    b, s, _ = x.shape
    tile = lambda bi, i: (bi, i, 0)
    mem_spec = pl.BlockSpec((None, None, N_MEM, D_MODEL), lambda bi, i: (layer, bi, 0, 0))
    sq = (D_MODEL, D_MODEL)
    return pl.pallas_call(
        functools.partial(_post_xattn_kernel, static_shift=static_shift),
        grid=(b, s // tm),
        in_specs=[
            pl.BlockSpec((None, tm, D_MODEL), tile),
            pl.BlockSpec((None, tm, o_mix.shape[-1]), tile),
            _const_spec(w_o_mix.shape),
            _const_spec((1, D_MODEL)),
            _const_spec(sq),
            _const_spec((1, XA_HD)),
            mem_spec,
            mem_spec,
            _const_spec(sq),
            _const_spec((1, 1)),
        ],
        out_specs=pl.BlockSpec((None, tm, D_MODEL), tile),
        out_shape=jax.ShapeDtypeStruct(x.shape, F32),
        scratch_shapes=[pltpu.VMEM((tm, D_MODEL), BF16)],
        compiler_params=_params(("parallel", "parallel")),
        name="post_xattn",
    )(x, o_mix, w_o_mix, norm_g[None, :], w_q.astype(BF16), q_g[None, :], k_mem, v_mem,
      w_o.astype(BF16), jnp.reshape(bound, (1, 1)).astype(F32))


def _ffn_kernel(x_ref, prev_ref, next_ref, ng_ref, wg_ref, wu_ref, cw_ref, cb_ref, wd_ref, y_ref,
                *, tm, n_chunks):
    i = pl.program_id(1)
    keep_prev = (i > 0).astype(F32)
    keep_next = (i < pl.num_programs(1) - 1).astype(F32)
    x = x_ref[...]
    xe = jnp.concatenate([prev_ref[...] * keep_prev, x, next_ref[...] * keep_next], axis=0)
    he = _rms(xe, ng_ref[...]).astype(BF16)
    hc = he[HALO:HALO + tm]
    cw = cw_ref[...]
    cb = cb_ref[...]
    y = x
    n_tiles = D_FF // MXU_DIM
    bounds = [MXU_DIM * ((n_tiles * c + n_chunks - 1) // n_chunks) for c in range(n_chunks)] + [D_FF]
    for c in range(n_chunks):
        cs = slice(bounds[c], bounds[c + 1])
        ge = _dot(he, wg_ref[:, cs])
        u = _dot(hc, wu_ref[:, cs])
        g = (ge[HALO - 1:HALO - 1 + tm] * cw[0:1, cs] + ge[HALO:HALO + tm] * cw[1:2, cs]
             + ge[HALO + 1:HALO + 1 + tm] * cw[2:3, cs] + cb[:, cs])
        act = (g * jax.nn.sigmoid(g) * u).astype(BF16)
        y = y + _dot(act, wd_ref[cs, :])
    y_ref[...] = y


def _ffn(x, norm_g, w_gu, conv_w, conv_b, w_down, tm, n_chunks=1):
    b, s, _ = x.shape
    nh = tm // HALO
    last = s // HALO - 1
    w_g = w_gu[:, :D_FF].astype(BF16)
    w_u = w_gu[:, D_FF:].astype(BF16)
    weights = (norm_g[None, :], w_g, w_u, conv_w, conv_b[None, :], w_down.astype(BF16))
    kern = functools.partial(_ffn_kernel, tm=tm, n_chunks=n_chunks)
    return pl.pallas_call(
        kern,
        grid=(b, s // tm),
        in_specs=[
            pl.BlockSpec((None, tm, D_MODEL), lambda bi, i: (bi, i, 0)),
            pl.BlockSpec((None, HALO, D_MODEL), lambda bi, i: (bi, jnp.maximum(i * nh - 1, 0), 0)),
            pl.BlockSpec((None, HALO, D_MODEL), lambda bi, i: (bi, jnp.minimum((i + 1) * nh, last), 0)),
        ] + [_const_spec(w.shape) for w in weights],
        out_specs=pl.BlockSpec((None, tm, D_MODEL), lambda bi, i: (bi, i, 0)),
        out_shape=jax.ShapeDtypeStruct(x.shape, F32),
        compiler_params=_params(("parallel", "parallel")),
        name="ffn",
    )(x, x, x, *weights)


def _tiles(s):
    return dict(tm=min(512, s), tm_pre=min(1024, s), tm_post=min(1024, s),
                tq_mla=min(4096, s), tq_diff=min(2048, s),
                tq_mla_online=min(1024, s), tq_diff_online=min(512, s))


def _mixer_weights(mla_p, diff_p):
    layers = []
    for i in range(DEPTH):
        j = i // 2
        if i % 2 == 0:
            weights, w_o_mix, bound = _mla_weights(*[p[j] for p in mla_p])
            layers.append(dict(weights=weights, w_o_mix=w_o_mix, bound=bound))
        else:
            norm_g, w_qkv, q_g, k_g, lam_p, sub_g, w_o = [p[j] for p in diff_p]
            bound = _score_bound(DIFF_HD, q_g, k_g)
            k_tab, q_tab = _diff_bias_tables(bound)
            layers.append(dict(pre=(norm_g, w_qkv, q_g, k_g, k_tab), lam_p=lam_p, sub_g=sub_g,
                               q_tab=q_tab, w_o_mix=w_o.astype(BF16), bound=bound))
    return layers


def _trunk(x, mem, mixers, xa_p, ffn_p):
    s = x.shape[1]
    assert s % KV_CHUNK == 0
    t = _tiles(s)
    xa_norm, xa_mem_norm, xa_w_q, xa_w_kv, xa_q_norm, xa_k_norm, xa_w_o = xa_p
    k_mem, v_mem = _mem_kv(mem, xa_mem_norm, xa_w_kv, xa_k_norm)
    cos, sin = _rope_tables(s)
    for i in range(DEPTH):
        mx = mixers[i]
        static_ok = mx["bound"] <= MAX_STATIC_BOUND
        if i % 2 == 0:
            q, k, vt = _mla_pre(x, mx["weights"], cos, sin, t["tm_pre"])
            o_mix = lax.cond(static_ok,
                             lambda q, k, vt: _flash_mla_static(q, k, vt, t["tq_mla"]),
                             lambda q, k, vt: _flash_mla(q, k, vt, t["tq_mla_online"]), q, k, vt)
        else:
            lambda_init = 0.8 - 0.6 * math.exp(-0.3 * i)
            q, k, vt = _diff_pre(x, *mx["pre"], t["tm_pre"])
            flash = lambda static: functools.partial(
                _flash_diff, lambda_init=lambda_init, static_shift=static,
                tq=t["tq_diff"] if static else t["tq_diff_online"])
            o_mix = lax.cond(static_ok, flash(True), flash(False),
                             q, k, vt, mx["lam_p"], mx["sub_g"], mx["q_tab"])
        xa_bound = _score_bound(XA_HD, xa_q_norm[i], xa_k_norm[i])
        post = lambda static: functools.partial(_post_xattn, layer=i, tm=t["tm_post"], static_shift=static)
        x = lax.cond(xa_bound <= MAX_STATIC_BOUND, post(True), post(False),
                     x, o_mix, mx["w_o_mix"], xa_norm[i], xa_w_q[i], xa_q_norm[i], k_mem, v_mem,
                     xa_w_o[i], xa_bound)
        x = _ffn(x, *[p[i] for p in ffn_p], t["tm"])
    return x


def kernel(x_prompt, x_sample, mem_prompt, mem_sample, mla_norm, mla_w_down, mla_q_lat_norm, mla_kv_lat_norm, mla_w_uq, mla_w_ukv, mla_q_norm, mla_k_norm, mla_w_o, diff_norm, diff_w_qkv, diff_q_norm, diff_k_norm, diff_lambda, diff_sub_norm, diff_w_o, xa_norm, xa_mem_norm, xa_w_q, xa_w_kv, xa_q_norm, xa_k_norm, xa_w_o, ffn_norm, ffn_w_gu, ffn_conv_w, ffn_conv_b, ffn_w_down):
    mla_p = (mla_norm, mla_w_down, mla_q_lat_norm, mla_kv_lat_norm, mla_w_uq, mla_w_ukv,
             mla_q_norm, mla_k_norm, mla_w_o)
    diff_p = (diff_norm, diff_w_qkv, diff_q_norm, diff_k_norm, diff_lambda, diff_sub_norm, diff_w_o)
    xa_p = (xa_norm, xa_mem_norm, xa_w_q, xa_w_kv, xa_q_norm, xa_k_norm, xa_w_o)
    ffn_p = (ffn_norm, ffn_w_gu, ffn_conv_w, ffn_conv_b, ffn_w_down)
    mixers = _mixer_weights(mla_p, diff_p)
    y_prompt = _trunk(x_prompt, mem_prompt, mixers, xa_p, ffn_p)
    y_sample = _trunk(x_sample, mem_sample, mixers, xa_p, ffn_p)
    return (y_prompt, y_sample)
```
